```python
import math
import jax, jax.numpy as jnp
from jax import lax
import numpy as np

D_MODEL = 1024
BATCH = 4
SEQ = 8192
DEPTH = 2

N_MIXERS = 2
CONV_WIDTH = 31
N_HEADS = 8
HEAD_DIM = 64
V_DIM = 2 * HEAD_DIM
D_FF = 4 * D_MODEL
REL_BUCKETS = 32
REL_MAX_DIST = 128
Q_BLOCK = 128
ALPHA = (2 * DEPTH) ** 0.25
BETA = (8 * DEPTH) ** -0.25
LN_EPS = 1e-5
N_CONV = (DEPTH + 1) // 2
N_ATTN = DEPTH // 2

kernel_name = 'hybrid_conformer_conv_diff_attn_deepnorm_adaln'


def layer_norm(x, g, b):
    xf = x.astype(jnp.float32)
    mu = jnp.mean(xf, axis=-1, keepdims=True)
    var = jnp.mean(jnp.square(xf - mu), axis=-1, keepdims=True)
    return ((xf - mu) * lax.rsqrt(var + LN_EPS)).astype(x.dtype) * g + b


def rms_norm(x, g):
    xf = x.astype(jnp.float32)
    return (xf * lax.rsqrt(jnp.mean(jnp.square(xf), axis=-1, keepdims=True) + LN_EPS)).astype(x.dtype) * g


def ada_mod(c, w, b):
    m = (jax.nn.silu(c) @ w + b)[:, None, :]
    shift, scale, gate = jnp.split(m, 3, axis=-1)
    return shift, scale, gate


def conv_module(h, w_pw1, b_pw1, w_dw, b_dw, g_cn, b_cn, w_pw2, b_pw2):
    u = jax.nn.glu(h @ w_pw1 + b_pw1, axis=-1)
    u = jnp.pad(u, ((0, 0), (CONV_WIDTH - 1, 0), (0, 0)))
    u = lax.conv_general_dilated(
        u, w_dw[:, None, :], window_strides=(1,), padding='VALID',
        dimension_numbers=('NWC', 'WIO', 'NWC'),
        feature_group_count=D_MODEL) + b_dw
    u = jax.nn.silu(layer_norm(u, g_cn, b_cn))
    return u @ w_pw2 + b_pw2


def t5_bucket(rel):
    n = jnp.maximum(rel, 0)
    max_exact = REL_BUCKETS // 2
    nf = jnp.maximum(n, 1).astype(jnp.float32)
    large = max_exact + (jnp.log(nf / max_exact) / math.log(REL_MAX_DIST / max_exact)
                         * (REL_BUCKETS - max_exact)).astype(jnp.int32)
    large = jnp.minimum(large, REL_BUCKETS - 1)
    return jnp.where(n < max_exact, n, large)


def diff_attention(h, w_qkv, lam_q1, lam_k1, lam_q2, lam_k2, g_sub, w_o, rel_bias, lambda_init):
    B, S, _ = h.shape
    q, k, v = jnp.split(h @ w_qkv, 3, axis=-1)
    q = q.reshape(B, S, N_HEADS, 2, HEAD_DIM) * (HEAD_DIM ** -0.5)
    k = k.reshape(B, S, N_HEADS, 2, HEAD_DIM)
    v = v.reshape(B, S, N_HEADS, V_DIM)
    f32 = jnp.float32
    lam = (jnp.exp(jnp.sum(lam_q1.astype(f32) * lam_k1.astype(f32)))
           - jnp.exp(jnp.sum(lam_q2.astype(f32) * lam_k2.astype(f32))) + lambda_init)
    n_blocks = S // Q_BLOCK
    qb = jnp.moveaxis(q.reshape(B, n_blocks, Q_BLOCK, N_HEADS, 2, HEAD_DIM), 1, 0)
    k_pos = jnp.arange(S)

    def block(args):
        q_blk, blk_idx = args
        q_pos = blk_idx * Q_BLOCK + jnp.arange(Q_BLOCK)
        rel = q_pos[:, None] - k_pos[None, :]
        bias = jnp.moveaxis(rel_bias[t5_bucket(rel)], -1, 0).astype(f32)
        logits = jnp.einsum('bqhmd,bkhmd->bhmqk', q_blk, k).astype(f32) + bias[None, :, None]
        logits = jnp.where(rel[None, None, None] >= 0, logits, -jnp.inf)
        p = jax.nn.softmax(logits, axis=-1)
        attn = p[:, :, 0] - lam * p[:, :, 1]
        return jnp.einsum('bhqk,bkhe->bqhe', attn.astype(v.dtype), v)

    out = lax.map(block, (qb, jnp.arange(n_blocks)))
    out = jnp.moveaxis(out, 0, 1).reshape(B, S, N_HEADS, V_DIM)
    out = rms_norm(out, g_sub) * (1.0 - lambda_init)
    return out.reshape(B, S, N_HEADS * V_DIM) @ w_o


def sq_relu_mlp(h, w1, w2):
    return jnp.square(jax.nn.relu(h @ w1)) @ w2


def _normal(key, shape, std):
    return std * jax.random.normal(key, shape, dtype=jnp.float32)


def setup_inputs(seed: int = 0) -> dict:
    key = jax.random.key(seed)
    ks = jax.random.split(key, 40)
    D = D_MODEL
    s_in = D ** -0.5

    def mod_bias(k1, k2, n):
        return jnp.concatenate([_normal(k1, (n, 2 * D), 0.01),
                                1.0 + _normal(k2, (n, D), 0.01)], axis=-1)

    qk_w = _normal(ks[14], (N_ATTN, D, 2 * N_HEADS * 2 * HEAD_DIM), s_in)
    v_w = _normal(ks[15], (N_ATTN, D, N_HEADS * V_DIM), BETA * s_in)
    return {
        'x': _normal(ks[0], (BATCH, SEQ, D), 1.0),
        'c': _normal(ks[1], (BATCH, D), 1.0),
        'conv_mod_w': _normal(ks[2], (N_CONV, D, 3 * D), 0.2 * s_in),
        'conv_mod_b': mod_bias(ks[3], ks[4], N_CONV),
        'conv_pw1_w': _normal(ks[5], (N_CONV, D, 2 * D), BETA * s_in),
        'conv_pw1_b': _normal(ks[6], (N_CONV, 2 * D), 0.01),
        'conv_dw_w': _normal(ks[7], (N_CONV, CONV_WIDTH, D), CONV_WIDTH ** -0.5),
        'conv_dw_b': _normal(ks[8], (N_CONV, D), 0.01),
        'conv_norm_g': 1.0 + _normal(ks[9], (N_CONV, D), 0.01),
        'conv_norm_b': _normal(ks[10], (N_CONV, D), 0.01),
        'conv_pw2_w': _normal(ks[11], (N_CONV, D, D), BETA * s_in),
        'conv_pw2_b': _normal(ks[12], (N_CONV, D), 0.01),
        'attn_mod_w': _normal(ks[13], (N_ATTN, D, 3 * D), 0.2 * s_in),
        'attn_mod_b': mod_bias(ks[16], ks[17], N_ATTN),
        'attn_qkv_w': jnp.concatenate([qk_w, v_w], axis=-1),
        'attn_lam_q1': _normal(ks[18], (N_ATTN, HEAD_DIM), 0.1),
        'attn_lam_k1': _normal(ks[19], (N_ATTN, HEAD_DIM), 0.1),
        'attn_lam_q2': _normal(ks[20], (N_ATTN, HEAD_DIM), 0.1),
        'attn_lam_k2': _normal(ks[21], (N_ATTN, HEAD_DIM), 0.1),
        'attn_subln_g': 1.0 + _normal(ks[22], (N_ATTN, V_DIM), 0.01),
        'attn_out_w': _normal(ks[23], (N_ATTN, N_HEADS * V_DIM, D), BETA * s_in),
        'rel_bias': _normal(ks[24], (REL_BUCKETS, N_HEADS), 0.5),
        'mlp_mod_w': _normal(ks[25], (DEPTH, D, 3 * D), 0.2 * s_in),
        'mlp_mod_b': mod_bias(ks[26], ks[27], DEPTH),
        'mlp_w1': _normal(ks[28], (DEPTH, D, D_FF), BETA * s_in),
        'mlp_w2': _normal(ks[29], (DEPTH, D_FF, D), BETA * D_FF ** -0.5),
        'post_mix_g': 1.0 + _normal(ks[30], (DEPTH, D), 0.01),
        'post_mix_b': _normal(ks[31], (DEPTH, D), 0.01),
        'post_mlp_g': 1.0 + _normal(ks[32], (DEPTH, D), 0.01),
        'post_mlp_b': _normal(ks[33], (DEPTH, D), 0.01),
    }


def reference(x, c, conv_mod_w, conv_mod_b, conv_pw1_w, conv_pw1_b, conv_dw_w, conv_dw_b,
              conv_norm_g, conv_norm_b, conv_pw2_w, conv_pw2_b, attn_mod_w, attn_mod_b,
              attn_qkv_w, attn_lam_q1, attn_lam_k1, attn_lam_q2, attn_lam_k2, attn_subln_g,
              attn_out_w, rel_bias, mlp_mod_w, mlp_mod_b, mlp_w1, mlp_w2,
              post_mix_g, post_mix_b, post_mlp_g, post_mlp_b):
    for i in range(DEPTH):
        j = i // N_MIXERS
        if i % N_MIXERS == 0:
            shift, scale, gate = ada_mod(c, conv_mod_w[j], conv_mod_b[j])
            y = conv_module(x * (1 + scale) + shift, conv_pw1_w[j], conv_pw1_b[j],
                            conv_dw_w[j], conv_dw_b[j], conv_norm_g[j], conv_norm_b[j],
                            conv_pw2_w[j], conv_pw2_b[j])
        else:
            lambda_init = 0.8 - 0.6 * math.exp(-0.3 * i)
            shift, scale, gate = ada_mod(c, attn_mod_w[j], attn_mod_b[j])
            y = diff_attention(x * (1 + scale) + shift, attn_qkv_w[j], attn_lam_q1[j],
                               attn_lam_k1[j], attn_lam_q2[j], attn_lam_k2[j],
                               attn_subln_g[j], attn_out_w[j], rel_bias, lambda_init)
        x = layer_norm(ALPHA * x + gate * y, post_mix_g[i], post_mix_b[i])
        shift, scale, gate = ada_mod(c, mlp_mod_w[i], mlp_mod_b[i])
        y = sq_relu_mlp(x * (1 + scale) + shift, mlp_w1[i], mlp_w2[i])
        x = layer_norm(ALPHA * x + gate * y, post_mlp_g[i], post_mlp_b[i])
    return x
```

```python
import functools
import math

import jax
import jax.numpy as jnp
import numpy as np
from jax import lax
from jax.experimental import pallas as pl
from jax.experimental.pallas import tpu as pltpu

DEPTH = 2
CONV_WIDTH = 31
N_HEADS = 8
HEAD_DIM = 64
V_DIM = 2 * HEAD_DIM
REL_BUCKETS = 32
REL_MAX_DIST = 128
ALPHA = (2 * DEPTH) ** 0.25
LN_EPS = 1e-5
ATTN_LAYER = 1
LAMBDA_INIT = 0.8 - 0.6 * math.exp(-0.3 * ATTN_LAYER)

V7X_SUBLANES = 8
V7X_LANES = 128
V7X_VMEM_BYTES = 64 * 1024 * 1024

MASK_VALUE = -1e30
CONV_HALO = 32
CONV_ROWS = 64
SEQ_TILE = 512
ATTN_TILE = 256
FF_CHUNK = 1024

F32 = jnp.float32
BF16 = jnp.bfloat16


def _t5_thresholds():
    max_exact = REL_BUCKETS // 2
    buckets = []
    for n in range(2 * REL_MAX_DIST):
        if n < max_exact:
            buckets.append(n)
        else:
            v = math.log(n / max_exact) / math.log(REL_MAX_DIST / max_exact) * (REL_BUCKETS - max_exact)
            buckets.append(min(max_exact + int(v), REL_BUCKETS - 1))
    assert all(b1 >= b0 for b0, b1 in zip(buckets, buckets[1:]))
    assert buckets[-1] == REL_BUCKETS - 1
    return [buckets.index(b) for b in range(REL_BUCKETS)]


T5_THRESHOLDS = _t5_thresholds()
T5_LAST_BUCKET_START = T5_THRESHOLDS[REL_BUCKETS - 1]


def _vmem_limit(nbytes):
    return int(min(nbytes, V7X_VMEM_BYTES - 8 * 1024 * 1024))


def _layer_norm(z, g, b):
    mu = jnp.mean(z, axis=-1, keepdims=True)
    zc = z - mu
    var = jnp.mean(zc * zc, axis=-1, keepdims=True)
    return zc * lax.rsqrt(var + LN_EPS) * g + b


def _const_spec(shape, single_buffer=False):
    nd = len(shape)
    kwargs = {"pipeline_mode": pl.Buffered(1)} if single_buffer else {}
    return pl.BlockSpec(shape, lambda *_: (0,) * nd, **kwargs)


def _ada_mod_kernel(c_ref, w_ref, b_ref, o_ref):
    c = c_ref[...]
    sc = c * jax.nn.sigmoid(c)
    o_ref[0] = jnp.dot(sc, w_ref[0], preferred_element_type=F32,
                       precision=lax.Precision.HIGHEST) + b_ref[0]


def _ada_mod(c, w, b):
    n, d, d3 = w.shape
    bsz = c.shape[0]
    nblk = d3 // d
    out = pl.pallas_call(
        _ada_mod_kernel,
        grid=(n, nblk),
        in_specs=[
            pl.BlockSpec((bsz, d), lambda i, j: (0, 0)),
            pl.BlockSpec((1, d, d), lambda i, j: (i, 0, j)),
            pl.BlockSpec((1, 1, d), lambda i, j: (i, 0, j)),
        ],
        out_specs=pl.BlockSpec((1, bsz, d), lambda i, j: (i, 0, j)),
        out_shape=jax.ShapeDtypeStruct((n, bsz, d3), F32),
        compiler_params=pltpu.CompilerParams(
            dimension_semantics=("arbitrary", "arbitrary"),
            vmem_limit_bytes=_vmem_limit(32 * 1024 * 1024)),
        name="ada_mod",
    )(c, w, b.reshape(n, 1, d3))
    return out.reshape(n, bsz, 3, d)


def _conv_mixer_kernel(x_ref, mod_ref, w1_ref, b1_ref, wdw_ref, bdw_ref, gcn_ref, bcn_ref,
                       w2_ref, b2_ref, pg_ref, pb_ref, o_ref, ext_ref, v_ref):
    ts, d = x_ref.shape[1], x_ref.shape[2]
    s_idx = pl.program_id(1)

    @pl.when(s_idx == 0)
    def _():
        ext_ref[0:CONV_HALO, :] = jnp.zeros((CONV_HALO, d), F32)

    mod = mod_ref[0]
    shift, scale, gate = mod[0:1], mod[1:2], mod[2:3]
    x = x_ref[0]
    h = (x * (1.0 + scale) + shift).astype(BF16)
    a = jnp.dot(h, w1_ref[...], preferred_element_type=F32) + b1_ref[...]
    ext_ref[CONV_HALO:CONV_HALO + ts, :] = a[:, :d] * jax.nn.sigmoid(a[:, d:])

    off0 = CONV_HALO - (CONV_WIDTH - 1)
    win = CONV_ROWS + CONV_HALO
    lane_chunk = 4 * V7X_LANES

    def row_block(rb, carry):
        t0 = pl.multiple_of(rb * CONV_ROWS, CONV_ROWS)
        parts = []
        for c0 in range(0, d, lane_chunk):
            w_win = ext_ref[pl.ds(t0, win), c0:c0 + lane_chunk]
            acc = jnp.broadcast_to(bdw_ref[:, c0:c0 + lane_chunk], (CONV_ROWS, lane_chunk))
            for r in range(V7X_SUBLANES):
                w_r = w_win if r == 0 else pltpu.roll(w_win, win - r, axis=0)
                for grp in range(CONV_HALO // V7X_SUBLANES + 1):
                    j = grp * V7X_SUBLANES + r - off0
                    if 0 <= j < CONV_WIDTH:
                        lo = grp * V7X_SUBLANES
                        acc = acc + wdw_ref[j:j + 1, c0:c0 + lane_chunk] * w_r[lo:lo + CONV_ROWS]
            parts.append(acc)
        y = jnp.concatenate(parts, axis=-1)
        y = _layer_norm(y, gcn_ref[...], bcn_ref[...])
        v_ref[pl.ds(t0, CONV_ROWS), :] = (y * jax.nn.sigmoid(y)).astype(BF16)
        return carry

    lax.fori_loop(0, ts // CONV_ROWS, row_block, 0)
    ext_ref[0:CONV_HALO, :] = ext_ref[ts:ts + CONV_HALO, :]

    y = jnp.dot(v_ref[...], w2_ref[...], preferred_element_type=F32) + b2_ref[...]
    o_ref[0] = _layer_norm(ALPHA * x_ref[0] + gate * y, pg_ref[...], pb_ref[...])


def _conv_mixer(x, mod, w1, b1, wdw, bdw, gcn, bcn, w2, b2, pg, pb):
    bsz, s, d = x.shape
    ts = SEQ_TILE
    row = lambda v: v.reshape(1, -1)
    return pl.pallas_call(
        _conv_mixer_kernel,
        grid=(bsz, s // ts),
        in_specs=[
            pl.BlockSpec((1, ts, d), lambda b, i: (b, i, 0)),
            pl.BlockSpec((1, 3, d), lambda b, i: (b, 0, 0)),
            _const_spec((d, 2 * d), True),
            _const_spec((1, 2 * d)),
            _const_spec((CONV_WIDTH, d)),
            _const_spec((1, d)),
            _const_spec((1, d)),
            _const_spec((1, d)),
            _const_spec((d, d), True),
            _const_spec((1, d)),
            _const_spec((1, d)),
            _const_spec((1, d)),
        ],
        out_specs=pl.BlockSpec((1, ts, d), lambda b, i: (b, i, 0)),
        out_shape=jax.ShapeDtypeStruct((bsz, s, d), F32),
        scratch_shapes=[
            pltpu.VMEM((ts + CONV_HALO, d), F32),
            pltpu.VMEM((ts, d), BF16),
        ],
        compiler_params=pltpu.CompilerParams(
            dimension_semantics=("arbitrary", "arbitrary"),
            vmem_limit_bytes=_vmem_limit(48 * 1024 * 1024)),
        name="conv_mixer",
    )(x, mod, w1.astype(BF16), row(b1), wdw, row(bdw), row(gcn), row(bcn),
      w2.astype(BF16), row(b2), row(pg), row(pb))


def _mlp_kernel(x_ref, mod_ref, w1_ref, w2_ref, pg_ref, pb_ref, o_ref):
    d_ff = w1_ref.shape[1]
    mod = mod_ref[0]
    shift, scale, gate = mod[0:1], mod[1:2], mod[2:3]
    x = x_ref[0]
    h = (x * (1.0 + scale) + shift).astype(BF16)
    y = None
    for c0 in range(0, d_ff, FF_CHUNK):
        a = jnp.dot(h, w1_ref[:, c0:c0 + FF_CHUNK], preferred_element_type=F32)
        a = jnp.maximum(a, 0.0)
        part = jnp.dot((a * a).astype(BF16), w2_ref[c0:c0 + FF_CHUNK, :], preferred_element_type=F32)
        y = part if y is None else y + part
    o_ref[0] = _layer_norm(ALPHA * x + gate * y, pg_ref[...], pb_ref[...])


def _mlp(x, mod, w1, w2, pg, pb):
    bsz, s, d = x.shape
    d_ff = w1.shape[1]
    ts = SEQ_TILE
    row = lambda v: v.reshape(1, -1)
    return pl.pallas_call(
        _mlp_kernel,
        grid=(bsz, s // ts),
        in_specs=[
            pl.BlockSpec((1, ts, d), lambda b, i: (b, i, 0)),
            pl.BlockSpec((1, 3, d), lambda b, i: (b, 0, 0)),
            _const_spec((d, d_ff), True),
            _const_spec((d_ff, d), True),
            _const_spec((1, d)),
            _const_spec((1, d)),
        ],
        out_specs=pl.BlockSpec((1, ts, d), lambda b, i: (b, i, 0)),
        out_shape=jax.ShapeDtypeStruct((bsz, s, d), F32),
        compiler_params=pltpu.CompilerParams(
            dimension_semantics=("arbitrary", "arbitrary"),
            vmem_limit_bytes=_vmem_limit(52 * 1024 * 1024)),
        name="sq_relu_mlp",
    )(x, mod, w1.astype(BF16), w2.astype(BF16), row(pg), row(pb))


def _qkv_kernel(x_ref, mod_ref, wq_ref, wk_ref, wvt_ref, q_ref, k_ref, vt_ref):
    mod = mod_ref[0]
    shift, scale = mod[0:1], mod[1:2]
    h = (x_ref[0] * (1.0 + scale) + shift).astype(BF16)
    q = jnp.dot(h, wq_ref[...], preferred_element_type=F32)
    q_ref[0] = (q * (HEAD_DIM ** -0.5)).astype(BF16)
    k_ref[0] = jnp.dot(h, wk_ref[...], preferred_element_type=F32).astype(BF16)
    vt = lax.dot_general(wvt_ref[...], h, (((1,), (1,)), ((), ())), preferred_element_type=F32)
    vt_ref[0, 0] = vt.astype(BF16)


def _qkv(x, mod, w_qkv):
    bsz, s, d = x.shape
    tk = ATTN_TILE
    dq = N_HEADS * 2 * HEAD_DIM
    dv = N_HEADS * V_DIM
    wq = w_qkv[:, :dq].astype(BF16)
    wk = w_qkv[:, dq:2 * dq].astype(BF16)
    wvt = w_qkv[:, 2 * dq:].T.astype(BF16)
    return pl.pallas_call(
        _qkv_kernel,
        grid=(bsz, s // tk),
        in_specs=[
            pl.BlockSpec((1, tk, d), lambda b, i: (b, i, 0)),
            pl.BlockSpec((1, 3, d), lambda b, i: (b, 0, 0)),
            _const_spec((d, dq), True),
            _const_spec((d, dq), True),
            _const_spec((dv, d), True),
        ],
        out_specs=[
            pl.BlockSpec((1, tk, dq), lambda b, i: (b, i, 0)),
            pl.BlockSpec((1, tk, dq), lambda b, i: (b, i, 0)),
            pl.BlockSpec((1, 1, dv, tk), lambda b, i: (b, i, 0, 0)),
        ],
        out_shape=[
            jax.ShapeDtypeStruct((bsz, s, dq), BF16),
            jax.ShapeDtypeStruct((bsz, s, dq), BF16),
            jax.ShapeDtypeStruct((bsz, s // tk, dv, tk), BF16),
        ],
        compiler_params=pltpu.CompilerParams(
            dimension_semantics=("arbitrary", "arbitrary"),
            vmem_limit_bytes=_vmem_limit(40 * 1024 * 1024)),
        name="qkv_proj",
    )(x, mod, wq, wk, wvt)


def _bias_tiles_kernel(rb_ref, o_ref):
    h = pl.program_id(0)
    t = o_ref.shape[2]
    far = rb_ref[h, REL_BUCKETS - 1]
    kk = lax.broadcasted_iota(jnp.int32, (t, t), 0)
    qq = lax.broadcasted_iota(jnp.int32, (t, t), 1)
    for dlt in range(o_ref.shape[1]):
        rel = dlt * t + qq - kk
        val = jnp.full((t, t), rb_ref[h, 0] - far, F32)
        for bkt in range(1, REL_BUCKETS):
            val = jnp.where(rel >= T5_THRESHOLDS[bkt], rb_ref[h, bkt] - far, val)
        o_ref[0, dlt] = jnp.where(rel >= 0, val, MASK_VALUE)


def _bias_tiles(rel_bias):
    t = ATTN_TILE
    n_near = -(-(T5_LAST_BUCKET_START + t - 1) // t)
    return pl.pallas_call(
        _bias_tiles_kernel,
        grid=(N_HEADS,),
        in_specs=[pl.BlockSpec(memory_space=pltpu.SMEM)],
        out_specs=pl.BlockSpec((1, n_near, t, t), lambda h: (h, 0, 0, 0)),
        out_shape=jax.ShapeDtypeStruct((N_HEADS, n_near, t, t), F32),
        compiler_params=pltpu.CompilerParams(dimension_semantics=("arbitrary",)),
        name="t5_bias_tiles",
    )(rel_bias.T)


def _attn_kernel(q_ref, k_ref, vt_ref, bias_ref, lam_ref, g_ref, o_ref,
                 qm_ref, m_ref, l_ref, acc_ref):
    t = q_ref.shape[1]
    n_near = bias_ref.shape[1]
    qi = pl.program_id(2)

    q = q_ref[0]
    lane = lax.broadcasted_iota(jnp.int32, q.shape, 1)
    zero = jnp.zeros_like(q)
    qm_ref[0] = jnp.where(lane < HEAD_DIM, q, zero)
    qm_ref[1] = jnp.where(lane >= HEAD_DIM, q, zero)
    m_ref[...] = jnp.full(m_ref.shape, MASK_VALUE, F32)
    l_ref[...] = jnp.zeros(l_ref.shape, F32)
    acc_ref[...] = jnp.zeros(acc_ref.shape, F32)

    def step(j, bias):
        k = k_ref[0, pl.ds(pl.multiple_of(j * t, t), t), :]
        vt = vt_ref[0, j]
        for mp in range(2):
            st = lax.dot_general(k, qm_ref[mp], (((1,), (1,)), ((), ())), preferred_element_type=F32)
            if bias is not None:
                st = st + bias
            m_old = m_ref[mp]
            m_new = jnp.maximum(m_old, jnp.max(st, axis=0, keepdims=True))
            alpha = jnp.exp(m_old - m_new)
            p = jnp.exp(st - m_new)
            l_ref[mp] = alpha * l_ref[mp] + jnp.sum(p, axis=0, keepdims=True)
            acc_ref[mp] = alpha * acc_ref[mp] + jnp.dot(vt, p.astype(BF16), preferred_element_type=F32)
            m_ref[mp] = m_new

    def far_step(j, carry):
        step(j, None)
        return carry

    lax.fori_loop(0, jnp.maximum(qi - (n_near - 1), 0), far_step, 0)
    for dlt in range(n_near - 1, 0, -1):
        @pl.when(qi >= dlt)
        def _(dlt=dlt):
            step(qi - dlt, bias_ref[0, dlt])
    step(qi, bias_ref[0, 0])

    lam_v = lam_ref[...]
    lam = (jnp.exp(jnp.sum(lam_v[0:1] * lam_v[1:2], axis=-1, keepdims=True))
           - jnp.exp(jnp.sum(lam_v[2:3] * lam_v[3:4], axis=-1, keepdims=True)) + LAMBDA_INIT)
    ot = acc_ref[0] * (1.0 / l_ref[0]) - lam * (acc_ref[1] * (1.0 / l_ref[1]))
    o = ot.T
    o = o * lax.rsqrt(jnp.mean(o * o, axis=-1, keepdims=True) + LN_EPS) * g_ref[...] * (1.0 - LAMBDA_INIT)
    o_ref[0] = o.astype(BF16)


def _attention(q, k, vt, bias, lam_vecs, g_sub):
    bsz, s, dq = q.shape
    t = ATTN_TILE
    nq = s // t
    n_near = bias.shape[1]
    hw = 2 * HEAD_DIM
    return pl.pallas_call(
        _attn_kernel,
        grid=(bsz, N_HEADS, nq),
        in_specs=[
            pl.BlockSpec((1, t, hw), lambda b, h, i: (b, i, h)),
            pl.BlockSpec((1, s, hw), lambda b, h, i: (b, 0, h)),
            pl.BlockSpec((1, nq, V_DIM, t), lambda b, h, i: (b, 0, h, 0)),
            pl.BlockSpec((1, n_near, t, t), lambda b, h, i: (h, 0, 0, 0)),
            _const_spec((4, HEAD_DIM)),
            _const_spec((1, V_DIM)),
        ],
        out_specs=pl.BlockSpec((1, t, V_DIM), lambda b, h, i: (b, i, h)),
        out_shape=jax.ShapeDtypeStruct((bsz, s, N_HEADS * V_DIM), BF16),
        scratch_shapes=[
            pltpu.VMEM((2, t, hw), BF16),
            pltpu.VMEM((2, 1, t), F32),
            pltpu.VMEM((2, 1, t), F32),
            pltpu.VMEM((2, V_DIM, t), F32),
        ],
        compiler_params=pltpu.CompilerParams(
            dimension_semantics=("arbitrary", "arbitrary", "arbitrary"),
            vmem_limit_bytes=_vmem_limit(40 * 1024 * 1024)),
        name="diff_attention",
    )(q, k, vt, bias, lam_vecs, g_sub.reshape(1, V_DIM))


def _out_proj_kernel(a_ref, x_ref, mod_ref, wo_ref, pg_ref, pb_ref, o_ref):
    gate = mod_ref[0][2:3]
    y = jnp.dot(a_ref[0], wo_ref[...], preferred_element_type=F32)
    o_ref[0] = _layer_norm(ALPHA * x_ref[0] + gate * y, pg_ref[...], pb_ref[...])


def _out_proj(a, x, mod, wo, pg, pb):
    bsz, s, d = x.shape
    da = a.shape[2]
    ts = SEQ_TILE
    row = lambda v: v.reshape(1, -1)
    return pl.pallas_call(
        _out_proj_kernel,
        grid=(bsz, s // ts),
        in_specs=[
            pl.BlockSpec((1, ts, da), lambda b, i: (b, i, 0)),
            pl.BlockSpec((1, ts, d), lambda b, i: (b, i, 0)),
            pl.BlockSpec((1, 3, d), lambda b, i: (b, 0, 0)),
            _const_spec((da, d), True),
            _const_spec((1, d)),
            _const_spec((1, d)),
        ],
        out_specs=pl.BlockSpec((1, ts, d), lambda b, i: (b, i, 0)),
        out_shape=jax.ShapeDtypeStruct((bsz, s, d), F32),
        compiler_params=pltpu.CompilerParams(
            dimension_semantics=("arbitrary", "arbitrary"),
            vmem_limit_bytes=_vmem_limit(32 * 1024 * 1024)),
        name="attn_out_proj",
    )(a, x, mod, wo.astype(BF16), row(pg), row(pb))


def kernel(x, c, conv_mod_w, conv_mod_b, conv_pw1_w, conv_pw1_b, conv_dw_w, conv_dw_b, conv_norm_g, conv_norm_b, conv_pw2_w, conv_pw2_b, attn_mod_w, attn_mod_b, attn_qkv_w, attn_lam_q1, attn_lam_k1, attn_lam_q2, attn_lam_k2, attn_subln_g, attn_out_w, rel_bias, mlp_mod_w, mlp_mod_b, mlp_w1, mlp_w2, post_mix_g, post_mix_b, post_mlp_g, post_mlp_b):
    assert x.shape[1] % SEQ_TILE == 0 and x.shape[1] % ATTN_TILE == 0
    assert SEQ_TILE % CONV_ROWS == 0 and CONV_HALO >= CONV_WIDTH - 1
    conv_mod = _ada_mod(c, conv_mod_w, conv_mod_b)
    attn_mod = _ada_mod(c, attn_mod_w, attn_mod_b)
    mlp_mod = _ada_mod(c, mlp_mod_w, mlp_mod_b)

    x = _conv_mixer(x, conv_mod[0], conv_pw1_w[0], conv_pw1_b[0], conv_dw_w[0], conv_dw_b[0],
                    conv_norm_g[0], conv_norm_b[0], conv_pw2_w[0], conv_pw2_b[0],
                    post_mix_g[0], post_mix_b[0])
    x = _mlp(x, mlp_mod[0], mlp_w1[0], mlp_w2[0], post_mlp_g[0], post_mlp_b[0])

    q, k, vt = _qkv(x, attn_mod[0], attn_qkv_w[0])
    bias = _bias_tiles(rel_bias)
    lam_vecs = jnp.stack([attn_lam_q1[0], attn_lam_k1[0], attn_lam_q2[0], attn_lam_k2[0]])
    a = _attention(q, k, vt, bias, lam_vecs, attn_subln_g[0])
    x = _out_proj(a, x, attn_mod[0], attn_out_w[0], post_mix_g[1], post_mix_b[1])
    x = _mlp(x, mlp_mod[1], mlp_w1[1], mlp_w2[1], post_mlp_g[1], post_mlp_b[1])
    return x
```

```python
import functools
import math

import jax
import jax.numpy as jnp
import numpy as np
from jax import lax
from jax.experimental import pallas as pl
from jax.experimental.pallas import tpu as pltpu

DEPTH = 2
CONV_WIDTH = 31
N_HEADS = 8
HEAD_DIM = 64
V_DIM = 2 * HEAD_DIM
REL_BUCKETS = 32
REL_MAX_DIST = 128
ALPHA = (2 * DEPTH) ** 0.25
LN_EPS = 1e-5
ATTN_LAYER = 1
LAMBDA_INIT = 0.8 - 0.6 * math.exp(-0.3 * ATTN_LAYER)

V7X_SUBLANES = 8
V7X_LANES = 128
V7X_VMEM_BYTES = 64 * 1024 * 1024

MASK_VALUE = -1e30
CONV_HALO = 32
CONV_ROWS = 64
SEQ_TILE = 512
ATTN_TILE = 512
FF_CHUNK = 1024

F32 = jnp.float32
BF16 = jnp.bfloat16


def _t5_thresholds():
    max_exact = REL_BUCKETS // 2
    buckets = []
    for n in range(2 * REL_MAX_DIST):
        if n < max_exact:
            buckets.append(n)
        else:
            v = math.log(n / max_exact) / math.log(REL_MAX_DIST / max_exact) * (REL_BUCKETS - max_exact)
            buckets.append(min(max_exact + int(v), REL_BUCKETS - 1))
    assert all(b1 >= b0 for b0, b1 in zip(buckets, buckets[1:]))
    assert buckets[-1] == REL_BUCKETS - 1
    return [buckets.index(b) for b in range(REL_BUCKETS)]


T5_THRESHOLDS = _t5_thresholds()
T5_LAST_BUCKET_START = T5_THRESHOLDS[REL_BUCKETS - 1]


def _vmem_limit(nbytes):
    return int(min(nbytes, V7X_VMEM_BYTES - 8 * 1024 * 1024))


def _layer_norm(z, g, b):
    mu = jnp.mean(z, axis=-1, keepdims=True)
    zc = z - mu
    var = jnp.mean(zc * zc, axis=-1, keepdims=True)
    return zc * lax.rsqrt(var + LN_EPS) * g + b


def _const_spec(shape, single_buffer=False):
    nd = len(shape)
    kwargs = {"pipeline_mode": pl.Buffered(1)} if single_buffer else {}
    return pl.BlockSpec(shape, lambda *_: (0,) * nd, **kwargs)


def _ada_mod_kernel(c_ref, w_ref, b_ref, o_ref):
    c = c_ref[...]
    sc = c * jax.nn.sigmoid(c)
    o_ref[0] = jnp.dot(sc, w_ref[0], preferred_element_type=F32,
                       precision=lax.Precision.HIGHEST) + b_ref[0]


def _ada_mod(c, w, b):
    n, d, d3 = w.shape
    bsz = c.shape[0]
    nblk = d3 // d
    out = pl.pallas_call(
        _ada_mod_kernel,
        grid=(n, nblk),
        in_specs=[
            pl.BlockSpec((bsz, d), lambda i, j: (0, 0)),
            pl.BlockSpec((1, d, d), lambda i, j: (i, 0, j)),
            pl.BlockSpec((1, 1, d), lambda i, j: (i, 0, j)),
        ],
        out_specs=pl.BlockSpec((1, bsz, d), lambda i, j: (i, 0, j)),
        out_shape=jax.ShapeDtypeStruct((n, bsz, d3), F32),
        compiler_params=pltpu.CompilerParams(
            dimension_semantics=("arbitrary", "arbitrary"),
            vmem_limit_bytes=_vmem_limit(32 * 1024 * 1024)),
        name="ada_mod",
    )(c, w, b.reshape(n, 1, d3))
    return out.reshape(n, bsz, 3, d)


def _conv_mixer_kernel(x_ref, mod_ref, w1_ref, b1_ref, wdw_ref, bdw_ref, gcn_ref, bcn_ref,
                       w2_ref, b2_ref, pg_ref, pb_ref, o_ref, ext_ref, v_ref):
    ts, d = x_ref.shape[1], x_ref.shape[2]
    s_idx = pl.program_id(1)

    @pl.when(s_idx == 0)
    def _():
        ext_ref[0:CONV_HALO, :] = jnp.zeros((CONV_HALO, d), F32)

    mod = mod_ref[0]
    shift, scale, gate = mod[0:1], mod[1:2], mod[2:3]
    x = x_ref[0]
    h = (x * (1.0 + scale) + shift).astype(BF16)
    a = jnp.dot(h, w1_ref[...], preferred_element_type=F32) + b1_ref[...]
    ext_ref[CONV_HALO:CONV_HALO + ts, :] = a[:, :d] * jax.nn.sigmoid(a[:, d:])

    off0 = CONV_HALO - (CONV_WIDTH - 1)
    win = CONV_ROWS + CONV_HALO
    lane_chunk = 4 * V7X_LANES

    def row_block(rb, carry):
        t0 = pl.multiple_of(rb * CONV_ROWS, CONV_ROWS)
        parts = []
        for c0 in range(0, d, lane_chunk):
            w_win = ext_ref[pl.ds(t0, win), c0:c0 + lane_chunk]
            acc = jnp.broadcast_to(bdw_ref[:, c0:c0 + lane_chunk], (CONV_ROWS, lane_chunk))
            for r in range(V7X_SUBLANES):
                w_r = w_win if r == 0 else pltpu.roll(w_win, win - r, axis=0)
                for grp in range(CONV_HALO // V7X_SUBLANES + 1):
                    j = grp * V7X_SUBLANES + r - off0
                    if 0 <= j < CONV_WIDTH:
                        lo = grp * V7X_SUBLANES
                        acc = acc + wdw_ref[j:j + 1, c0:c0 + lane_chunk] * w_r[lo:lo + CONV_ROWS]
            parts.append(acc)
        y = jnp.concatenate(parts, axis=-1)
        y = _layer_norm(y, gcn_ref[...], bcn_ref[...])
        v_ref[pl.ds(t0, CONV_ROWS), :] = (y * jax.nn.sigmoid(y)).astype(BF16)
        return carry

    lax.fori_loop(0, ts // CONV_ROWS, row_block, 0)
    ext_ref[0:CONV_HALO, :] = ext_ref[ts:ts + CONV_HALO, :]

    y = jnp.dot(v_ref[...], w2_ref[...], preferred_element_type=F32) + b2_ref[...]
    o_ref[0] = _layer_norm(ALPHA * x_ref[0] + gate * y, pg_ref[...], pb_ref[...])


def _conv_mixer(x, mod, w1, b1, wdw, bdw, gcn, bcn, w2, b2, pg, pb):
    bsz, s, d = x.shape
    ts = SEQ_TILE
    row = lambda v: v.reshape(1, -1)
    return pl.pallas_call(
        _conv_mixer_kernel,
        grid=(bsz, s // ts),
        in_specs=[
            pl.BlockSpec((1, ts, d), lambda b, i: (b, i, 0)),
            pl.BlockSpec((1, 3, d), lambda b, i: (b, 0, 0)),
            _const_spec((d, 2 * d), True),
            _const_spec((1, 2 * d)),
            _const_spec((CONV_WIDTH, d)),
            _const_spec((1, d)),
            _const_spec((1, d)),
            _const_spec((1, d)),
            _const_spec((d, d), True),
            _const_spec((1, d)),
            _const_spec((1, d)),
            _const_spec((1, d)),
        ],
        out_specs=pl.BlockSpec((1, ts, d), lambda b, i: (b, i, 0)),
        out_shape=jax.ShapeDtypeStruct((bsz, s, d), F32),
        scratch_shapes=[
            pltpu.VMEM((ts + CONV_HALO, d), F32),
            pltpu.VMEM((ts, d), BF16),
        ],
        compiler_params=pltpu.CompilerParams(
            dimension_semantics=("arbitrary", "arbitrary"),
            vmem_limit_bytes=_vmem_limit(48 * 1024 * 1024)),
        name="conv_mixer",
    )(x, mod, w1.astype(BF16), row(b1), wdw, row(bdw), row(gcn), row(bcn),
      w2.astype(BF16), row(b2), row(pg), row(pb))


def _mlp_kernel(x_ref, mod_ref, w1_ref, w2_ref, pg_ref, pb_ref, o_ref):
    d_ff = w1_ref.shape[1]
    mod = mod_ref[0]
    shift, scale, gate = mod[0:1], mod[1:2], mod[2:3]
    x = x_ref[0]
    h = (x * (1.0 + scale) + shift).astype(BF16)
    y = None
    for c0 in range(0, d_ff, FF_CHUNK):
        a = jnp.dot(h, w1_ref[:, c0:c0 + FF_CHUNK], preferred_element_type=F32)
        a = jnp.maximum(a, 0.0)
        part = jnp.dot((a * a).astype(BF16), w2_ref[c0:c0 + FF_CHUNK, :], preferred_element_type=F32)
        y = part if y is None else y + part
    o_ref[0] = _layer_norm(ALPHA * x + gate * y, pg_ref[...], pb_ref[...])


def _mlp(x, mod, w1, w2, pg, pb):
    bsz, s, d = x.shape
    d_ff = w1.shape[1]
    ts = SEQ_TILE
    row = lambda v: v.reshape(1, -1)
    return pl.pallas_call(
        _mlp_kernel,
        grid=(bsz, s // ts),
        in_specs=[
            pl.BlockSpec((1, ts, d), lambda b, i: (b, i, 0)),
            pl.BlockSpec((1, 3, d), lambda b, i: (b, 0, 0)),
            _const_spec((d, d_ff), True),
            _const_spec((d_ff, d), True),
            _const_spec((1, d)),
            _const_spec((1, d)),
        ],
        out_specs=pl.BlockSpec((1, ts, d), lambda b, i: (b, i, 0)),
        out_shape=jax.ShapeDtypeStruct((bsz, s, d), F32),
        compiler_params=pltpu.CompilerParams(
            dimension_semantics=("arbitrary", "arbitrary"),
            vmem_limit_bytes=_vmem_limit(52 * 1024 * 1024)),
        name="sq_relu_mlp",
    )(x, mod, w1.astype(BF16), w2.astype(BF16), row(pg), row(pb))


def _qkv_kernel(x_ref, mod_ref, wq_ref, wk_ref, wvt_ref, q_ref, k_ref, vt_ref):
    mod = mod_ref[0]
    shift, scale = mod[0:1], mod[1:2]
    h = (x_ref[0] * (1.0 + scale) + shift).astype(BF16)
    q = jnp.dot(h, wq_ref[...], preferred_element_type=F32)
    q_ref[0] = (q * (HEAD_DIM ** -0.5)).astype(BF16)
    k_ref[0] = jnp.dot(h, wk_ref[...], preferred_element_type=F32).astype(BF16)
    vt = lax.dot_general(wvt_ref[...], h, (((1,), (1,)), ((), ())), preferred_element_type=F32)
    vt_ref[0, 0] = vt.astype(BF16)


def _qkv(x, mod, w_qkv):
    bsz, s, d = x.shape
    tk = ATTN_TILE
    dq = N_HEADS * 2 * HEAD_DIM
    dv = N_HEADS * V_DIM
    wq = w_qkv[:, :dq].astype(BF16)
    wk = w_qkv[:, dq:2 * dq].astype(BF16)
    wvt = w_qkv[:, 2 * dq:].T.astype(BF16)
    return pl.pallas_call(
        _qkv_kernel,
        grid=(bsz, s // tk),
        in_specs=[
            pl.BlockSpec((1, tk, d), lambda b, i: (b, i, 0)),
            pl.BlockSpec((1, 3, d), lambda b, i: (b, 0, 0)),
            _const_spec((d, dq), True),
            _const_spec((d, dq), True),
            _const_spec((dv, d), True),
        ],
        out_specs=[
            pl.BlockSpec((1, tk, dq), lambda b, i: (b, i, 0)),
            pl.BlockSpec((1, tk, dq), lambda b, i: (b, i, 0)),
            pl.BlockSpec((1, 1, dv, tk), lambda b, i: (b, i, 0, 0)),
        ],
        out_shape=[
            jax.ShapeDtypeStruct((bsz, s, dq), BF16),
            jax.ShapeDtypeStruct((bsz, s, dq), BF16),
            jax.ShapeDtypeStruct((bsz, s // tk, dv, tk), BF16),
        ],
        compiler_params=pltpu.CompilerParams(
            dimension_semantics=("arbitrary", "arbitrary"),
            vmem_limit_bytes=_vmem_limit(40 * 1024 * 1024)),
        name="qkv_proj",
    )(x, mod, wq, wk, wvt)


def _bias_tiles_kernel(rb_ref, o_ref):
    h = pl.program_id(0)
    t = o_ref.shape[2]
    far = rb_ref[h, REL_BUCKETS - 1]
    kk = lax.broadcasted_iota(jnp.int32, (t, t), 0)
    qq = lax.broadcasted_iota(jnp.int32, (t, t), 1)
    for dlt in range(o_ref.shape[1]):
        rel = dlt * t + qq - kk
        val = jnp.full((t, t), rb_ref[h, 0] - far, F32)
        for bkt in range(1, REL_BUCKETS):
            val = jnp.where(rel >= T5_THRESHOLDS[bkt], rb_ref[h, bkt] - far, val)
        o_ref[0, dlt] = jnp.where(rel >= 0, val, MASK_VALUE)


def _bias_tiles(rel_bias):
    t = ATTN_TILE
    n_near = -(-(T5_LAST_BUCKET_START + t - 1) // t)
    return pl.pallas_call(
        _bias_tiles_kernel,
        grid=(N_HEADS,),
        in_specs=[pl.BlockSpec(memory_space=pltpu.SMEM)],
        out_specs=pl.BlockSpec((1, n_near, t, t), lambda h: (h, 0, 0, 0)),
        out_shape=jax.ShapeDtypeStruct((N_HEADS, n_near, t, t), F32),
        compiler_params=pltpu.CompilerParams(dimension_semantics=("arbitrary",)),
        name="t5_bias_tiles",
    )(rel_bias.T)


def _attn_kernel(q_ref, k_ref, vt_ref, bias_ref, lam_ref, g_ref, o_ref,
                 qc_ref, m_ref, l_ref, acc_ref):
    t = q_ref.shape[1]
    n_near = bias_ref.shape[1]
    qi = pl.program_id(2)

    q = q_ref[0]
    lane = lax.broadcasted_iota(jnp.int32, q.shape, 1)
    zero = jnp.zeros_like(q)
    qc_ref[0:t, :] = jnp.where(lane < HEAD_DIM, q, zero)
    qc_ref[t:2 * t, :] = jnp.where(lane >= HEAD_DIM, q, zero)
    m_ref[...] = jnp.full(m_ref.shape, MASK_VALUE, F32)
    l_ref[...] = jnp.zeros(l_ref.shape, F32)
    acc_ref[...] = jnp.zeros(acc_ref.shape, F32)

    def step(j, bias):
        k = k_ref[0, pl.ds(pl.multiple_of(j * t, t), t), :]
        st = lax.dot_general(k, qc_ref[...], (((1,), (1,)), ((), ())), preferred_element_type=F32)
        if bias is not None:
            st = jnp.concatenate([st[:, :t] + bias, st[:, t:] + bias], axis=1)
        m_old = m_ref[...]
        m_new = jnp.maximum(m_old, jnp.max(st, axis=0, keepdims=True))
        alpha = jnp.exp(m_old - m_new)
        p = jnp.exp(st - m_new)
        l_ref[...] = alpha * l_ref[...] + jnp.sum(p, axis=0, keepdims=True)
        acc_ref[...] = alpha * acc_ref[...] + jnp.dot(vt_ref[0, j], p.astype(BF16),
                                                      preferred_element_type=F32)
        m_ref[...] = m_new

    def far_step(j, carry):
        step(j, None)
        return carry

    lax.fori_loop(0, jnp.maximum(qi - (n_near - 1), 0), far_step, 0)
    for dlt in range(n_near - 1, 0, -1):
        @pl.when(qi >= dlt)
        def _(dlt=dlt):
            step(qi - dlt, bias_ref[0, dlt])
    step(qi, bias_ref[0, 0])

    lam_v = lam_ref[...]
    lam = (jnp.exp(jnp.sum(lam_v[0:1] * lam_v[1:2], axis=-1, keepdims=True))
           - jnp.exp(jnp.sum(lam_v[2:3] * lam_v[3:4], axis=-1, keepdims=True)) + LAMBDA_INIT)
    on = acc_ref[...] * (1.0 / l_ref[...])
    o = (on[:, :t] - lam * on[:, t:]).T
    o = o * lax.rsqrt(jnp.mean(o * o, axis=-1, keepdims=True) + LN_EPS) * g_ref[...] * (1.0 - LAMBDA_INIT)
    o_ref[0] = o.astype(BF16)


def _attention(q, k, vt, bias, lam_vecs, g_sub):
    bsz, s, dq = q.shape
    t = ATTN_TILE
    nq = s // t
    n_near = bias.shape[1]
    hw = 2 * HEAD_DIM
    return pl.pallas_call(
        _attn_kernel,
        grid=(bsz, N_HEADS, nq),
        in_specs=[
            pl.BlockSpec((1, t, hw), lambda b, h, i: (b, i, h)),
            pl.BlockSpec((1, s, hw), lambda b, h, i: (b, 0, h)),
            pl.BlockSpec((1, nq, V_DIM, t), lambda b, h, i: (b, 0, h, 0)),
            pl.BlockSpec((1, n_near, t, t), lambda b, h, i: (h, 0, 0, 0)),
            _const_spec((4, HEAD_DIM)),
            _const_spec((1, V_DIM)),
        ],
        out_specs=pl.BlockSpec((1, t, V_DIM), lambda b, h, i: (b, i, h)),
        out_shape=jax.ShapeDtypeStruct((bsz, s, N_HEADS * V_DIM), BF16),
        scratch_shapes=[
            pltpu.VMEM((2 * t, hw), BF16),
            pltpu.VMEM((1, 2 * t), F32),
            pltpu.VMEM((1, 2 * t), F32),
            pltpu.VMEM((V_DIM, 2 * t), F32),
        ],
        compiler_params=pltpu.CompilerParams(
            dimension_semantics=("arbitrary", "arbitrary", "arbitrary"),
            vmem_limit_bytes=_vmem_limit(48 * 1024 * 1024)),
        name="diff_attention",
    )(q, k, vt, bias, lam_vecs, g_sub.reshape(1, V_DIM))


def _out_proj_kernel(a_ref, x_ref, mod_ref, wo_ref, pg_ref, pb_ref, o_ref):
    gate = mod_ref[0][2:3]
    y = jnp.dot(a_ref[0], wo_ref[...], preferred_element_type=F32)
    o_ref[0] = _layer_norm(ALPHA * x_ref[0] + gate * y, pg_ref[...], pb_ref[...])


def _out_proj(a, x, mod, wo, pg, pb):
    bsz, s, d = x.shape
    da = a.shape[2]
    ts = SEQ_TILE
    row = lambda v: v.reshape(1, -1)
    return pl.pallas_call(
        _out_proj_kernel,
        grid=(bsz, s // ts),
        in_specs=[
            pl.BlockSpec((1, ts, da), lambda b, i: (b, i, 0)),
            pl.BlockSpec((1, ts, d), lambda b, i: (b, i, 0)),
            pl.BlockSpec((1, 3, d), lambda b, i: (b, 0, 0)),
            _const_spec((da, d), True),
            _const_spec((1, d)),
            _const_spec((1, d)),
        ],
        out_specs=pl.BlockSpec((1, ts, d), lambda b, i: (b, i, 0)),
        out_shape=jax.ShapeDtypeStruct((bsz, s, d), F32),
        compiler_params=pltpu.CompilerParams(
            dimension_semantics=("arbitrary", "arbitrary"),
            vmem_limit_bytes=_vmem_limit(32 * 1024 * 1024)),
        name="attn_out_proj",
    )(a, x, mod, wo.astype(BF16), row(pg), row(pb))


def kernel(x, c, conv_mod_w, conv_mod_b, conv_pw1_w, conv_pw1_b, conv_dw_w, conv_dw_b, conv_norm_g, conv_norm_b, conv_pw2_w, conv_pw2_b, attn_mod_w, attn_mod_b, attn_qkv_w, attn_lam_q1, attn_lam_k1, attn_lam_q2, attn_lam_k2, attn_subln_g, attn_out_w, rel_bias, mlp_mod_w, mlp_mod_b, mlp_w1, mlp_w2, post_mix_g, post_mix_b, post_mlp_g, post_mlp_b):
    assert x.shape[1] % SEQ_TILE == 0 and x.shape[1] % ATTN_TILE == 0
    assert SEQ_TILE % CONV_ROWS == 0 and CONV_HALO >= CONV_WIDTH - 1
    conv_mod = _ada_mod(c, conv_mod_w, conv_mod_b)
    attn_mod = _ada_mod(c, attn_mod_w, attn_mod_b)
    mlp_mod = _ada_mod(c, mlp_mod_w, mlp_mod_b)

    x = _conv_mixer(x, conv_mod[0], conv_pw1_w[0], conv_pw1_b[0], conv_dw_w[0], conv_dw_b[0],
                    conv_norm_g[0], conv_norm_b[0], conv_pw2_w[0], conv_pw2_b[0],
                    post_mix_g[0], post_mix_b[0])
    x = _mlp(x, mlp_mod[0], mlp_w1[0], mlp_w2[0], post_mlp_g[0], post_mlp_b[0])

    q, k, vt = _qkv(x, attn_mod[0], attn_qkv_w[0])
    bias = _bias_tiles(rel_bias)
    lam_vecs = jnp.stack([attn_lam_q1[0], attn_lam_k1[0], attn_lam_q2[0], attn_lam_k2[0]])
    a = _attention(q, k, vt, bias, lam_vecs, attn_subln_g[0])
    x = _out_proj(a, x, attn_mod[0], attn_out_w[0], post_mix_g[1], post_mix_b[1])
    x = _mlp(x, mlp_mod[1], mlp_w1[1], mlp_w2[1], post_mlp_g[1], post_mlp_b[1])
    return x
```

```python
import functools
import math

import jax
import jax.numpy as jnp
import numpy as np
from jax import lax
from jax.experimental import pallas as pl
from jax.experimental.pallas import tpu as pltpu

DEPTH = 2
CONV_WIDTH = 31
N_HEADS = 8
HEAD_DIM = 64
V_DIM = 2 * HEAD_DIM
REL_BUCKETS = 32
REL_MAX_DIST = 128
ALPHA = (2 * DEPTH) ** 0.25
LN_EPS = 1e-5
ATTN_LAYER = 1
LAMBDA_INIT = 0.8 - 0.6 * math.exp(-0.3 * ATTN_LAYER)
LOG2_E = math.log2(math.e)

V7X_SUBLANES = 8
V7X_LANES = 128
V7X_VMEM_BYTES = 64 * 1024 * 1024

MASK_VALUE = -1e30
CONV_HALO = 32
CONV_ROWS = 64
SEQ_TILE = 512
ATTN_TILE = 512
FF_CHUNK = 1024

F32 = jnp.float32
BF16 = jnp.bfloat16


def _t5_thresholds():
    max_exact = REL_BUCKETS // 2
    buckets = []
    for n in range(2 * REL_MAX_DIST):
        if n < max_exact:
            buckets.append(n)
        else:
            v = math.log(n / max_exact) / math.log(REL_MAX_DIST / max_exact) * (REL_BUCKETS - max_exact)
            buckets.append(min(max_exact + int(v), REL_BUCKETS - 1))
    assert all(b1 >= b0 for b0, b1 in zip(buckets, buckets[1:]))
    assert buckets[-1] == REL_BUCKETS - 1
    return [buckets.index(b) for b in range(REL_BUCKETS)]


T5_THRESHOLDS = _t5_thresholds()
T5_LAST_BUCKET_START = T5_THRESHOLDS[REL_BUCKETS - 1]


def _vmem_limit(nbytes):
    return int(min(nbytes, V7X_VMEM_BYTES - 8 * 1024 * 1024))


def _layer_norm(z, g, b):
    mu = jnp.mean(z, axis=-1, keepdims=True)
    zc = z - mu
    var = jnp.mean(zc * zc, axis=-1, keepdims=True)
    return zc * lax.rsqrt(var + LN_EPS) * g + b


def _const_spec(shape, single_buffer=False):
    nd = len(shape)
    kwargs = {"pipeline_mode": pl.Buffered(1)} if single_buffer else {}
    return pl.BlockSpec(shape, lambda *_: (0,) * nd, **kwargs)


def _ada_mod_kernel(c_ref, w_ref, b_ref, o_ref):
    c = c_ref[...]
    sc = c * jax.nn.sigmoid(c)
    o_ref[0] = jnp.dot(sc, w_ref[0], preferred_element_type=F32,
                       precision=lax.Precision.HIGHEST) + b_ref[0]


def _ada_mod(c, w, b):
    n, d, d3 = w.shape
    bsz = c.shape[0]
    nblk = d3 // d
    out = pl.pallas_call(
        _ada_mod_kernel,
        grid=(n, nblk),
        in_specs=[
            pl.BlockSpec((bsz, d), lambda i, j: (0, 0)),
            pl.BlockSpec((1, d, d), lambda i, j: (i, 0, j)),
            pl.BlockSpec((1, 1, d), lambda i, j: (i, 0, j)),
        ],
        out_specs=pl.BlockSpec((1, bsz, d), lambda i, j: (i, 0, j)),
        out_shape=jax.ShapeDtypeStruct((n, bsz, d3), F32),
        compiler_params=pltpu.CompilerParams(
            dimension_semantics=("arbitrary", "arbitrary"),
            vmem_limit_bytes=_vmem_limit(32 * 1024 * 1024)),
        name="ada_mod",
    )(c, w, b.reshape(n, 1, d3))
    return out.reshape(n, bsz, 3, d)


def _conv_mixer_kernel(x_ref, mod_ref, w1_ref, b1_ref, wdw_ref, bdw_ref, gcn_ref, bcn_ref,
                       w2_ref, b2_ref, pg_ref, pb_ref, o_ref, ext_ref, v_ref):
    ts, d = x_ref.shape[1], x_ref.shape[2]
    s_idx = pl.program_id(1)

    @pl.when(s_idx == 0)
    def _():
        ext_ref[0:CONV_HALO, :] = jnp.zeros((CONV_HALO, d), F32)

    mod = mod_ref[0]
    shift, scale, gate = mod[0:1], mod[1:2], mod[2:3]
    x = x_ref[0]
    h = (x * (1.0 + scale) + shift).astype(BF16)
    a = jnp.dot(h, w1_ref[...], preferred_element_type=F32) + b1_ref[...]
    ext_ref[CONV_HALO:CONV_HALO + ts, :] = a[:, :d] * jax.nn.sigmoid(a[:, d:])

    off0 = CONV_HALO - (CONV_WIDTH - 1)
    win = CONV_ROWS + CONV_HALO
    lane_chunk = 4 * V7X_LANES

    def row_block(rb, carry):
        t0 = pl.multiple_of(rb * CONV_ROWS, CONV_ROWS)
        parts = []
        for c0 in range(0, d, lane_chunk):
            w_win = ext_ref[pl.ds(t0, win), c0:c0 + lane_chunk]
            acc = jnp.broadcast_to(bdw_ref[:, c0:c0 + lane_chunk], (CONV_ROWS, lane_chunk))
            for r in range(V7X_SUBLANES):
                w_r = w_win if r == 0 else pltpu.roll(w_win, win - r, axis=0)
                for grp in range(CONV_HALO // V7X_SUBLANES + 1):
                    j = grp * V7X_SUBLANES + r - off0
                    if 0 <= j < CONV_WIDTH:
                        lo = grp * V7X_SUBLANES
                        acc = acc + wdw_ref[j:j + 1, c0:c0 + lane_chunk] * w_r[lo:lo + CONV_ROWS]
            parts.append(acc)
        y = jnp.concatenate(parts, axis=-1)
        y = _layer_norm(y, gcn_ref[...], bcn_ref[...])
        v_ref[pl.ds(t0, CONV_ROWS), :] = (y * jax.nn.sigmoid(y)).astype(BF16)
        return carry

    lax.fori_loop(0, ts // CONV_ROWS, row_block, 0)
    ext_ref[0:CONV_HALO, :] = ext_ref[ts:ts + CONV_HALO, :]

    y = jnp.dot(v_ref[...], w2_ref[...], preferred_element_type=F32) + b2_ref[...]
    o_ref[0] = _layer_norm(ALPHA * x_ref[0] + gate * y, pg_ref[...], pb_ref[...])


def _conv_mixer(x, mod, w1, b1, wdw, bdw, gcn, bcn, w2, b2, pg, pb):
    bsz, s, d = x.shape
    ts = SEQ_TILE
    row = lambda v: v.reshape(1, -1)
    return pl.pallas_call(
        _conv_mixer_kernel,
        grid=(bsz, s // ts),
        in_specs=[
            pl.BlockSpec((1, ts, d), lambda b, i: (b, i, 0)),
            pl.BlockSpec((1, 3, d), lambda b, i: (b, 0, 0)),
            _const_spec((d, 2 * d), True),
            _const_spec((1, 2 * d)),
            _const_spec((CONV_WIDTH, d)),
            _const_spec((1, d)),
            _const_spec((1, d)),
            _const_spec((1, d)),
            _const_spec((d, d), True),
            _const_spec((1, d)),
            _const_spec((1, d)),
            _const_spec((1, d)),
        ],
        out_specs=pl.BlockSpec((1, ts, d), lambda b, i: (b, i, 0)),
        out_shape=jax.ShapeDtypeStruct((bsz, s, d), F32),
        scratch_shapes=[
            pltpu.VMEM((ts + CONV_HALO, d), F32),
            pltpu.VMEM((ts, d), BF16),
        ],
        compiler_params=pltpu.CompilerParams(
            dimension_semantics=("arbitrary", "arbitrary"),
            vmem_limit_bytes=_vmem_limit(48 * 1024 * 1024)),
        name="conv_mixer",
    )(x, mod, w1.astype(BF16), row(b1), wdw, row(bdw), row(gcn), row(bcn),
      w2.astype(BF16), row(b2), row(pg), row(pb))


def _mlp_kernel(x_ref, mod_ref, w1_ref, w2_ref, pg_ref, pb_ref, o_ref):
    d_ff = w1_ref.shape[1]
    mod = mod_ref[0]
    shift, scale, gate = mod[0:1], mod[1:2], mod[2:3]
    x = x_ref[0]
    h = (x * (1.0 + scale) + shift).astype(BF16)
    y = None
    for c0 in range(0, d_ff, FF_CHUNK):
        a = jnp.dot(h, w1_ref[:, c0:c0 + FF_CHUNK], preferred_element_type=F32)
        a = jnp.maximum(a, 0.0)
        part = jnp.dot((a * a).astype(BF16), w2_ref[c0:c0 + FF_CHUNK, :], preferred_element_type=F32)
        y = part if y is None else y + part
    o_ref[0] = _layer_norm(ALPHA * x + gate * y, pg_ref[...], pb_ref[...])


def _mlp(x, mod, w1, w2, pg, pb):
    bsz, s, d = x.shape
    d_ff = w1.shape[1]
    ts = SEQ_TILE
    row = lambda v: v.reshape(1, -1)
    return pl.pallas_call(
        _mlp_kernel,
        grid=(bsz, s // ts),
        in_specs=[
            pl.BlockSpec((1, ts, d), lambda b, i: (b, i, 0)),
            pl.BlockSpec((1, 3, d), lambda b, i: (b, 0, 0)),
            _const_spec((d, d_ff), True),
            _const_spec((d_ff, d), True),
            _const_spec((1, d)),
            _const_spec((1, d)),
        ],
        out_specs=pl.BlockSpec((1, ts, d), lambda b, i: (b, i, 0)),
        out_shape=jax.ShapeDtypeStruct((bsz, s, d), F32),
        compiler_params=pltpu.CompilerParams(
            dimension_semantics=("arbitrary", "arbitrary"),
            vmem_limit_bytes=_vmem_limit(52 * 1024 * 1024)),
        name="sq_relu_mlp",
    )(x, mod, w1.astype(BF16), w2.astype(BF16), row(pg), row(pb))


def _qkv_kernel(x_ref, mod_ref, wq_ref, wk_ref, wvt_ref, q_ref, k_ref, vt_ref):
    mod = mod_ref[0]
    shift, scale = mod[0:1], mod[1:2]
    h = (x_ref[0] * (1.0 + scale) + shift).astype(BF16)
    q = jnp.dot(h, wq_ref[...], preferred_element_type=F32)
    q_ref[0] = (q * (HEAD_DIM ** -0.5 * LOG2_E)).astype(BF16)
    k_ref[0] = jnp.dot(h, wk_ref[...], preferred_element_type=F32).astype(BF16)
    vt = lax.dot_general(wvt_ref[...], h, (((1,), (1,)), ((), ())), preferred_element_type=F32)
    vt_ref[0, 0] = vt.astype(BF16)


def _qkv(x, mod, w_qkv):
    bsz, s, d = x.shape
    tk = ATTN_TILE
    dq = N_HEADS * 2 * HEAD_DIM
    dv = N_HEADS * V_DIM
    wq = w_qkv[:, :dq].astype(BF16)
    wk = w_qkv[:, dq:2 * dq].astype(BF16)
    wvt = w_qkv[:, 2 * dq:].T.astype(BF16)
    return pl.pallas_call(
        _qkv_kernel,
        grid=(bsz, s // tk),
        in_specs=[
            pl.BlockSpec((1, tk, d), lambda b, i: (b, i, 0)),
            pl.BlockSpec((1, 3, d), lambda b, i: (b, 0, 0)),
            _const_spec((d, dq), True),
            _const_spec((d, dq), True),
            _const_spec((dv, d), True),
        ],
        out_specs=[
            pl.BlockSpec((1, tk, dq), lambda b, i: (b, i, 0)),
            pl.BlockSpec((1, tk, dq), lambda b, i: (b, i, 0)),
            pl.BlockSpec((1, 1, dv, tk), lambda b, i: (b, i, 0, 0)),
        ],
        out_shape=[
            jax.ShapeDtypeStruct((bsz, s, dq), BF16),
            jax.ShapeDtypeStruct((bsz, s, dq), BF16),
            jax.ShapeDtypeStruct((bsz, s // tk, dv, tk), BF16),
        ],
        compiler_params=pltpu.CompilerParams(
            dimension_semantics=("arbitrary", "arbitrary"),
            vmem_limit_bytes=_vmem_limit(40 * 1024 * 1024)),
        name="qkv_proj",
    )(x, mod, wq, wk, wvt)


def _bias_tiles_kernel(rb_ref, o_ref):
    h = pl.program_id(0)
    t = o_ref.shape[2]
    far = rb_ref[h, REL_BUCKETS - 1]
    kk = lax.broadcasted_iota(jnp.int32, (t, t), 0)
    qq = lax.broadcasted_iota(jnp.int32, (t, t), 1)
    for dlt in range(o_ref.shape[1]):
        rel = dlt * t + qq - kk
        val = jnp.full((t, t), (rb_ref[h, 0] - far) * LOG2_E, F32)
        for bkt in range(1, REL_BUCKETS):
            val = jnp.where(rel >= T5_THRESHOLDS[bkt], (rb_ref[h, bkt] - far) * LOG2_E, val)
        o_ref[0, dlt] = jnp.where(rel >= 0, val, MASK_VALUE)


def _bias_tiles(rel_bias):
    t = ATTN_TILE
    n_near = -(-(T5_LAST_BUCKET_START + t - 1) // t)
    return pl.pallas_call(
        _bias_tiles_kernel,
        grid=(N_HEADS,),
        in_specs=[pl.BlockSpec(memory_space=pltpu.SMEM)],
        out_specs=pl.BlockSpec((1, n_near, t, t), lambda h: (h, 0, 0, 0)),
        out_shape=jax.ShapeDtypeStruct((N_HEADS, n_near, t, t), F32),
        compiler_params=pltpu.CompilerParams(dimension_semantics=("arbitrary",)),
        name="t5_bias_tiles",
    )(rel_bias.T)


_TAB_QI, _TAB_KJ, _TAB_BIAS = range(3)
_PIPE_DEPTH = 1
_TICKS_PER_ITER = 2
ATTN_COL_BLOCK = 256
ONES_ROWS = 16


def _attn_schedule(pairs, inert):
    cols = [inert] * _PIPE_DEPTH + list(pairs) + [inert] * _PIPE_DEPTH
    n_ticks = len(cols) - _PIPE_DEPTH
    cols += [inert] * (-n_ticks % _TICKS_PER_ITER)
    return np.asarray(cols, np.int32).T


def _attn_schedules(nq, n_near):
    far = [(qi, kj, 0) for qi in range(nq) for kj in range(qi - n_near + 1)]
    near = [(qi, qi - d, d) for qi in range(nq) for d in reversed(range(min(n_near, qi + 1)))]
    return _attn_schedule(far, (nq, 0, 0)), _attn_schedule(near, (nq, 0, 0))


def _attn_kernel(far_ref, near_ref, q_ref, k_ref, vt_ref, bias_ref, lam_ref, g_ref, o_ref,
                 s0_ref, s1_ref, mt0_ref, mt1_ref, qc_ref, m_ref, acc_ref):
    s_refs, mt_refs = (s0_ref, s1_ref), (mt0_ref, mt1_ref)
    n_cb, t, cb = s0_ref.shape
    nq = m_ref.shape[0] - 1

    for slot in range(2):
        s_refs[slot][...] = jnp.zeros(s_refs[slot].shape, F32)
        mt_refs[slot][...] = jnp.zeros(mt_refs[slot].shape, F32)
    m_ref[...] = jnp.full(m_ref.shape, MASK_VALUE, F32)
    acc_ref[...] = jnp.zeros(acc_ref.shape, F32)

    def mask_q(qi, carry):
        q = q_ref[0, pl.ds(pl.multiple_of(qi * t, t), t), :]
        lane = lax.broadcasted_iota(jnp.int32, q.shape, 1)
        zero = jnp.zeros_like(q)
        qcat = jnp.concatenate([jnp.where(lane < HEAD_DIM, q, zero),
                                jnp.where(lane >= HEAD_DIM, q, zero)], axis=0)
        qc_ref[qi] = qcat.reshape(n_cb, cb, q.shape[1])
        return carry

    lax.fori_loop(0, nq, mask_q, 0)

    def q_tile_index(qi):
        return jnp.where(qi == nq, 0, qi)

    def tick(tab_ref, n, par, with_bias):
        col_a, col_b = n + 1, n
        cur = 1 - par

        qi_a = q_tile_index(tab_ref[_TAB_QI, col_a])
        qi_b = tab_ref[_TAB_QI, col_b]
        bias_idx = tab_ref[_TAB_BIAS, col_a]
        vt = vt_ref[0, tab_ref[_TAB_KJ, col_b]]
        vt_aug = jnp.concatenate([vt, jnp.ones((ONES_ROWS, t), BF16)], axis=0)
        k = k_ref[0, pl.ds(pl.multiple_of(tab_ref[_TAB_KJ, col_a] * t, t), t), :]

        for c in range(n_cb):
            m_old = m_ref[qi_b, c]
            m_new = jnp.maximum(m_old, mt_refs[cur][c])
            m_ref[qi_b, c] = m_new
            p = jnp.exp2(s_refs[cur][c] - m_new).astype(BF16)
            pv = jnp.dot(vt_aug, p, preferred_element_type=F32)
            acc_ref[qi_b, c] = jnp.exp2(m_old - m_new) * acc_ref[qi_b, c] + pv

            st = lax.dot_general(k, qc_ref[qi_a, c], (((1,), (1,)), ((), ())),
                                 preferred_element_type=F32)
            if with_bias:
                st = st + bias_ref[0, bias_idx, :, pl.ds((c * cb) % t, cb)]
            s_refs[par][c] = st
            mt_refs[par][c] = jnp.max(st, axis=0, keepdims=True)

    def run(tab_ref, with_bias):
        n_ticks = tab_ref.shape[1] - _PIPE_DEPTH
        assert n_ticks % _TICKS_PER_ITER == 0

        def body(i, carry):
            for u in range(_TICKS_PER_ITER):
                tick(tab_ref, _TICKS_PER_ITER * i + u, u % 2, with_bias)
            return carry

        lax.fori_loop(0, n_ticks // _TICKS_PER_ITER, body, 0)

    run(far_ref, False)
    run(near_ref, True)

    lam_v = lam_ref[...]
    lam = (jnp.exp(jnp.sum(lam_v[0:1] * lam_v[1:2], axis=-1, keepdims=True))
           - jnp.exp(jnp.sum(lam_v[2:3] * lam_v[3:4], axis=-1, keepdims=True)) + LAMBDA_INIT)

    def finish(qi, carry):
        def normalised(c):
            acc = acc_ref[qi, c]
            return acc[:V_DIM] * (1.0 / acc[V_DIM:V_DIM + 1])
        half = n_cb // 2
        o = jnp.concatenate([normalised(c) - lam * normalised(half + c) for c in range(half)],
                            axis=1).T
        o = (o * lax.rsqrt(jnp.mean(o * o, axis=-1, keepdims=True) + LN_EPS)
             * g_ref[...] * (1.0 - LAMBDA_INIT))
        o_ref[0, pl.ds(pl.multiple_of(qi * t, t), t), :] = o.astype(BF16)
        return carry

    lax.fori_loop(0, nq, finish, 0)


def _attention(q, k, vt, bias, lam_vecs, g_sub):
    bsz, s, dq = q.shape
    t = ATTN_TILE
    nq = s // t
    n_near = bias.shape[1]
    hw = 2 * HEAD_DIM
    far_tab, near_tab = _attn_schedules(nq, n_near)
    cb = ATTN_COL_BLOCK
    n_cb = 2 * t // cb
    return pl.pallas_call(
        _attn_kernel,
        grid=(bsz, N_HEADS),
        in_specs=[
            pl.BlockSpec(memory_space=pltpu.SMEM),
            pl.BlockSpec(memory_space=pltpu.SMEM),
            pl.BlockSpec((1, s, hw), lambda b, h: (b, 0, h)),
            pl.BlockSpec((1, s, hw), lambda b, h: (b, 0, h)),
            pl.BlockSpec((1, nq, V_DIM, t), lambda b, h: (b, 0, h, 0)),
            pl.BlockSpec((1, n_near, t, t), lambda b, h: (h, 0, 0, 0)),
            _const_spec((4, HEAD_DIM)),
            _const_spec((1, V_DIM)),
        ],
        out_specs=pl.BlockSpec((1, s, V_DIM), lambda b, h: (b, 0, h)),
        out_shape=jax.ShapeDtypeStruct((bsz, s, N_HEADS * V_DIM), BF16),
        scratch_shapes=[
            pltpu.VMEM((n_cb, t, cb), F32),
            pltpu.VMEM((n_cb, t, cb), F32),
            pltpu.VMEM((n_cb, 1, cb), F32),
            pltpu.VMEM((n_cb, 1, cb), F32),
            pltpu.VMEM((nq, n_cb, cb, hw), BF16),
            pltpu.VMEM((nq + 1, n_cb, 1, cb), F32),
            pltpu.VMEM((nq + 1, n_cb, V_DIM + ONES_ROWS, cb), F32),
        ],
        compiler_params=pltpu.CompilerParams(
            dimension_semantics=("arbitrary", "arbitrary"),
            vmem_limit_bytes=_vmem_limit(52 * 1024 * 1024)),
        name="diff_attention",
    )(jnp.asarray(far_tab), jnp.asarray(near_tab), q, k, vt, bias, lam_vecs,
      g_sub.reshape(1, V_DIM))


def _out_proj_kernel(a_ref, x_ref, mod_ref, wo_ref, pg_ref, pb_ref, o_ref):
    gate = mod_ref[0][2:3]
    y = jnp.dot(a_ref[0], wo_ref[...], preferred_element_type=F32)
    o_ref[0] = _layer_norm(ALPHA * x_ref[0] + gate * y, pg_ref[...], pb_ref[...])


def _out_proj(a, x, mod, wo, pg, pb):
    bsz, s, d = x.shape
    da = a.shape[2]
    ts = SEQ_TILE
    row = lambda v: v.reshape(1, -1)
    return pl.pallas_call(
        _out_proj_kernel,
        grid=(bsz, s // ts),
        in_specs=[
            pl.BlockSpec((1, ts, da), lambda b, i: (b, i, 0)),
            pl.BlockSpec((1, ts, d), lambda b, i: (b, i, 0)),
            pl.BlockSpec((1, 3, d), lambda b, i: (b, 0, 0)),
            _const_spec((da, d), True),
            _const_spec((1, d)),
            _const_spec((1, d)),
        ],
        out_specs=pl.BlockSpec((1, ts, d), lambda b, i: (b, i, 0)),
        out_shape=jax.ShapeDtypeStruct((bsz, s, d), F32),
        compiler_params=pltpu.CompilerParams(
            dimension_semantics=("arbitrary", "arbitrary"),
            vmem_limit_bytes=_vmem_limit(32 * 1024 * 1024)),
        name="attn_out_proj",
    )(a, x, mod, wo.astype(BF16), row(pg), row(pb))


def kernel(x, c, conv_mod_w, conv_mod_b, conv_pw1_w, conv_pw1_b, conv_dw_w, conv_dw_b, conv_norm_g, conv_norm_b, conv_pw2_w, conv_pw2_b, attn_mod_w, attn_mod_b, attn_qkv_w, attn_lam_q1, attn_lam_k1, attn_lam_q2, attn_lam_k2, attn_subln_g, attn_out_w, rel_bias, mlp_mod_w, mlp_mod_b, mlp_w1, mlp_w2, post_mix_g, post_mix_b, post_mlp_g, post_mlp_b):
    assert x.shape[1] % SEQ_TILE == 0 and x.shape[1] % ATTN_TILE == 0
    assert SEQ_TILE % CONV_ROWS == 0 and CONV_HALO >= CONV_WIDTH - 1
    conv_mod = _ada_mod(c, conv_mod_w, conv_mod_b)
    attn_mod = _ada_mod(c, attn_mod_w, attn_mod_b)
    mlp_mod = _ada_mod(c, mlp_mod_w, mlp_mod_b)

    x = _conv_mixer(x, conv_mod[0], conv_pw1_w[0], conv_pw1_b[0], conv_dw_w[0], conv_dw_b[0],
                    conv_norm_g[0], conv_norm_b[0], conv_pw2_w[0], conv_pw2_b[0],
                    post_mix_g[0], post_mix_b[0])
    x = _mlp(x, mlp_mod[0], mlp_w1[0], mlp_w2[0], post_mlp_g[0], post_mlp_b[0])

    q, k, vt = _qkv(x, attn_mod[0], attn_qkv_w[0])
    bias = _bias_tiles(rel_bias)
    lam_vecs = jnp.stack([attn_lam_q1[0], attn_lam_k1[0], attn_lam_q2[0], attn_lam_k2[0]])
    a = _attention(q, k, vt, bias, lam_vecs, attn_subln_g[0])
    x = _out_proj(a, x, attn_mod[0], attn_out_w[0], post_mix_g[1], post_mix_b[1])
    x = _mlp(x, mlp_mod[1], mlp_w1[1], mlp_w2[1], post_mlp_g[1], post_mlp_b[1])
    return x
```

```python
import functools
import math

import jax
import jax.numpy as jnp
import numpy as np
from jax import lax
from jax.experimental import pallas as pl
from jax.experimental.pallas import tpu as pltpu

DEPTH = 2
CONV_WIDTH = 31
N_HEADS = 8
HEAD_DIM = 64
V_DIM = 2 * HEAD_DIM
REL_BUCKETS = 32
REL_MAX_DIST = 128
ALPHA = (2 * DEPTH) ** 0.25
LN_EPS = 1e-5
ATTN_LAYER = 1
LAMBDA_INIT = 0.8 - 0.6 * math.exp(-0.3 * ATTN_LAYER)
LOG2_E = math.log2(math.e)

V7X_SUBLANES = 8
V7X_LANES = 128
V7X_VMEM_BYTES = 64 * 1024 * 1024

MASK_VALUE = -1e30
CONV_HALO = 32
CONV_ROWS = 64
SEQ_TILE = 512
ATTN_TILE = 512
FF_CHUNK = 1024

F32 = jnp.float32
BF16 = jnp.bfloat16


def _t5_thresholds():
    max_exact = REL_BUCKETS // 2
    buckets = []
    for n in range(2 * REL_MAX_DIST):
        if n < max_exact:
            buckets.append(n)
        else:
            v = math.log(n / max_exact) / math.log(REL_MAX_DIST / max_exact) * (REL_BUCKETS - max_exact)
            buckets.append(min(max_exact + int(v), REL_BUCKETS - 1))
    assert all(b1 >= b0 for b0, b1 in zip(buckets, buckets[1:]))
    assert buckets[-1] == REL_BUCKETS - 1
    return [buckets.index(b) for b in range(REL_BUCKETS)]


T5_THRESHOLDS = _t5_thresholds()
T5_LAST_BUCKET_START = T5_THRESHOLDS[REL_BUCKETS - 1]


def _vmem_limit(nbytes):
    return int(min(nbytes, V7X_VMEM_BYTES - 8 * 1024 * 1024))


def _layer_norm(z, g, b):
    mu = jnp.mean(z, axis=-1, keepdims=True)
    zc = z - mu
    var = jnp.mean(zc * zc, axis=-1, keepdims=True)
    return zc * lax.rsqrt(var + LN_EPS) * g + b


def _const_spec(shape, single_buffer=False):
    nd = len(shape)
    kwargs = {"pipeline_mode": pl.Buffered(1)} if single_buffer else {}
    return pl.BlockSpec(shape, lambda *_: (0,) * nd, **kwargs)


def _ada_mod_kernel(c_ref, w_ref, b_ref, o_ref):
    c = c_ref[...]
    sc = c * jax.nn.sigmoid(c)
    o_ref[0] = jnp.dot(sc, w_ref[0], preferred_element_type=F32,
                       precision=lax.Precision.HIGHEST) + b_ref[0]


def _ada_mod(c, w, b):
    n, d, d3 = w.shape
    bsz = c.shape[0]
    nblk = d3 // d
    out = pl.pallas_call(
        _ada_mod_kernel,
        grid=(n, nblk),
        in_specs=[
            pl.BlockSpec((bsz, d), lambda i, j: (0, 0)),
            pl.BlockSpec((1, d, d), lambda i, j: (i, 0, j)),
            pl.BlockSpec((1, 1, d), lambda i, j: (i, 0, j)),
        ],
        out_specs=pl.BlockSpec((1, bsz, d), lambda i, j: (i, 0, j)),
        out_shape=jax.ShapeDtypeStruct((n, bsz, d3), F32),
        compiler_params=pltpu.CompilerParams(
            dimension_semantics=("arbitrary", "arbitrary"),
            vmem_limit_bytes=_vmem_limit(32 * 1024 * 1024)),
        name="ada_mod",
    )(c, w, b.reshape(n, 1, d3))
    return out.reshape(n, bsz, 3, d)


def _conv_mixer_kernel(x_ref, mod_ref, w1_ref, b1_ref, wdw_ref, bdw_ref, gcn_ref, bcn_ref,
                       w2_ref, b2_ref, pg_ref, pb_ref, o_ref, ext_ref, y_ref):
    ts, d = x_ref.shape[1], x_ref.shape[2]
    n_lb = ext_ref.shape[0]
    s_idx = pl.program_id(1)

    @pl.when(s_idx == 0)
    def _():
        ext_ref[:, 0:CONV_HALO, :] = jnp.zeros((n_lb, CONV_HALO, V7X_LANES), F32)

    mod = mod_ref[0]
    shift, scale, gate = mod[0:1], mod[1:2], mod[2:3]
    x = x_ref[0]
    h = (x * (1.0 + scale) + shift).astype(BF16)
    a = jnp.dot(h, w1_ref[...], preferred_element_type=F32) + b1_ref[...]
    u = a[:, :d] * jax.nn.sigmoid(a[:, d:])
    for lb in range(n_lb):
        ext_ref[lb, CONV_HALO:CONV_HALO + ts, :] = u[:, lb * V7X_LANES:(lb + 1) * V7X_LANES]

    off0 = CONV_HALO - (CONV_WIDTH - 1)

    n_grp = CONV_ROWS // V7X_SUBLANES

    def row_block(rb, carry):
        t0 = pl.multiple_of(rb * CONV_ROWS, CONV_ROWS)

        def lane_block(lb, carry2):
            w_all = wdw_ref[lb]
            acc = [jnp.broadcast_to(bdw_ref[lb], (V7X_SUBLANES, V7X_LANES))] * n_grp
            for r in range(V7X_SUBLANES):
                taps = [j for j in range(CONV_WIDTH) if (off0 + j) % V7X_SUBLANES == r]
                first = off0 + taps[0]
                n_load = n_grp + (taps[-1] - taps[0]) // V7X_SUBLANES
                groups = [ext_ref[lb, pl.ds(t0 + (first + V7X_SUBLANES * g), V7X_SUBLANES), :]
                          for g in range(n_load)]
                for j in taps:
                    w_row = jnp.broadcast_to(w_all[j:j + 1], (V7X_SUBLANES, V7X_LANES))
                    g0 = (off0 + j - first) // V7X_SUBLANES
                    acc = [acc[i] + w_row * groups[g0 + i] for i in range(n_grp)]
            y_ref[lb, pl.ds(t0, CONV_ROWS), :] = jnp.concatenate(acc, axis=0)
            return carry2

        lax.fori_loop(0, n_lb, lane_block, 0)
        return carry

    lax.fori_loop(0, ts // CONV_ROWS, row_block, 0)
    ext_ref[:, 0:CONV_HALO, :] = ext_ref[:, ts:ts + CONV_HALO, :]

    y = y_ref[...]
    mu = jnp.sum(jnp.sum(y, axis=0, keepdims=True), axis=2, keepdims=True) * (1.0 / d)
    yc = y - mu
    var = jnp.sum(jnp.sum(yc * yc, axis=0, keepdims=True), axis=2, keepdims=True) * (1.0 / d)
    y = yc * lax.rsqrt(var + LN_EPS) * gcn_ref[...] + bcn_ref[...]
    y = (y * jax.nn.sigmoid(y)).astype(BF16)
    v = jnp.concatenate([y[lb] for lb in range(n_lb)], axis=-1)
    y = jnp.dot(v, w2_ref[...], preferred_element_type=F32) + b2_ref[...]
    o_ref[0] = _layer_norm(ALPHA * x_ref[0] + gate * y, pg_ref[...], pb_ref[...])


def _conv_mixer(x, mod, w1, b1, wdw, bdw, gcn, bcn, w2, b2, pg, pb):
    bsz, s, d = x.shape
    ts = SEQ_TILE
    n_lb = d // V7X_LANES
    row = lambda v: v.reshape(1, -1)
    slab = lambda v: v.reshape(-1, n_lb, V7X_LANES).transpose(1, 0, 2)
    return pl.pallas_call(
        _conv_mixer_kernel,
        grid=(bsz, s // ts),
        in_specs=[
            pl.BlockSpec((1, ts, d), lambda b, i: (b, i, 0)),
            pl.BlockSpec((1, 3, d), lambda b, i: (b, 0, 0)),
            _const_spec((d, 2 * d), True),
            _const_spec((1, 2 * d)),
            _const_spec((n_lb, CONV_WIDTH, V7X_LANES)),
            _const_spec((n_lb, 1, V7X_LANES)),
            _const_spec((n_lb, 1, V7X_LANES)),
            _const_spec((n_lb, 1, V7X_LANES)),
            _const_spec((d, d), True),
            _const_spec((1, d)),
            _const_spec((1, d)),
            _const_spec((1, d)),
        ],
        out_specs=pl.BlockSpec((1, ts, d), lambda b, i: (b, i, 0)),
        out_shape=jax.ShapeDtypeStruct((bsz, s, d), F32),
        scratch_shapes=[
            pltpu.VMEM((n_lb, ts + CONV_HALO, V7X_LANES), F32),
            pltpu.VMEM((n_lb, ts, V7X_LANES), F32),
        ],
        compiler_params=pltpu.CompilerParams(
            dimension_semantics=("arbitrary", "arbitrary"),
            vmem_limit_bytes=_vmem_limit(48 * 1024 * 1024)),
        name="conv_mixer",
    )(x, mod, w1.astype(BF16), row(b1), slab(wdw), slab(bdw), slab(gcn), slab(bcn),
      w2.astype(BF16), row(b2), row(pg), row(pb))


def _mlp_kernel(x_ref, mod_ref, w1_ref, w2_ref, pg_ref, pb_ref, o_ref):
    d_ff = w1_ref.shape[1]
    mod = mod_ref[0]
    shift, scale, gate = mod[0:1], mod[1:2], mod[2:3]
    x = x_ref[0]
    h = (x * (1.0 + scale) + shift).astype(BF16)
    y = None
    for c0 in range(0, d_ff, FF_CHUNK):
        a = jnp.dot(h, w1_ref[:, c0:c0 + FF_CHUNK], preferred_element_type=F32)
        a = jnp.maximum(a, 0.0)
        part = jnp.dot((a * a).astype(BF16), w2_ref[c0:c0 + FF_CHUNK, :], preferred_element_type=F32)
        y = part if y is None else y + part
    o_ref[0] = _layer_norm(ALPHA * x + gate * y, pg_ref[...], pb_ref[...])


def _mlp(x, mod, w1, w2, pg, pb):
    bsz, s, d = x.shape
    d_ff = w1.shape[1]
    ts = SEQ_TILE
    row = lambda v: v.reshape(1, -1)
    return pl.pallas_call(
        _mlp_kernel,
        grid=(bsz, s // ts),
        in_specs=[
            pl.BlockSpec((1, ts, d), lambda b, i: (b, i, 0)),
            pl.BlockSpec((1, 3, d), lambda b, i: (b, 0, 0)),
            _const_spec((d, d_ff), True),
            _const_spec((d_ff, d), True),
            _const_spec((1, d)),
            _const_spec((1, d)),
        ],
        out_specs=pl.BlockSpec((1, ts, d), lambda b, i: (b, i, 0)),
        out_shape=jax.ShapeDtypeStruct((bsz, s, d), F32),
        compiler_params=pltpu.CompilerParams(
            dimension_semantics=("arbitrary", "arbitrary"),
            vmem_limit_bytes=_vmem_limit(52 * 1024 * 1024)),
        name="sq_relu_mlp",
    )(x, mod, w1.astype(BF16), w2.astype(BF16), row(pg), row(pb))


def _qkv_kernel(x_ref, mod_ref, wq_ref, wk_ref, wvt_ref, q_ref, k_ref, vt_ref):
    mod = mod_ref[0]
    shift, scale = mod[0:1], mod[1:2]
    h = (x_ref[0] * (1.0 + scale) + shift).astype(BF16)
    q = jnp.dot(h, wq_ref[...], preferred_element_type=F32)
    q_ref[0] = (q * (HEAD_DIM ** -0.5 * LOG2_E)).astype(BF16)
    k_ref[0] = jnp.dot(h, wk_ref[...], preferred_element_type=F32).astype(BF16)
    vt = lax.dot_general(wvt_ref[...], h, (((1,), (1,)), ((), ())), preferred_element_type=F32)
    vt_ref[0, 0] = vt.astype(BF16)


def _qkv(x, mod, w_qkv):
    bsz, s, d = x.shape
    tk = ATTN_TILE
    dq = N_HEADS * 2 * HEAD_DIM
    dv = N_HEADS * V_DIM
    wq = w_qkv[:, :dq].astype(BF16)
    wk = w_qkv[:, dq:2 * dq].astype(BF16)
    wvt = w_qkv[:, 2 * dq:].T.astype(BF16)
    return pl.pallas_call(
        _qkv_kernel,
        grid=(bsz, s // tk),
        in_specs=[
            pl.BlockSpec((1, tk, d), lambda b, i: (b, i, 0)),
            pl.BlockSpec((1, 3, d), lambda b, i: (b, 0, 0)),
            _const_spec((d, dq), True),
            _const_spec((d, dq), True),
            _const_spec((dv, d), True),
        ],
        out_specs=[
            pl.BlockSpec((1, tk, dq), lambda b, i: (b, i, 0)),
            pl.BlockSpec((1, tk, dq), lambda b, i: (b, i, 0)),
            pl.BlockSpec((1, 1, dv, tk), lambda b, i: (b, i, 0, 0)),
        ],
        out_shape=[
            jax.ShapeDtypeStruct((bsz, s, dq), BF16),
            jax.ShapeDtypeStruct((bsz, s, dq), BF16),
            jax.ShapeDtypeStruct((bsz, s // tk, dv, tk), BF16),
        ],
        compiler_params=pltpu.CompilerParams(
            dimension_semantics=("arbitrary", "arbitrary"),
            vmem_limit_bytes=_vmem_limit(40 * 1024 * 1024)),
        name="qkv_proj",
    )(x, mod, wq, wk, wvt)


def _bias_tiles_kernel(rb_ref, o_ref):
    h = pl.program_id(0)
    t = o_ref.shape[2]
    far = rb_ref[h, REL_BUCKETS - 1]
    kk = lax.broadcasted_iota(jnp.int32, (t, t), 0)
    qq = lax.broadcasted_iota(jnp.int32, (t, t), 1)
    for dlt in range(o_ref.shape[1]):
        rel = dlt * t + qq - kk
        val = jnp.full((t, t), (rb_ref[h, 0] - far) * LOG2_E, F32)
        for bkt in range(1, REL_BUCKETS):
            val = jnp.where(rel >= T5_THRESHOLDS[bkt], (rb_ref[h, bkt] - far) * LOG2_E, val)
        o_ref[0, dlt] = jnp.where(rel >= 0, val, MASK_VALUE)


def _bias_tiles(rel_bias):
    t = ATTN_TILE
    n_near = -(-(T5_LAST_BUCKET_START + t - 1) // t)
    return pl.pallas_call(
        _bias_tiles_kernel,
        grid=(N_HEADS,),
        in_specs=[pl.BlockSpec(memory_space=pltpu.SMEM)],
        out_specs=pl.BlockSpec((1, n_near, t, t), lambda h: (h, 0, 0, 0)),
        out_shape=jax.ShapeDtypeStruct((N_HEADS, n_near, t, t), F32),
        compiler_params=pltpu.CompilerParams(dimension_semantics=("arbitrary",)),
        name="t5_bias_tiles",
    )(rel_bias.T)


_TAB_QI, _TAB_KJ, _TAB_BIAS = range(3)
_PIPE_DEPTH = 1
_TICKS_PER_ITER = 8
ATTN_COL_BLOCK = 256


def _attn_schedule(pairs, inert):
    cols = [inert] * _PIPE_DEPTH + list(pairs) + [inert] * _PIPE_DEPTH
    n_ticks = len(cols) - _PIPE_DEPTH
    cols += [inert] * (-n_ticks % _TICKS_PER_ITER)
    return np.asarray(cols, np.int32).T


def _attn_schedules(nq, n_near):
    far = [(qi, kj, 0) for qi in range(nq) for kj in range(qi - n_near + 1)]
    near = [(qi, qi - d, d) for qi in range(nq) for d in reversed(range(min(n_near, qi + 1)))]
    return _attn_schedule(far, (nq, 0, 0)), _attn_schedule(near, (nq, 0, 0))


def _attn_kernel(far_ref, near_ref, q_ref, k_ref, vt_ref, bias_ref, lam_ref, g_ref, o_ref,
                 s0_ref, s1_ref, mt0_ref, mt1_ref, qc_ref, m_ref, l_ref, acc_ref):
    s_refs, mt_refs = (s0_ref, s1_ref), (mt0_ref, mt1_ref)
    n_cb, t, cb = s0_ref.shape
    nq = m_ref.shape[0] - 1

    for slot in range(2):
        s_refs[slot][...] = jnp.zeros(s_refs[slot].shape, F32)
        mt_refs[slot][...] = jnp.zeros(mt_refs[slot].shape, F32)
    m_ref[...] = jnp.full(m_ref.shape, MASK_VALUE, F32)
    l_ref[...] = jnp.zeros(l_ref.shape, F32)
    acc_ref[...] = jnp.zeros(acc_ref.shape, F32)

    def mask_q(qi, carry):
        q = q_ref[0, pl.ds(pl.multiple_of(qi * t, t), t), :]
        lane = lax.broadcasted_iota(jnp.int32, q.shape, 1)
        zero = jnp.zeros_like(q)
        qcat = jnp.concatenate([jnp.where(lane < HEAD_DIM, q, zero),
                                jnp.where(lane >= HEAD_DIM, q, zero)], axis=0)
        qc_ref[qi] = qcat.reshape(n_cb, cb, q.shape[1])
        return carry

    lax.fori_loop(0, nq, mask_q, 0)

    def q_tile_index(qi):
        return jnp.where(qi == nq, 0, qi)

    def tick(tab_ref, n, par, with_bias):
        col_a, col_b = n + 1, n
        cur = 1 - par

        qi_a = q_tile_index(tab_ref[_TAB_QI, col_a])
        qi_b = tab_ref[_TAB_QI, col_b]
        bias_idx = tab_ref[_TAB_BIAS, col_a]
        vt = vt_ref[0, tab_ref[_TAB_KJ, col_b]]
        k = k_ref[0, pl.ds(pl.multiple_of(tab_ref[_TAB_KJ, col_a] * t, t), t), :]

        for c in range(n_cb):
            m_old = m_ref[qi_b, c]
            m_new = jnp.maximum(m_old, mt_refs[cur][c])
            m_ref[qi_b, c] = m_new
            p = jnp.exp2(s_refs[cur][c] - m_new)
            alpha = jnp.exp2(m_old - m_new)
            l_ref[qi_b, c] = alpha * l_ref[qi_b, c] + jnp.sum(p, axis=0, keepdims=True)
            pv = jnp.dot(vt, p.astype(BF16), preferred_element_type=F32)
            acc_ref[qi_b, c] = alpha * acc_ref[qi_b, c] + pv

            st = lax.dot_general(k, qc_ref[qi_a, c], (((1,), (1,)), ((), ())),
                                 preferred_element_type=F32)
            if with_bias:
                st = st + bias_ref[0, bias_idx, :, pl.ds((c * cb) % t, cb)]
            s_refs[par][c] = st
            mt_refs[par][c] = jnp.max(st, axis=0, keepdims=True)

    def run(tab_ref, with_bias):
        n_ticks = tab_ref.shape[1] - _PIPE_DEPTH
        assert n_ticks % _TICKS_PER_ITER == 0

        def body(i, carry):
            for u in range(_TICKS_PER_ITER):
                tick(tab_ref, _TICKS_PER_ITER * i + u, u % 2, with_bias)
            return carry

        lax.fori_loop(0, n_ticks // _TICKS_PER_ITER, body, 0)

    run(far_ref, False)
    run(near_ref, True)

    lam_v = lam_ref[...]
    lam = (jnp.exp(jnp.sum(lam_v[0:1] * lam_v[1:2], axis=-1, keepdims=True))
           - jnp.exp(jnp.sum(lam_v[2:3] * lam_v[3:4], axis=-1, keepdims=True)) + LAMBDA_INIT)

    def finish(qi, carry):
        def normalised(c):
            return acc_ref[qi, c] * (1.0 / l_ref[qi, c])
        half = n_cb // 2
        o = jnp.concatenate([normalised(c) - lam * normalised(half + c) for c in range(half)],
                            axis=1).T
        o = (o * lax.rsqrt(jnp.mean(o * o, axis=-1, keepdims=True) + LN_EPS)
             * g_ref[...] * (1.0 - LAMBDA_INIT))
        o_ref[0, pl.ds(pl.multiple_of(qi * t, t), t), :] = o.astype(BF16)
        return carry

    lax.fori_loop(0, nq, finish, 0)


def _attention(q, k, vt, bias, lam_vecs, g_sub):
    bsz, s, dq = q.shape
    t = ATTN_TILE
    nq = s // t
    n_near = bias.shape[1]
    hw = 2 * HEAD_DIM
    far_tab, near_tab = _attn_schedules(nq, n_near)
    cb = ATTN_COL_BLOCK
    n_cb = 2 * t // cb
    return pl.pallas_call(
        _attn_kernel,
        grid=(bsz, N_HEADS),
        in_specs=[
            pl.BlockSpec(memory_space=pltpu.SMEM),
            pl.BlockSpec(memory_space=pltpu.SMEM),
            pl.BlockSpec((1, s, hw), lambda b, h: (b, 0, h)),
            pl.BlockSpec((1, s, hw), lambda b, h: (b, 0, h)),
            pl.BlockSpec((1, nq, V_DIM, t), lambda b, h: (b, 0, h, 0)),
            pl.BlockSpec((1, n_near, t, t), lambda b, h: (h, 0, 0, 0)),
            _const_spec((4, HEAD_DIM)),
            _const_spec((1, V_DIM)),
        ],
        out_specs=pl.BlockSpec((1, s, V_DIM), lambda b, h: (b, 0, h)),
        out_shape=jax.ShapeDtypeStruct((bsz, s, N_HEADS * V_DIM), BF16),
        scratch_shapes=[
            pltpu.VMEM((n_cb, t, cb), F32),
            pltpu.VMEM((n_cb, t, cb), F32),
            pltpu.VMEM((n_cb, 1, cb), F32),
            pltpu.VMEM((n_cb, 1, cb), F32),
            pltpu.VMEM((nq, n_cb, cb, hw), BF16),
            pltpu.VMEM((nq + 1, n_cb, 1, cb), F32),
            pltpu.VMEM((nq + 1, n_cb, 1, cb), F32),
            pltpu.VMEM((nq + 1, n_cb, V_DIM, cb), F32),
        ],
        compiler_params=pltpu.CompilerParams(
            dimension_semantics=("arbitrary", "arbitrary"),
            vmem_limit_bytes=_vmem_limit(52 * 1024 * 1024)),
        name="diff_attention",
    )(jnp.asarray(far_tab), jnp.asarray(near_tab), q, k, vt, bias, lam_vecs,
      g_sub.reshape(1, V_DIM))


def _out_proj_kernel(a_ref, x_ref, mod_ref, wo_ref, pg_ref, pb_ref, o_ref):
    gate = mod_ref[0][2:3]
    y = jnp.dot(a_ref[0], wo_ref[...], preferred_element_type=F32)
    o_ref[0] = _layer_norm(ALPHA * x_ref[0] + gate * y, pg_ref[...], pb_ref[...])


def _out_proj(a, x, mod, wo, pg, pb):
    bsz, s, d = x.shape
    da = a.shape[2]
    ts = SEQ_TILE
    row = lambda v: v.reshape(1, -1)
    return pl.pallas_call(
        _out_proj_kernel,
        grid=(bsz, s // ts),
        in_specs=[
            pl.BlockSpec((1, ts, da), lambda b, i: (b, i, 0)),
            pl.BlockSpec((1, ts, d), lambda b, i: (b, i, 0)),
            pl.BlockSpec((1, 3, d), lambda b, i: (b, 0, 0)),
            _const_spec((da, d), True),
            _const_spec((1, d)),
            _const_spec((1, d)),
        ],
        out_specs=pl.BlockSpec((1, ts, d), lambda b, i: (b, i, 0)),
        out_shape=jax.ShapeDtypeStruct((bsz, s, d), F32),
        compiler_params=pltpu.CompilerParams(
            dimension_semantics=("arbitrary", "arbitrary"),
            vmem_limit_bytes=_vmem_limit(32 * 1024 * 1024)),
        name="attn_out_proj",
    )(a, x, mod, wo.astype(BF16), row(pg), row(pb))


def kernel(x, c, conv_mod_w, conv_mod_b, conv_pw1_w, conv_pw1_b, conv_dw_w, conv_dw_b, conv_norm_g, conv_norm_b, conv_pw2_w, conv_pw2_b, attn_mod_w, attn_mod_b, attn_qkv_w, attn_lam_q1, attn_lam_k1, attn_lam_q2, attn_lam_k2, attn_subln_g, attn_out_w, rel_bias, mlp_mod_w, mlp_mod_b, mlp_w1, mlp_w2, post_mix_g, post_mix_b, post_mlp_g, post_mlp_b):
    assert x.shape[1] % SEQ_TILE == 0 and x.shape[1] % ATTN_TILE == 0
    assert SEQ_TILE % CONV_ROWS == 0 and CONV_HALO >= CONV_WIDTH - 1
    conv_mod = _ada_mod(c, conv_mod_w, conv_mod_b)
    attn_mod = _ada_mod(c, attn_mod_w, attn_mod_b)
    mlp_mod = _ada_mod(c, mlp_mod_w, mlp_mod_b)

    x = _conv_mixer(x, conv_mod[0], conv_pw1_w[0], conv_pw1_b[0], conv_dw_w[0], conv_dw_b[0],
                    conv_norm_g[0], conv_norm_b[0], conv_pw2_w[0], conv_pw2_b[0],
                    post_mix_g[0], post_mix_b[0])
    x = _mlp(x, mlp_mod[0], mlp_w1[0], mlp_w2[0], post_mlp_g[0], post_mlp_b[0])

    q, k, vt = _qkv(x, attn_mod[0], attn_qkv_w[0])
    bias = _bias_tiles(rel_bias)
    lam_vecs = jnp.stack([attn_lam_q1[0], attn_lam_k1[0], attn_lam_q2[0], attn_lam_k2[0]])
    a = _attention(q, k, vt, bias, lam_vecs, attn_subln_g[0])
    x = _out_proj(a, x, attn_mod[0], attn_out_w[0], post_mix_g[1], post_mix_b[1])
    x = _mlp(x, mlp_mod[1], mlp_w1[1], mlp_w2[1], post_mlp_g[1], post_mlp_b[1])
    return x
```

```python
import functools
import math

import jax
import jax.numpy as jnp
import numpy as np
from jax import lax
from jax.experimental import pallas as pl
from jax.experimental.pallas import tpu as pltpu

DEPTH = 2
CONV_WIDTH = 31
N_HEADS = 8
HEAD_DIM = 64
V_DIM = 2 * HEAD_DIM
REL_BUCKETS = 32
REL_MAX_DIST = 128
ALPHA = (2 * DEPTH) ** 0.25
LN_EPS = 1e-5
ATTN_LAYER = 1
LAMBDA_INIT = 0.8 - 0.6 * math.exp(-0.3 * ATTN_LAYER)
LOG2_E = math.log2(math.e)

V7X_SUBLANES = 8
V7X_LANES = 128
V7X_VMEM_BYTES = 64 * 1024 * 1024

MASK_VALUE = -1e30
CONV_HALO = 32
CONV_ROWS = 64
SEQ_TILE = 512
ATTN_TILE = 512
FF_CHUNK = 1024

F32 = jnp.float32
BF16 = jnp.bfloat16


def _t5_thresholds():
    max_exact = REL_BUCKETS // 2
    buckets = []
    for n in range(2 * REL_MAX_DIST):
        if n < max_exact:
            buckets.append(n)
        else:
            v = math.log(n / max_exact) / math.log(REL_MAX_DIST / max_exact) * (REL_BUCKETS - max_exact)
            buckets.append(min(max_exact + int(v), REL_BUCKETS - 1))
    assert all(b1 >= b0 for b0, b1 in zip(buckets, buckets[1:]))
    assert buckets[-1] == REL_BUCKETS - 1
    return [buckets.index(b) for b in range(REL_BUCKETS)]


T5_THRESHOLDS = _t5_thresholds()
T5_LAST_BUCKET_START = T5_THRESHOLDS[REL_BUCKETS - 1]


def _vmem_limit(nbytes):
    return int(min(nbytes, V7X_VMEM_BYTES - 8 * 1024 * 1024))


def _layer_norm(z, g, b):
    mu = jnp.mean(z, axis=-1, keepdims=True)
    zc = z - mu
    var = jnp.mean(zc * zc, axis=-1, keepdims=True)
    return zc * lax.rsqrt(var + LN_EPS) * g + b


def _const_spec(shape, single_buffer=False):
    nd = len(shape)
    kwargs = {"pipeline_mode": pl.Buffered(1)} if single_buffer else {}
    return pl.BlockSpec(shape, lambda *_: (0,) * nd, **kwargs)


def _ada_mod_kernel(c_ref, w_ref, b_ref, o_ref):
    c = c_ref[...]
    sc = c * jax.nn.sigmoid(c)
    o_ref[0] = jnp.dot(sc, w_ref[0], preferred_element_type=F32,
                       precision=lax.Precision.HIGHEST) + b_ref[0]


def _ada_mod(c, w, b):
    n, d, d3 = w.shape
    bsz = c.shape[0]
    nblk = d3 // d
    out = pl.pallas_call(
        _ada_mod_kernel,
        grid=(n, nblk),
        in_specs=[
            pl.BlockSpec((bsz, d), lambda i, j: (0, 0)),
            pl.BlockSpec((1, d, d), lambda i, j: (i, 0, j)),
            pl.BlockSpec((1, 1, d), lambda i, j: (i, 0, j)),
        ],
        out_specs=pl.BlockSpec((1, bsz, d), lambda i, j: (i, 0, j)),
        out_shape=jax.ShapeDtypeStruct((n, bsz, d3), F32),
        compiler_params=pltpu.CompilerParams(
            dimension_semantics=("arbitrary", "arbitrary"),
            vmem_limit_bytes=_vmem_limit(32 * 1024 * 1024)),
        name="ada_mod",
    )(c, w, b.reshape(n, 1, d3))
    return out.reshape(n, bsz, 3, d)


def _conv_mixer_kernel(x_ref, mod_ref, w1_ref, b1_ref, wdw_ref, bdw_ref, gcn_ref, bcn_ref,
                       w2_ref, b2_ref, pg_ref, pb_ref, o_ref, ext_ref, y_ref):
    ts, d = x_ref.shape[1], x_ref.shape[2]
    n_lb = ext_ref.shape[0]
    s_idx = pl.program_id(1)

    @pl.when(s_idx == 0)
    def _():
        ext_ref[:, 0:CONV_HALO, :] = jnp.zeros((n_lb, CONV_HALO, V7X_LANES), F32)

    mod = mod_ref[0]
    shift, scale, gate = mod[0:1], mod[1:2], mod[2:3]
    x = x_ref[0]
    h = (x * (1.0 + scale) + shift).astype(BF16)
    a = jnp.dot(h, w1_ref[...], preferred_element_type=F32) + b1_ref[...]
    u = a[:, :d] * jax.nn.sigmoid(a[:, d:])
    for lb in range(n_lb):
        ext_ref[lb, CONV_HALO:CONV_HALO + ts, :] = u[:, lb * V7X_LANES:(lb + 1) * V7X_LANES]

    off0 = CONV_HALO - (CONV_WIDTH - 1)

    n_grp = CONV_ROWS // V7X_SUBLANES

    def row_block(rb, carry):
        t0 = pl.multiple_of(rb * CONV_ROWS, CONV_ROWS)

        def lane_block(lb, carry2):
            w_all = wdw_ref[lb]
            acc = [jnp.broadcast_to(bdw_ref[lb], (V7X_SUBLANES, V7X_LANES))] * n_grp
            for r in range(V7X_SUBLANES):
                taps = [j for j in range(CONV_WIDTH) if (off0 + j) % V7X_SUBLANES == r]
                first = off0 + taps[0]
                n_load = n_grp + (taps[-1] - taps[0]) // V7X_SUBLANES
                groups = [ext_ref[lb, pl.ds(t0 + (first + V7X_SUBLANES * g), V7X_SUBLANES), :]
                          for g in range(n_load)]
                for j in taps:
                    w_row = jnp.broadcast_to(w_all[j:j + 1], (V7X_SUBLANES, V7X_LANES))
                    g0 = (off0 + j - first) // V7X_SUBLANES
                    acc = [acc[i] + w_row * groups[g0 + i] for i in range(n_grp)]
            y_ref[lb, pl.ds(t0, CONV_ROWS), :] = jnp.concatenate(acc, axis=0)
            return carry2

        lax.fori_loop(0, n_lb, lane_block, 0)
        return carry

    lax.fori_loop(0, ts // CONV_ROWS, row_block, 0)
    ext_ref[:, 0:CONV_HALO, :] = ext_ref[:, ts:ts + CONV_HALO, :]

    y = y_ref[...]
    mu = jnp.sum(jnp.sum(y, axis=0, keepdims=True), axis=2, keepdims=True) * (1.0 / d)
    yc = y - mu
    var = jnp.sum(jnp.sum(yc * yc, axis=0, keepdims=True), axis=2, keepdims=True) * (1.0 / d)
    y = yc * lax.rsqrt(var + LN_EPS) * gcn_ref[...] + bcn_ref[...]
    y = (y * jax.nn.sigmoid(y)).astype(BF16)
    v = jnp.concatenate([y[lb] for lb in range(n_lb)], axis=-1)
    y = jnp.dot(v, w2_ref[...], preferred_element_type=F32) + b2_ref[...]
    o_ref[0] = _layer_norm(ALPHA * x_ref[0] + gate * y, pg_ref[...], pb_ref[...])


def _conv_mixer(x, mod, w1, b1, wdw, bdw, gcn, bcn, w2, b2, pg, pb):
    bsz, s, d = x.shape
    ts = SEQ_TILE
    n_lb = d // V7X_LANES
    row = lambda v: v.reshape(1, -1)
    slab = lambda v: v.reshape(-1, n_lb, V7X_LANES).transpose(1, 0, 2)
    return pl.pallas_call(
        _conv_mixer_kernel,
        grid=(bsz, s // ts),
        in_specs=[
            pl.BlockSpec((1, ts, d), lambda b, i: (b, i, 0)),
            pl.BlockSpec((1, 3, d), lambda b, i: (b, 0, 0)),
            _const_spec((d, 2 * d), True),
            _const_spec((1, 2 * d)),
            _const_spec((n_lb, CONV_WIDTH, V7X_LANES)),
            _const_spec((n_lb, 1, V7X_LANES)),
            _const_spec((n_lb, 1, V7X_LANES)),
            _const_spec((n_lb, 1, V7X_LANES)),
            _const_spec((d, d), True),
            _const_spec((1, d)),
            _const_spec((1, d)),
            _const_spec((1, d)),
        ],
        out_specs=pl.BlockSpec((1, ts, d), lambda b, i: (b, i, 0)),
        out_shape=jax.ShapeDtypeStruct((bsz, s, d), F32),
        scratch_shapes=[
            pltpu.VMEM((n_lb, ts + CONV_HALO, V7X_LANES), F32),
            pltpu.VMEM((n_lb, ts, V7X_LANES), F32),
        ],
        compiler_params=pltpu.CompilerParams(
            dimension_semantics=("arbitrary", "arbitrary"),
            vmem_limit_bytes=_vmem_limit(48 * 1024 * 1024)),
        name="conv_mixer",
    )(x, mod, w1.astype(BF16), row(b1), slab(wdw), slab(bdw), slab(gcn), slab(bcn),
      w2.astype(BF16), row(b2), row(pg), row(pb))


def _mlp_sublayer(x, mod, w1_ref, w2_ref, pg_ref, pb_ref):
    d_ff = w1_ref.shape[1]
    shift, scale, gate = mod[0:1], mod[1:2], mod[2:3]
    h = (x * (1.0 + scale) + shift).astype(BF16)
    y = None
    for c0 in range(0, d_ff, FF_CHUNK):
        a = jnp.dot(h, w1_ref[:, c0:c0 + FF_CHUNK], preferred_element_type=F32)
        a = jnp.maximum(a, 0.0)
        part = jnp.dot((a * a).astype(BF16), w2_ref[c0:c0 + FF_CHUNK, :], preferred_element_type=F32)
        y = part if y is None else y + part
    return _layer_norm(ALPHA * x + gate * y, pg_ref[...], pb_ref[...])


def _mlp_kernel(x_ref, mod_ref, w1_ref, w2_ref, pg_ref, pb_ref, o_ref):
    o_ref[0] = _mlp_sublayer(x_ref[0], mod_ref[0], w1_ref, w2_ref, pg_ref, pb_ref)


def _mlp(x, mod, w1, w2, pg, pb):
    bsz, s, d = x.shape
    d_ff = w1.shape[1]
    ts = SEQ_TILE
    row = lambda v: v.reshape(1, -1)
    return pl.pallas_call(
        _mlp_kernel,
        grid=(bsz, s // ts),
        in_specs=[
            pl.BlockSpec((1, ts, d), lambda b, i: (b, i, 0)),
            pl.BlockSpec((1, 3, d), lambda b, i: (b, 0, 0)),
            _const_spec((d, d_ff), True),
            _const_spec((d_ff, d), True),
            _const_spec((1, d)),
            _const_spec((1, d)),
        ],
        out_specs=pl.BlockSpec((1, ts, d), lambda b, i: (b, i, 0)),
        out_shape=jax.ShapeDtypeStruct((bsz, s, d), F32),
        compiler_params=pltpu.CompilerParams(
            dimension_semantics=("arbitrary", "arbitrary"),
            vmem_limit_bytes=_vmem_limit(52 * 1024 * 1024)),
        name="sq_relu_mlp",
    )(x, mod, w1.astype(BF16), w2.astype(BF16), row(pg), row(pb))


def _qkv_kernel(x_ref, mod_ref, wq_ref, wk_ref, wvt_ref, q_ref, k_ref, vt_ref):
    mod = mod_ref[0]
    shift, scale = mod[0:1], mod[1:2]
    h = (x_ref[0] * (1.0 + scale) + shift).astype(BF16)
    q = jnp.dot(h, wq_ref[...], preferred_element_type=F32)
    q_ref[0] = (q * (HEAD_DIM ** -0.5 * LOG2_E)).astype(BF16)
    k_ref[0] = jnp.dot(h, wk_ref[...], preferred_element_type=F32).astype(BF16)
    vt = lax.dot_general(wvt_ref[...], h, (((1,), (1,)), ((), ())), preferred_element_type=F32)
    vt_ref[0, 0] = vt.astype(BF16)


def _qkv(x, mod, w_qkv):
    bsz, s, d = x.shape
    tk = ATTN_TILE
    dq = N_HEADS * 2 * HEAD_DIM
    dv = N_HEADS * V_DIM
    wq = w_qkv[:, :dq].astype(BF16)
    wk = w_qkv[:, dq:2 * dq].astype(BF16)
    wvt = w_qkv[:, 2 * dq:].T.astype(BF16)
    return pl.pallas_call(
        _qkv_kernel,
        grid=(bsz, s // tk),
        in_specs=[
            pl.BlockSpec((1, tk, d), lambda b, i: (b, i, 0)),
            pl.BlockSpec((1, 3, d), lambda b, i: (b, 0, 0)),
            _const_spec((d, dq), True),
            _const_spec((d, dq), True),
            _const_spec((dv, d), True),
        ],
        out_specs=[
            pl.BlockSpec((1, tk, dq), lambda b, i: (b, i, 0)),
            pl.BlockSpec((1, tk, dq), lambda b, i: (b, i, 0)),
            pl.BlockSpec((1, 1, dv, tk), lambda b, i: (b, i, 0, 0)),
        ],
        out_shape=[
            jax.ShapeDtypeStruct((bsz, s, dq), BF16),
            jax.ShapeDtypeStruct((bsz, s, dq), BF16),
            jax.ShapeDtypeStruct((bsz, s // tk, dv, tk), BF16),
        ],
        compiler_params=pltpu.CompilerParams(
            dimension_semantics=("arbitrary", "arbitrary"),
            vmem_limit_bytes=_vmem_limit(40 * 1024 * 1024)),
        name="qkv_proj",
    )(x, mod, wq, wk, wvt)


def _bias_tiles_kernel(rb_ref, o_ref):
    h = pl.program_id(0)
    t = o_ref.shape[2]
    far = rb_ref[h, REL_BUCKETS - 1]
    kk = lax.broadcasted_iota(jnp.int32, (t, t), 0)
    qq = lax.broadcasted_iota(jnp.int32, (t, t), 1)
    for dlt in range(o_ref.shape[1]):
        rel = dlt * t + qq - kk
        val = jnp.full((t, t), (rb_ref[h, 0] - far) * LOG2_E, F32)
        for bkt in range(1, REL_BUCKETS):
            val = jnp.where(rel >= T5_THRESHOLDS[bkt], (rb_ref[h, bkt] - far) * LOG2_E, val)
        o_ref[0, dlt] = jnp.where(rel >= 0, val, MASK_VALUE)


def _bias_tiles(rel_bias):
    t = ATTN_TILE
    n_near = -(-(T5_LAST_BUCKET_START + t - 1) // t)
    return pl.pallas_call(
        _bias_tiles_kernel,
        grid=(N_HEADS,),
        in_specs=[pl.BlockSpec(memory_space=pltpu.SMEM)],
        out_specs=pl.BlockSpec((1, n_near, t, t), lambda h: (h, 0, 0, 0)),
        out_shape=jax.ShapeDtypeStruct((N_HEADS, n_near, t, t), F32),
        compiler_params=pltpu.CompilerParams(dimension_semantics=("arbitrary",)),
        name="t5_bias_tiles",
    )(rel_bias.T)


_TAB_QI, _TAB_KJ, _TAB_BIAS = range(3)
_PIPE_DEPTH = 1
_TICKS_PER_ITER = 8
ATTN_COL_BLOCK = 256


def _attn_schedule(pairs, inert):
    cols = [inert] * _PIPE_DEPTH + list(pairs) + [inert] * _PIPE_DEPTH
    n_ticks = len(cols) - _PIPE_DEPTH
    cols += [inert] * (-n_ticks % _TICKS_PER_ITER)
    return np.asarray(cols, np.int32).T


def _attn_schedules(nq, n_near):
    far = [(qi, kj, 0) for qi in range(nq) for kj in range(qi - n_near + 1)]
    near = [(qi, qi - d, d) for qi in range(nq) for d in reversed(range(min(n_near, qi + 1)))]
    return _attn_schedule(far, (nq, 0, 0)), _attn_schedule(near, (nq, 0, 0))


def _attn_kernel(far_ref, near_ref, q_ref, k_ref, vt_ref, bias_ref, lam_ref, g_ref, o_ref,
                 s0_ref, s1_ref, mt0_ref, mt1_ref, qc_ref, m_ref, l_ref, acc_ref):
    s_refs, mt_refs = (s0_ref, s1_ref), (mt0_ref, mt1_ref)
    n_cb, t, cb = s0_ref.shape
    nq = m_ref.shape[0] - 1

    for slot in range(2):
        s_refs[slot][...] = jnp.zeros(s_refs[slot].shape, F32)
        mt_refs[slot][...] = jnp.zeros(mt_refs[slot].shape, F32)
    m_ref[...] = jnp.full(m_ref.shape, MASK_VALUE, F32)
    l_ref[...] = jnp.zeros(l_ref.shape, F32)
    acc_ref[...] = jnp.zeros(acc_ref.shape, F32)

    def mask_q(qi, carry):
        q = q_ref[0, pl.ds(pl.multiple_of(qi * t, t), t), :]
        lane = lax.broadcasted_iota(jnp.int32, q.shape, 1)
        zero = jnp.zeros_like(q)
        qcat = jnp.concatenate([jnp.where(lane < HEAD_DIM, q, zero),
                                jnp.where(lane >= HEAD_DIM, q, zero)], axis=0)
        qc_ref[qi] = qcat.reshape(n_cb, cb, q.shape[1])
        return carry

    lax.fori_loop(0, nq, mask_q, 0)

    def q_tile_index(qi):
        return jnp.where(qi == nq, 0, qi)

    def tick(tab_ref, n, par, with_bias):
        col_a, col_b = n + 1, n
        cur = 1 - par

        qi_a = q_tile_index(tab_ref[_TAB_QI, col_a])
        qi_b = tab_ref[_TAB_QI, col_b]
        bias_idx = tab_ref[_TAB_BIAS, col_a]
        vt = vt_ref[0, tab_ref[_TAB_KJ, col_b]]
        k = k_ref[0, pl.ds(pl.multiple_of(tab_ref[_TAB_KJ, col_a] * t, t), t), :]

        for c in range(n_cb):
            m_old = m_ref[qi_b, c]
            m_new = jnp.maximum(m_old, mt_refs[cur][c])
            m_ref[qi_b, c] = m_new
            p = jnp.exp2(s_refs[cur][c] - m_new)
            alpha = jnp.exp2(m_old - m_new)
            l_ref[qi_b, c] = alpha * l_ref[qi_b, c] + jnp.sum(p, axis=0, keepdims=True)
            pv = jnp.dot(vt, p.astype(BF16), preferred_element_type=F32)
            acc_ref[qi_b, c] = alpha * acc_ref[qi_b, c] + pv

            st = lax.dot_general(k, qc_ref[qi_a, c], (((1,), (1,)), ((), ())),
                                 preferred_element_type=F32)
            if with_bias:
                st = st + bias_ref[0, bias_idx, :, pl.ds((c * cb) % t, cb)]
            s_refs[par][c] = st
            mt_refs[par][c] = jnp.max(st, axis=0, keepdims=True)

    def run(tab_ref, with_bias):
        n_ticks = tab_ref.shape[1] - _PIPE_DEPTH
        assert n_ticks % _TICKS_PER_ITER == 0

        def body(i, carry):
            for u in range(_TICKS_PER_ITER):
                tick(tab_ref, _TICKS_PER_ITER * i + u, u % 2, with_bias)
            return carry

        lax.fori_loop(0, n_ticks // _TICKS_PER_ITER, body, 0)

    run(far_ref, False)
    run(near_ref, True)

    lam_v = lam_ref[...]
    lam = (jnp.exp(jnp.sum(lam_v[0:1] * lam_v[1:2], axis=-1, keepdims=True))
           - jnp.exp(jnp.sum(lam_v[2:3] * lam_v[3:4], axis=-1, keepdims=True)) + LAMBDA_INIT)

    def finish(qi, carry):
        def normalised(c):
            return acc_ref[qi, c] * (1.0 / l_ref[qi, c])
        half = n_cb // 2
        o = jnp.concatenate([normalised(c) - lam * normalised(half + c) for c in range(half)],
                            axis=1).T
        o = (o * lax.rsqrt(jnp.mean(o * o, axis=-1, keepdims=True) + LN_EPS)
             * g_ref[...] * (1.0 - LAMBDA_INIT))
        o_ref[0, pl.ds(pl.multiple_of(qi * t, t), t), :] = o.astype(BF16)
        return carry

    lax.fori_loop(0, nq, finish, 0)


def _attention(q, k, vt, bias, lam_vecs, g_sub):
    bsz, s, dq = q.shape
    t = ATTN_TILE
    nq = s // t
    n_near = bias.shape[1]
    hw = 2 * HEAD_DIM
    far_tab, near_tab = _attn_schedules(nq, n_near)
    cb = ATTN_COL_BLOCK
    n_cb = 2 * t // cb
    return pl.pallas_call(
        _attn_kernel,
        grid=(bsz, N_HEADS),
        in_specs=[
            pl.BlockSpec(memory_space=pltpu.SMEM),
            pl.BlockSpec(memory_space=pltpu.SMEM),
            pl.BlockSpec((1, s, hw), lambda b, h: (b, 0, h)),
            pl.BlockSpec((1, s, hw), lambda b, h: (b, 0, h)),
            pl.BlockSpec((1, nq, V_DIM, t), lambda b, h: (b, 0, h, 0)),
            pl.BlockSpec((1, n_near, t, t), lambda b, h: (h, 0, 0, 0)),
            _const_spec((4, HEAD_DIM)),
            _const_spec((1, V_DIM)),
        ],
        out_specs=pl.BlockSpec((1, s, V_DIM), lambda b, h: (b, 0, h)),
        out_shape=jax.ShapeDtypeStruct((bsz, s, N_HEADS * V_DIM), BF16),
        scratch_shapes=[
            pltpu.VMEM((n_cb, t, cb), F32),
            pltpu.VMEM((n_cb, t, cb), F32),
            pltpu.VMEM((n_cb, 1, cb), F32),
            pltpu.VMEM((n_cb, 1, cb), F32),
            pltpu.VMEM((nq, n_cb, cb, hw), BF16),
            pltpu.VMEM((nq + 1, n_cb, 1, cb), F32),
            pltpu.VMEM((nq + 1, n_cb, 1, cb), F32),
            pltpu.VMEM((nq + 1, n_cb, V_DIM, cb), F32),
        ],
        compiler_params=pltpu.CompilerParams(
            dimension_semantics=("arbitrary", "arbitrary"),
            vmem_limit_bytes=_vmem_limit(52 * 1024 * 1024)),
        name="diff_attention",
    )(jnp.asarray(far_tab), jnp.asarray(near_tab), q, k, vt, bias, lam_vecs,
      g_sub.reshape(1, V_DIM))


def _out_proj_mlp_kernel(a_ref, x_ref, amod_ref, wo_ref, mg_ref, mb_ref,
                         mod_ref, w1_ref, w2_ref, pg_ref, pb_ref, o_ref):
    gate = amod_ref[0][2:3]
    y = jnp.dot(a_ref[0], wo_ref[...], preferred_element_type=F32)
    x = _layer_norm(ALPHA * x_ref[0] + gate * y, mg_ref[...], mb_ref[...])
    o_ref[0] = _mlp_sublayer(x, mod_ref[0], w1_ref, w2_ref, pg_ref, pb_ref)


def _out_proj_mlp(a, x, amod, wo, mg, mb, mod, w1, w2, pg, pb):
    bsz, s, d = x.shape
    da = a.shape[2]
    d_ff = w1.shape[1]
    ts = SEQ_TILE
    row = lambda v: v.reshape(1, -1)
    return pl.pallas_call(
        _out_proj_mlp_kernel,
        grid=(bsz, s // ts),
        in_specs=[
            pl.BlockSpec((1, ts, da), lambda b, i: (b, i, 0)),
            pl.BlockSpec((1, ts, d), lambda b, i: (b, i, 0)),
            pl.BlockSpec((1, 3, d), lambda b, i: (b, 0, 0)),
            _const_spec((da, d), True),
            _const_spec((1, d)),
            _const_spec((1, d)),
            pl.BlockSpec((1, 3, d), lambda b, i: (b, 0, 0)),
            _const_spec((d, d_ff), True),
            _const_spec((d_ff, d), True),
            _const_spec((1, d)),
            _const_spec((1, d)),
        ],
        out_specs=pl.BlockSpec((1, ts, d), lambda b, i: (b, i, 0)),
        out_shape=jax.ShapeDtypeStruct((bsz, s, d), F32),
        compiler_params=pltpu.CompilerParams(
            dimension_semantics=("arbitrary", "arbitrary"),
            vmem_limit_bytes=_vmem_limit(54 * 1024 * 1024)),
        name="out_proj_mlp",
    )(a, x, amod, wo.astype(BF16), row(mg), row(mb),
      mod, w1.astype(BF16), w2.astype(BF16), row(pg), row(pb))


def kernel(x, c, conv_mod_w, conv_mod_b, conv_pw1_w, conv_pw1_b, conv_dw_w, conv_dw_b, conv_norm_g, conv_norm_b, conv_pw2_w, conv_pw2_b, attn_mod_w, attn_mod_b, attn_qkv_w, attn_lam_q1, attn_lam_k1, attn_lam_q2, attn_lam_k2, attn_subln_g, attn_out_w, rel_bias, mlp_mod_w, mlp_mod_b, mlp_w1, mlp_w2, post_mix_g, post_mix_b, post_mlp_g, post_mlp_b):
    assert x.shape[1] % SEQ_TILE == 0 and x.shape[1] % ATTN_TILE == 0
    assert SEQ_TILE % CONV_ROWS == 0 and CONV_HALO >= CONV_WIDTH - 1
    conv_mod = _ada_mod(c, conv_mod_w, conv_mod_b)
    attn_mod = _ada_mod(c, attn_mod_w, attn_mod_b)
    mlp_mod = _ada_mod(c, mlp_mod_w, mlp_mod_b)

    x = _conv_mixer(x, conv_mod[0], conv_pw1_w[0], conv_pw1_b[0], conv_dw_w[0], conv_dw_b[0],
                    conv_norm_g[0], conv_norm_b[0], conv_pw2_w[0], conv_pw2_b[0],
                    post_mix_g[0], post_mix_b[0])
    x = _mlp(x, mlp_mod[0], mlp_w1[0], mlp_w2[0], post_mlp_g[0], post_mlp_b[0])

    q, k, vt = _qkv(x, attn_mod[0], attn_qkv_w[0])
    bias = _bias_tiles(rel_bias)
    lam_vecs = jnp.stack([attn_lam_q1[0], attn_lam_k1[0], attn_lam_q2[0], attn_lam_k2[0]])
    a = _attention(q, k, vt, bias, lam_vecs, attn_subln_g[0])
    return _out_proj_mlp(a, x, attn_mod[0], attn_out_w[0], post_mix_g[1], post_mix_b[1],
                         mlp_mod[1], mlp_w1[1], mlp_w2[1], post_mlp_g[1], post_mlp_b[1])
```

```python
import functools
import math

import jax
import jax.numpy as jnp
import numpy as np
from jax import lax
from jax.experimental import pallas as pl
from jax.experimental.pallas import tpu as pltpu

DEPTH = 2
CONV_WIDTH = 31
N_HEADS = 8
HEAD_DIM = 64
V_DIM = 2 * HEAD_DIM
REL_BUCKETS = 32
REL_MAX_DIST = 128
ALPHA = (2 * DEPTH) ** 0.25
LN_EPS = 1e-5
ATTN_LAYER = 1
LAMBDA_INIT = 0.8 - 0.6 * math.exp(-0.3 * ATTN_LAYER)
LOG2_E = math.log2(math.e)

V7X_SUBLANES = 8
V7X_LANES = 128
V7X_VMEM_BYTES = 64 * 1024 * 1024

MASK_VALUE = -1e30
CONV_HALO = 32
CONV_ROWS = 64
SEQ_TILE = 512
ATTN_TILE = 512
FF_CHUNK = 1024

F32 = jnp.float32
BF16 = jnp.bfloat16


def _t5_thresholds():
    max_exact = REL_BUCKETS // 2
    buckets = []
    for n in range(2 * REL_MAX_DIST):
        if n < max_exact:
            buckets.append(n)
        else:
            v = math.log(n / max_exact) / math.log(REL_MAX_DIST / max_exact) * (REL_BUCKETS - max_exact)
            buckets.append(min(max_exact + int(v), REL_BUCKETS - 1))
    assert all(b1 >= b0 for b0, b1 in zip(buckets, buckets[1:]))
    assert buckets[-1] == REL_BUCKETS - 1
    return [buckets.index(b) for b in range(REL_BUCKETS)]


T5_THRESHOLDS = _t5_thresholds()
T5_LAST_BUCKET_START = T5_THRESHOLDS[REL_BUCKETS - 1]


def _vmem_limit(nbytes):
    return int(min(nbytes, V7X_VMEM_BYTES - 8 * 1024 * 1024))


def _layer_norm(z, g, b):
    mu = jnp.mean(z, axis=-1, keepdims=True)
    zc = z - mu
    var = jnp.mean(zc * zc, axis=-1, keepdims=True)
    return zc * lax.rsqrt(var + LN_EPS) * g + b


def _const_spec(shape, single_buffer=False):
    nd = len(shape)
    kwargs = {"pipeline_mode": pl.Buffered(1)} if single_buffer else {}
    return pl.BlockSpec(shape, lambda *_: (0,) * nd, **kwargs)


def _ada_mod_kernel(c_ref, w_ref, b_ref, o_ref):
    c = c_ref[...]
    sc = c * jax.nn.sigmoid(c)
    o_ref[0] = jnp.dot(sc, w_ref[0], preferred_element_type=F32,
                       precision=lax.Precision.HIGHEST) + b_ref[0]


def _ada_mod(c, w, b):
    n, d, d3 = w.shape
    bsz = c.shape[0]
    nblk = d3 // d
    out = pl.pallas_call(
        _ada_mod_kernel,
        grid=(n, nblk),
        in_specs=[
            pl.BlockSpec((bsz, d), lambda i, j: (0, 0)),
            pl.BlockSpec((1, d, d), lambda i, j: (i, 0, j)),
            pl.BlockSpec((1, 1, d), lambda i, j: (i, 0, j)),
        ],
        out_specs=pl.BlockSpec((1, bsz, d), lambda i, j: (i, 0, j)),
        out_shape=jax.ShapeDtypeStruct((n, bsz, d3), F32),
        compiler_params=pltpu.CompilerParams(
            dimension_semantics=("arbitrary", "arbitrary"),
            vmem_limit_bytes=_vmem_limit(32 * 1024 * 1024)),
        name="ada_mod",
    )(c, w, b.reshape(n, 1, d3))
    return out.reshape(n, bsz, 3, d)


def _conv_mixer_kernel(x_ref, mod_ref, w1_ref, b1_ref, wdw_ref, bdw_ref, gcn_ref, bcn_ref,
                       w2_ref, b2_ref, pg_ref, pb_ref, o_ref, ext_ref, y_ref):
    ts, d = x_ref.shape[1], x_ref.shape[2]
    n_lb = ext_ref.shape[0]
    s_idx = pl.program_id(1)

    @pl.when(s_idx == 0)
    def _():
        ext_ref[:, 0:CONV_HALO, :] = jnp.zeros((n_lb, CONV_HALO, V7X_LANES), F32)

    mod = mod_ref[0]
    shift, scale, gate = mod[0:1], mod[1:2], mod[2:3]
    x = x_ref[0]
    h = (x * (1.0 + scale) + shift).astype(BF16)
    a = jnp.dot(h, w1_ref[...], preferred_element_type=F32) + b1_ref[...]
    u = a[:, :d] * jax.nn.sigmoid(a[:, d:])
    for lb in range(n_lb):
        ext_ref[lb, CONV_HALO:CONV_HALO + ts, :] = u[:, lb * V7X_LANES:(lb + 1) * V7X_LANES]

    off0 = CONV_HALO - (CONV_WIDTH - 1)

    n_grp = CONV_ROWS // V7X_SUBLANES

    def row_block(rb, carry):
        t0 = pl.multiple_of(rb * CONV_ROWS, CONV_ROWS)

        def lane_block(lb, carry2):
            w_all = wdw_ref[lb]
            acc = [jnp.broadcast_to(bdw_ref[lb], (V7X_SUBLANES, V7X_LANES))] * n_grp
            for r in range(V7X_SUBLANES):
                taps = [j for j in range(CONV_WIDTH) if (off0 + j) % V7X_SUBLANES == r]
                first = off0 + taps[0]
                n_load = n_grp + (taps[-1] - taps[0]) // V7X_SUBLANES
                groups = [ext_ref[lb, pl.ds(t0 + (first + V7X_SUBLANES * g), V7X_SUBLANES), :]
                          for g in range(n_load)]
                for j in taps:
                    w_row = jnp.broadcast_to(w_all[j:j + 1], (V7X_SUBLANES, V7X_LANES))
                    g0 = (off0 + j - first) // V7X_SUBLANES
                    acc = [acc[i] + w_row * groups[g0 + i] for i in range(n_grp)]
            y_ref[lb, pl.ds(t0, CONV_ROWS), :] = jnp.concatenate(acc, axis=0)
            return carry2

        lax.fori_loop(0, n_lb, lane_block, 0)
        return carry

    lax.fori_loop(0, ts // CONV_ROWS, row_block, 0)
    ext_ref[:, 0:CONV_HALO, :] = ext_ref[:, ts:ts + CONV_HALO, :]

    y = y_ref[...]
    mu = jnp.sum(jnp.sum(y, axis=0, keepdims=True), axis=2, keepdims=True) * (1.0 / d)
    yc = y - mu
    var = jnp.sum(jnp.sum(yc * yc, axis=0, keepdims=True), axis=2, keepdims=True) * (1.0 / d)
    y = yc * lax.rsqrt(var + LN_EPS) * gcn_ref[...] + bcn_ref[...]
    y = (y * jax.nn.sigmoid(y)).astype(BF16)
    v = jnp.concatenate([y[lb] for lb in range(n_lb)], axis=-1)
    y = jnp.dot(v, w2_ref[...], preferred_element_type=F32) + b2_ref[...]
    o_ref[0] = _layer_norm(ALPHA * x_ref[0] + gate * y, pg_ref[...], pb_ref[...])


def _conv_mixer(x, mod, w1, b1, wdw, bdw, gcn, bcn, w2, b2, pg, pb):
    bsz, s, d = x.shape
    ts = SEQ_TILE
    n_lb = d // V7X_LANES
    row = lambda v: v.reshape(1, -1)
    slab = lambda v: v.reshape(-1, n_lb, V7X_LANES).transpose(1, 0, 2)
    return pl.pallas_call(
        _conv_mixer_kernel,
        grid=(bsz, s // ts),
        in_specs=[
            pl.BlockSpec((1, ts, d), lambda b, i: (b, i, 0)),
            pl.BlockSpec((1, 3, d), lambda b, i: (b, 0, 0)),
            _const_spec((d, 2 * d), True),
            _const_spec((1, 2 * d)),
            _const_spec((n_lb, CONV_WIDTH, V7X_LANES)),
            _const_spec((n_lb, 1, V7X_LANES)),
            _const_spec((n_lb, 1, V7X_LANES)),
            _const_spec((n_lb, 1, V7X_LANES)),
            _const_spec((d, d), True),
            _const_spec((1, d)),
            _const_spec((1, d)),
            _const_spec((1, d)),
        ],
        out_specs=pl.BlockSpec((1, ts, d), lambda b, i: (b, i, 0)),
        out_shape=jax.ShapeDtypeStruct((bsz, s, d), F32),
        scratch_shapes=[
            pltpu.VMEM((n_lb, ts + CONV_HALO, V7X_LANES), F32),
            pltpu.VMEM((n_lb, ts, V7X_LANES), F32),
        ],
        compiler_params=pltpu.CompilerParams(
            dimension_semantics=("arbitrary", "arbitrary"),
            vmem_limit_bytes=_vmem_limit(48 * 1024 * 1024)),
        name="conv_mixer",
    )(x, mod, w1.astype(BF16), row(b1), slab(wdw), slab(bdw), slab(gcn), slab(bcn),
      w2.astype(BF16), row(b2), row(pg), row(pb))


def _mlp_sublayer(x, mod, w1_ref, w2_ref, pg_ref, pb_ref):
    d_ff = w1_ref.shape[1]
    shift, scale, gate = mod[0:1], mod[1:2], mod[2:3]
    h = (x * (1.0 + scale) + shift).astype(BF16)
    y = None
    for c0 in range(0, d_ff, FF_CHUNK):
        a = jnp.dot(h, w1_ref[:, c0:c0 + FF_CHUNK], preferred_element_type=F32)
        a = jnp.maximum(a, 0.0)
        part = jnp.dot((a * a).astype(BF16), w2_ref[c0:c0 + FF_CHUNK, :], preferred_element_type=F32)
        y = part if y is None else y + part
    return _layer_norm(ALPHA * x + gate * y, pg_ref[...], pb_ref[...])


def _mlp_kernel(x_ref, mod_ref, w1_ref, w2_ref, pg_ref, pb_ref, o_ref):
    o_ref[0] = _mlp_sublayer(x_ref[0], mod_ref[0], w1_ref, w2_ref, pg_ref, pb_ref)


def _mlp(x, mod, w1, w2, pg, pb):
    bsz, s, d = x.shape
    d_ff = w1.shape[1]
    ts = SEQ_TILE
    row = lambda v: v.reshape(1, -1)
    return pl.pallas_call(
        _mlp_kernel,
        grid=(bsz, s // ts),
        in_specs=[
            pl.BlockSpec((1, ts, d), lambda b, i: (b, i, 0)),
            pl.BlockSpec((1, 3, d), lambda b, i: (b, 0, 0)),
            _const_spec((d, d_ff), True),
            _const_spec((d_ff, d), True),
            _const_spec((1, d)),
            _const_spec((1, d)),
        ],
        out_specs=pl.BlockSpec((1, ts, d), lambda b, i: (b, i, 0)),
        out_shape=jax.ShapeDtypeStruct((bsz, s, d), F32),
        compiler_params=pltpu.CompilerParams(
            dimension_semantics=("arbitrary", "arbitrary"),
            vmem_limit_bytes=_vmem_limit(52 * 1024 * 1024)),
        name="sq_relu_mlp",
    )(x, mod, w1.astype(BF16), w2.astype(BF16), row(pg), row(pb))


def _qkv_kernel(x_ref, mod_ref, wq_ref, wk_ref, wvt_ref, q_ref, k_ref, vt_ref):
    mod = mod_ref[0]
    shift, scale = mod[0:1], mod[1:2]
    h = (x_ref[0] * (1.0 + scale) + shift).astype(BF16)
    q = jnp.dot(h, wq_ref[...], preferred_element_type=F32)
    q_ref[0] = (q * (HEAD_DIM ** -0.5 * LOG2_E)).astype(BF16)
    k_ref[0] = jnp.dot(h, wk_ref[...], preferred_element_type=F32).astype(BF16)
    vt = lax.dot_general(wvt_ref[...], h, (((1,), (1,)), ((), ())), preferred_element_type=F32)
    vt_ref[0, 0] = vt.astype(BF16)


def _qkv(x, mod, w_qkv):
    bsz, s, d = x.shape
    tk = ATTN_TILE
    dq = N_HEADS * 2 * HEAD_DIM
    dv = N_HEADS * V_DIM
    wq = w_qkv[:, :dq].astype(BF16)
    wk = w_qkv[:, dq:2 * dq].astype(BF16)
    wvt = w_qkv[:, 2 * dq:].T.astype(BF16)
    return pl.pallas_call(
        _qkv_kernel,
        grid=(bsz, s // tk),
        in_specs=[
            pl.BlockSpec((1, tk, d), lambda b, i: (b, i, 0)),
            pl.BlockSpec((1, 3, d), lambda b, i: (b, 0, 0)),
            _const_spec((d, dq), True),
            _const_spec((d, dq), True),
            _const_spec((dv, d), True),
        ],
        out_specs=[
            pl.BlockSpec((1, tk, dq), lambda b, i: (b, i, 0)),
            pl.BlockSpec((1, tk, dq), lambda b, i: (b, i, 0)),
            pl.BlockSpec((1, 1, dv, tk), lambda b, i: (b, i, 0, 0)),
        ],
        out_shape=[
            jax.ShapeDtypeStruct((bsz, s, dq), BF16),
            jax.ShapeDtypeStruct((bsz, s, dq), BF16),
            jax.ShapeDtypeStruct((bsz, s // tk, dv, tk), BF16),
        ],
        compiler_params=pltpu.CompilerParams(
            dimension_semantics=("arbitrary", "arbitrary"),
            vmem_limit_bytes=_vmem_limit(40 * 1024 * 1024)),
        name="qkv_proj",
    )(x, mod, wq, wk, wvt)


def _bias_tiles_kernel(rb_ref, o_ref):
    h = pl.program_id(0)
    t = o_ref.shape[2]
    far = rb_ref[h, REL_BUCKETS - 1]
    kk = lax.broadcasted_iota(jnp.int32, (t, t), 0)
    qq = lax.broadcasted_iota(jnp.int32, (t, t), 1)
    for dlt in range(o_ref.shape[1]):
        rel = dlt * t + qq - kk
        val = jnp.full((t, t), (rb_ref[h, 0] - far) * LOG2_E, F32)
        for bkt in range(1, REL_BUCKETS):
            val = jnp.where(rel >= T5_THRESHOLDS[bkt], (rb_ref[h, bkt] - far) * LOG2_E, val)
        o_ref[0, dlt] = jnp.where(rel >= 0, val, MASK_VALUE)


def _bias_tiles(rel_bias):
    t = ATTN_TILE
    n_near = -(-(T5_LAST_BUCKET_START + t - 1) // t)
    return pl.pallas_call(
        _bias_tiles_kernel,
        grid=(N_HEADS,),
        in_specs=[pl.BlockSpec(memory_space=pltpu.SMEM)],
        out_specs=pl.BlockSpec((1, n_near, t, t), lambda h: (h, 0, 0, 0)),
        out_shape=jax.ShapeDtypeStruct((N_HEADS, n_near, t, t), F32),
        compiler_params=pltpu.CompilerParams(dimension_semantics=("arbitrary",)),
        name="t5_bias_tiles",
    )(rel_bias.T)


_TAB_QI, _TAB_KJ, _TAB_BIAS = range(3)
_PIPE_DEPTH = 1
_TICKS_PER_ITER = 16
ATTN_COL_BLOCK = 256


def _attn_schedule(pairs, inert):
    cols = [inert] * _PIPE_DEPTH + list(pairs) + [inert] * _PIPE_DEPTH
    n_ticks = len(cols) - _PIPE_DEPTH
    cols += [inert] * (-n_ticks % _TICKS_PER_ITER)
    return np.asarray(cols, np.int32).T


def _attn_schedules(nq, n_near):
    far = [(qi, kj, 0) for qi in range(nq) for kj in range(qi - n_near + 1)]
    near = [(qi, qi - d, d) for qi in range(nq) for d in reversed(range(min(n_near, qi + 1)))]
    return _attn_schedule(far, (nq, 0, 0)), _attn_schedule(near, (nq, 0, 0))


def _attn_kernel(far_ref, near_ref, q_ref, k_ref, vt_ref, bias_ref, lam_ref, g_ref, o_ref,
                 s0_ref, s1_ref, mt0_ref, mt1_ref, qc_ref, m_ref, l_ref, acc_ref):
    s_refs, mt_refs = (s0_ref, s1_ref), (mt0_ref, mt1_ref)
    n_cb, t, cb = s0_ref.shape
    nq = m_ref.shape[0] - 1

    for slot in range(2):
        s_refs[slot][...] = jnp.zeros(s_refs[slot].shape, F32)
        mt_refs[slot][...] = jnp.zeros(mt_refs[slot].shape, F32)
    m_ref[...] = jnp.full(m_ref.shape, MASK_VALUE, F32)
    l_ref[...] = jnp.zeros(l_ref.shape, F32)
    acc_ref[...] = jnp.zeros(acc_ref.shape, F32)

    def mask_q(qi, carry):
        q = q_ref[0, pl.ds(pl.multiple_of(qi * t, t), t), :]
        lane = lax.broadcasted_iota(jnp.int32, q.shape, 1)
        zero = jnp.zeros_like(q)
        qcat = jnp.concatenate([jnp.where(lane < HEAD_DIM, q, zero),
                                jnp.where(lane >= HEAD_DIM, q, zero)], axis=0)
        qc_ref[qi] = qcat.reshape(n_cb, cb, q.shape[1])
        return carry

    lax.fori_loop(0, nq, mask_q, 0)

    def q_tile_index(qi):
        return jnp.where(qi == nq, 0, qi)

    def tick(tab_ref, n, par, with_bias):
        col_a, col_b = n + 1, n
        cur = 1 - par

        qi_a = q_tile_index(tab_ref[_TAB_QI, col_a])
        qi_b = tab_ref[_TAB_QI, col_b]
        bias_idx = tab_ref[_TAB_BIAS, col_a]
        vt = vt_ref[0, tab_ref[_TAB_KJ, col_b]]
        k = k_ref[0, pl.ds(pl.multiple_of(tab_ref[_TAB_KJ, col_a] * t, t), t), :]

        for c in range(n_cb):
            st = lax.dot_general(k, qc_ref[qi_a, c], (((1,), (1,)), ((), ())),
                                 preferred_element_type=F32)
            if with_bias:
                st = st + bias_ref[0, bias_idx, :, pl.ds((c * cb) % t, cb)]
            s_refs[par][c] = st
            mt_refs[par][c] = jnp.max(st, axis=0, keepdims=True)

            m_old = m_ref[qi_b, c]
            m_new = jnp.maximum(m_old, mt_refs[cur][c])
            m_ref[qi_b, c] = m_new
            p = jnp.exp2(s_refs[cur][c] - m_new)
            alpha = jnp.exp2(m_old - m_new)
            l_ref[qi_b, c] = alpha * l_ref[qi_b, c] + jnp.sum(p, axis=0, keepdims=True)
            pv = jnp.dot(vt, p.astype(BF16), preferred_element_type=F32)
            acc_ref[qi_b, c] = alpha * acc_ref[qi_b, c] + pv

    def run(tab_ref, with_bias):
        n_ticks = tab_ref.shape[1] - _PIPE_DEPTH
        assert n_ticks % _TICKS_PER_ITER == 0

        def body(i, carry):
            for u in range(_TICKS_PER_ITER):
                tick(tab_ref, _TICKS_PER_ITER * i + u, u % 2, with_bias)
            return carry

        lax.fori_loop(0, n_ticks // _TICKS_PER_ITER, body, 0)

    run(far_ref, False)
    run(near_ref, True)

    lam_v = lam_ref[...]
    lam = (jnp.exp(jnp.sum(lam_v[0:1] * lam_v[1:2], axis=-1, keepdims=True))
           - jnp.exp(jnp.sum(lam_v[2:3] * lam_v[3:4], axis=-1, keepdims=True)) + LAMBDA_INIT)

    def finish(qi, carry):
        def normalised(c):
            return acc_ref[qi, c] * (1.0 / l_ref[qi, c])
        half = n_cb // 2
        o = jnp.concatenate([normalised(c) - lam * normalised(half + c) for c in range(half)],
                            axis=1).T
        o = (o * lax.rsqrt(jnp.mean(o * o, axis=-1, keepdims=True) + LN_EPS)
             * g_ref[...] * (1.0 - LAMBDA_INIT))
        o_ref[0, pl.ds(pl.multiple_of(qi * t, t), t), :] = o.astype(BF16)
        return carry

    lax.fori_loop(0, nq, finish, 0)


def _attention(q, k, vt, bias, lam_vecs, g_sub):
    bsz, s, dq = q.shape
    t = ATTN_TILE
    nq = s // t
    n_near = bias.shape[1]
    hw = 2 * HEAD_DIM
    far_tab, near_tab = _attn_schedules(nq, n_near)
    cb = ATTN_COL_BLOCK
    n_cb = 2 * t // cb
    return pl.pallas_call(
        _attn_kernel,
        grid=(bsz, N_HEADS),
        in_specs=[
            pl.BlockSpec(memory_space=pltpu.SMEM),
            pl.BlockSpec(memory_space=pltpu.SMEM),
            pl.BlockSpec((1, s, hw), lambda b, h: (b, 0, h)),
            pl.BlockSpec((1, s, hw), lambda b, h: (b, 0, h)),
            pl.BlockSpec((1, nq, V_DIM, t), lambda b, h: (b, 0, h, 0)),
            pl.BlockSpec((1, n_near, t, t), lambda b, h: (h, 0, 0, 0)),
            _const_spec((4, HEAD_DIM)),
            _const_spec((1, V_DIM)),
        ],
        out_specs=pl.BlockSpec((1, s, V_DIM), lambda b, h: (b, 0, h)),
        out_shape=jax.ShapeDtypeStruct((bsz, s, N_HEADS * V_DIM), BF16),
        scratch_shapes=[
            pltpu.VMEM((n_cb, t, cb), F32),
            pltpu.VMEM((n_cb, t, cb), F32),
            pltpu.VMEM((n_cb, 1, cb), F32),
            pltpu.VMEM((n_cb, 1, cb), F32),
            pltpu.VMEM((nq, n_cb, cb, hw), BF16),
            pltpu.VMEM((nq + 1, n_cb, 1, cb), F32),
            pltpu.VMEM((nq + 1, n_cb, 1, cb), F32),
            pltpu.VMEM((nq + 1, n_cb, V_DIM, cb), F32),
        ],
        compiler_params=pltpu.CompilerParams(
            dimension_semantics=("arbitrary", "arbitrary"),
            vmem_limit_bytes=_vmem_limit(52 * 1024 * 1024)),
        name="diff_attention",
    )(jnp.asarray(far_tab), jnp.asarray(near_tab), q, k, vt, bias, lam_vecs,
      g_sub.reshape(1, V_DIM))


def _out_proj_mlp_kernel(a_ref, x_ref, amod_ref, wo_ref, mg_ref, mb_ref,
                         mod_ref, w1_ref, w2_ref, pg_ref, pb_ref, o_ref):
    gate = amod_ref[0][2:3]
    y = jnp.dot(a_ref[0], wo_ref[...], preferred_element_type=F32)
    x = _layer_norm(ALPHA * x_ref[0] + gate * y, mg_ref[...], mb_ref[...])
    o_ref[0] = _mlp_sublayer(x, mod_ref[0], w1_ref, w2_ref, pg_ref, pb_ref)


def _out_proj_mlp(a, x, amod, wo, mg, mb, mod, w1, w2, pg, pb):
    bsz, s, d = x.shape
    da = a.shape[2]
    d_ff = w1.shape[1]
    ts = SEQ_TILE
    row = lambda v: v.reshape(1, -1)
    return pl.pallas_call(
        _out_proj_mlp_kernel,
        grid=(bsz, s // ts),
        in_specs=[
            pl.BlockSpec((1, ts, da), lambda b, i: (b, i, 0)),
            pl.BlockSpec((1, ts, d), lambda b, i: (b, i, 0)),
            pl.BlockSpec((1, 3, d), lambda b, i: (b, 0, 0)),
            _const_spec((da, d), True),
            _const_spec((1, d)),
            _const_spec((1, d)),
            pl.BlockSpec((1, 3, d), lambda b, i: (b, 0, 0)),
            _const_spec((d, d_ff), True),
            _const_spec((d_ff, d), True),
            _const_spec((1, d)),
            _const_spec((1, d)),
        ],
        out_specs=pl.BlockSpec((1, ts, d), lambda b, i: (b, i, 0)),
        out_shape=jax.ShapeDtypeStruct((bsz, s, d), F32),
        compiler_params=pltpu.CompilerParams(
            dimension_semantics=("arbitrary", "arbitrary"),
            vmem_limit_bytes=_vmem_limit(54 * 1024 * 1024)),
        name="out_proj_mlp",
    )(a, x, amod, wo.astype(BF16), row(mg), row(mb),
      mod, w1.astype(BF16), w2.astype(BF16), row(pg), row(pb))


def kernel(x, c, conv_mod_w, conv_mod_b, conv_pw1_w, conv_pw1_b, conv_dw_w, conv_dw_b, conv_norm_g, conv_norm_b, conv_pw2_w, conv_pw2_b, attn_mod_w, attn_mod_b, attn_qkv_w, attn_lam_q1, attn_lam_k1, attn_lam_q2, attn_lam_k2, attn_subln_g, attn_out_w, rel_bias, mlp_mod_w, mlp_mod_b, mlp_w1, mlp_w2, post_mix_g, post_mix_b, post_mlp_g, post_mlp_b):
    assert x.shape[1] % SEQ_TILE == 0 and x.shape[1] % ATTN_TILE == 0
    assert SEQ_TILE % CONV_ROWS == 0 and CONV_HALO >= CONV_WIDTH - 1
    conv_mod = _ada_mod(c, conv_mod_w, conv_mod_b)
    attn_mod = _ada_mod(c, attn_mod_w, attn_mod_b)
    mlp_mod = _ada_mod(c, mlp_mod_w, mlp_mod_b)

    x = _conv_mixer(x, conv_mod[0], conv_pw1_w[0], conv_pw1_b[0], conv_dw_w[0], conv_dw_b[0],
                    conv_norm_g[0], conv_norm_b[0], conv_pw2_w[0], conv_pw2_b[0],
                    post_mix_g[0], post_mix_b[0])
    x = _mlp(x, mlp_mod[0], mlp_w1[0], mlp_w2[0], post_mlp_g[0], post_mlp_b[0])

    q, k, vt = _qkv(x, attn_mod[0], attn_qkv_w[0])
    bias = _bias_tiles(rel_bias)
    lam_vecs = jnp.stack([attn_lam_q1[0], attn_lam_k1[0], attn_lam_q2[0], attn_lam_k2[0]])
    a = _attention(q, k, vt, bias, lam_vecs, attn_subln_g[0])
    return _out_proj_mlp(a, x, attn_mod[0], attn_out_w[0], post_mix_g[1], post_mix_b[1],
                         mlp_mod[1], mlp_w1[1], mlp_w2[1], post_mlp_g[1], post_mlp_b[1])
```

```python
import functools
import math

import jax
import jax.numpy as jnp
import numpy as np
from jax import lax
from jax.experimental import pallas as pl
from jax.experimental.pallas import tpu as pltpu

DEPTH = 2
CONV_WIDTH = 31
N_HEADS = 8
HEAD_DIM = 64
V_DIM = 2 * HEAD_DIM
REL_BUCKETS = 32
REL_MAX_DIST = 128
ALPHA = (2 * DEPTH) ** 0.25
LN_EPS = 1e-5
ATTN_LAYER = 1
LAMBDA_INIT = 0.8 - 0.6 * math.exp(-0.3 * ATTN_LAYER)
LOG2_E = math.log2(math.e)

V7X_SUBLANES = 8
V7X_LANES = 128
V7X_VMEM_BYTES = 64 * 1024 * 1024

MASK_VALUE = -1e30
CONV_HALO = 32
CONV_ROWS = 64
SEQ_TILE = 512
ATTN_TILE = 512
FF_CHUNK = 1024

F32 = jnp.float32
BF16 = jnp.bfloat16


def _t5_thresholds():
    max_exact = REL_BUCKETS // 2
    buckets = []
    for n in range(2 * REL_MAX_DIST):
        if n < max_exact:
            buckets.append(n)
        else:
            v = math.log(n / max_exact) / math.log(REL_MAX_DIST / max_exact) * (REL_BUCKETS - max_exact)
            buckets.append(min(max_exact + int(v), REL_BUCKETS - 1))
    assert all(b1 >= b0 for b0, b1 in zip(buckets, buckets[1:]))
    assert buckets[-1] == REL_BUCKETS - 1
    return [buckets.index(b) for b in range(REL_BUCKETS)]


T5_THRESHOLDS = _t5_thresholds()
T5_LAST_BUCKET_START = T5_THRESHOLDS[REL_BUCKETS - 1]


def _vmem_limit(nbytes):
    return int(min(nbytes, V7X_VMEM_BYTES - 8 * 1024 * 1024))


def _layer_norm(z, g, b):
    mu = jnp.mean(z, axis=-1, keepdims=True)
    zc = z - mu
    var = jnp.mean(zc * zc, axis=-1, keepdims=True)
    return zc * lax.rsqrt(var + LN_EPS) * g + b


def _const_spec(shape, single_buffer=False):
    nd = len(shape)
    kwargs = {"pipeline_mode": pl.Buffered(1)} if single_buffer else {}
    return pl.BlockSpec(shape, lambda *_: (0,) * nd, **kwargs)


def _ada_mod_kernel(c_ref, w_ref, b_ref, o_ref):
    c = c_ref[...]
    sc = c * jax.nn.sigmoid(c)
    o_ref[0] = jnp.dot(sc, w_ref[0], preferred_element_type=F32,
                       precision=lax.Precision.HIGHEST) + b_ref[0]


def _ada_mod(c, w, b):
    n, d, d3 = w.shape
    bsz = c.shape[0]
    nblk = d3 // d
    out = pl.pallas_call(
        _ada_mod_kernel,
        grid=(n, nblk),
        in_specs=[
            pl.BlockSpec((bsz, d), lambda i, j: (0, 0)),
            pl.BlockSpec((1, d, d), lambda i, j: (i, 0, j)),
            pl.BlockSpec((1, 1, d), lambda i, j: (i, 0, j)),
        ],
        out_specs=pl.BlockSpec((1, bsz, d), lambda i, j: (i, 0, j)),
        out_shape=jax.ShapeDtypeStruct((n, bsz, d3), F32),
        compiler_params=pltpu.CompilerParams(
            dimension_semantics=("arbitrary", "arbitrary"),
            vmem_limit_bytes=_vmem_limit(32 * 1024 * 1024)),
        name="ada_mod",
    )(c, w, b.reshape(n, 1, d3))
    return out.reshape(n, bsz, 3, d)


def _conv_mixer_kernel(x_ref, mod_ref, w1_ref, b1_ref, wdw_ref, bdw_ref, gcn_ref, bcn_ref,
                       w2_ref, b2_ref, pg_ref, pb_ref, o_ref, ext_ref, y_ref):
    ts, d = x_ref.shape[1], x_ref.shape[2]
    n_lb = ext_ref.shape[0]
    s_idx = pl.program_id(1)

    @pl.when(s_idx == 0)
    def _():
        ext_ref[:, 0:CONV_HALO, :] = jnp.zeros((n_lb, CONV_HALO, V7X_LANES), F32)

    mod = mod_ref[0]
    shift, scale, gate = mod[0:1], mod[1:2], mod[2:3]
    x = x_ref[0]
    h = (x * (1.0 + scale) + shift).astype(BF16)
    a = jnp.dot(h, w1_ref[...], preferred_element_type=F32) + b1_ref[...]
    u = a[:, :d] * jax.nn.sigmoid(a[:, d:])
    for lb in range(n_lb):
        ext_ref[lb, CONV_HALO:CONV_HALO + ts, :] = u[:, lb * V7X_LANES:(lb + 1) * V7X_LANES]

    off0 = CONV_HALO - (CONV_WIDTH - 1)

    n_grp = CONV_ROWS // V7X_SUBLANES

    def row_block(rb, carry):
        t0 = pl.multiple_of(rb * CONV_ROWS, CONV_ROWS)

        def lane_block(lb, carry2):
            w_all = wdw_ref[lb]
            acc = [jnp.broadcast_to(bdw_ref[lb], (V7X_SUBLANES, V7X_LANES))] * n_grp
            for r in range(V7X_SUBLANES):
                taps = [j for j in range(CONV_WIDTH) if (off0 + j) % V7X_SUBLANES == r]
                first = off0 + taps[0]
                n_load = n_grp + (taps[-1] - taps[0]) // V7X_SUBLANES
                groups = [ext_ref[lb, pl.ds(t0 + (first + V7X_SUBLANES * g), V7X_SUBLANES), :]
                          for g in range(n_load)]
                for j in taps:
                    w_row = jnp.broadcast_to(w_all[j:j + 1], (V7X_SUBLANES, V7X_LANES))
                    g0 = (off0 + j - first) // V7X_SUBLANES
                    acc = [acc[i] + w_row * groups[g0 + i] for i in range(n_grp)]
            y_ref[lb, pl.ds(t0, CONV_ROWS), :] = jnp.concatenate(acc, axis=0)
            return carry2

        lax.fori_loop(0, n_lb, lane_block, 0)
        return carry

    lax.fori_loop(0, ts // CONV_ROWS, row_block, 0)
    ext_ref[:, 0:CONV_HALO, :] = ext_ref[:, ts:ts + CONV_HALO, :]

    y = y_ref[...]
    mu = jnp.sum(jnp.sum(y, axis=0, keepdims=True), axis=2, keepdims=True) * (1.0 / d)
    yc = y - mu
    var = jnp.sum(jnp.sum(yc * yc, axis=0, keepdims=True), axis=2, keepdims=True) * (1.0 / d)
    y = yc * lax.rsqrt(var + LN_EPS) * gcn_ref[...] + bcn_ref[...]
    y = (y * jax.nn.sigmoid(y)).astype(BF16)
    v = jnp.concatenate([y[lb] for lb in range(n_lb)], axis=-1)
    y = jnp.dot(v, w2_ref[...], preferred_element_type=F32) + b2_ref[...]
    o_ref[0] = _layer_norm(ALPHA * x_ref[0] + gate * y, pg_ref[...], pb_ref[...])


def _conv_mixer(x, mod, w1, b1, wdw, bdw, gcn, bcn, w2, b2, pg, pb):
    bsz, s, d = x.shape
    ts = SEQ_TILE
    n_lb = d // V7X_LANES
    row = lambda v: v.reshape(1, -1)
    slab = lambda v: v.reshape(-1, n_lb, V7X_LANES).transpose(1, 0, 2)
    return pl.pallas_call(
        _conv_mixer_kernel,
        grid=(bsz, s // ts),
        in_specs=[
            pl.BlockSpec((1, ts, d), lambda b, i: (b, i, 0)),
            pl.BlockSpec((1, 3, d), lambda b, i: (b, 0, 0)),
            _const_spec((d, 2 * d), True),
            _const_spec((1, 2 * d)),
            _const_spec((n_lb, CONV_WIDTH, V7X_LANES)),
            _const_spec((n_lb, 1, V7X_LANES)),
            _const_spec((n_lb, 1, V7X_LANES)),
            _const_spec((n_lb, 1, V7X_LANES)),
            _const_spec((d, d), True),
            _const_spec((1, d)),
            _const_spec((1, d)),
            _const_spec((1, d)),
        ],
        out_specs=pl.BlockSpec((1, ts, d), lambda b, i: (b, i, 0)),
        out_shape=jax.ShapeDtypeStruct((bsz, s, d), F32),
        scratch_shapes=[
            pltpu.VMEM((n_lb, ts + CONV_HALO, V7X_LANES), F32),
            pltpu.VMEM((n_lb, ts, V7X_LANES), F32),
        ],
        compiler_params=pltpu.CompilerParams(
            dimension_semantics=("arbitrary", "arbitrary"),
            vmem_limit_bytes=_vmem_limit(48 * 1024 * 1024)),
        name="conv_mixer",
    )(x, mod, w1.astype(BF16), row(b1), slab(wdw), slab(bdw), slab(gcn), slab(bcn),
      w2.astype(BF16), row(b2), row(pg), row(pb))


def _mlp_sublayer(x, mod, w1_ref, w2_ref, pg_ref, pb_ref):
    d_ff = w1_ref.shape[1]
    shift, scale, gate = mod[0:1], mod[1:2], mod[2:3]
    h = (x * (1.0 + scale) + shift).astype(BF16)
    y = None
    for c0 in range(0, d_ff, FF_CHUNK):
        a = jnp.dot(h, w1_ref[:, c0:c0 + FF_CHUNK], preferred_element_type=F32)
        a = jnp.maximum(a, 0.0)
        part = jnp.dot((a * a).astype(BF16), w2_ref[c0:c0 + FF_CHUNK, :], preferred_element_type=F32)
        y = part if y is None else y + part
    return _layer_norm(ALPHA * x + gate * y, pg_ref[...], pb_ref[...])


def _mlp_kernel(x_ref, mod_ref, w1_ref, w2_ref, pg_ref, pb_ref, o_ref):
    o_ref[0] = _mlp_sublayer(x_ref[0], mod_ref[0], w1_ref, w2_ref, pg_ref, pb_ref)


def _mlp(x, mod, w1, w2, pg, pb):
    bsz, s, d = x.shape
    d_ff = w1.shape[1]
    ts = SEQ_TILE
    row = lambda v: v.reshape(1, -1)
    return pl.pallas_call(
        _mlp_kernel,
        grid=(bsz, s // ts),
        in_specs=[
            pl.BlockSpec((1, ts, d), lambda b, i: (b, i, 0)),
            pl.BlockSpec((1, 3, d), lambda b, i: (b, 0, 0)),
            _const_spec((d, d_ff), True),
            _const_spec((d_ff, d), True),
            _const_spec((1, d)),
            _const_spec((1, d)),
        ],
        out_specs=pl.BlockSpec((1, ts, d), lambda b, i: (b, i, 0)),
        out_shape=jax.ShapeDtypeStruct((bsz, s, d), F32),
        compiler_params=pltpu.CompilerParams(
            dimension_semantics=("arbitrary", "arbitrary"),
            vmem_limit_bytes=_vmem_limit(52 * 1024 * 1024)),
        name="sq_relu_mlp",
    )(x, mod, w1.astype(BF16), w2.astype(BF16), row(pg), row(pb))


def _qkv_kernel(x_ref, mod_ref, wq_ref, wk_ref, wvt_ref, q_ref, k_ref, vt_ref):
    mod = mod_ref[0]
    shift, scale = mod[0:1], mod[1:2]
    h = (x_ref[0] * (1.0 + scale) + shift).astype(BF16)
    q = jnp.dot(h, wq_ref[...], preferred_element_type=F32)
    q_ref[0] = (q * (HEAD_DIM ** -0.5 * LOG2_E)).astype(BF16)
    k_ref[0] = jnp.dot(h, wk_ref[...], preferred_element_type=F32).astype(BF16)
    vt = lax.dot_general(wvt_ref[...], h, (((1,), (1,)), ((), ())), preferred_element_type=F32)
    vt_ref[0, 0] = vt.astype(BF16)


def _qkv(x, mod, w_qkv):
    bsz, s, d = x.shape
    tk = ATTN_TILE
    dq = N_HEADS * 2 * HEAD_DIM
    dv = N_HEADS * V_DIM
    wq = w_qkv[:, :dq].astype(BF16)
    wk = w_qkv[:, dq:2 * dq].astype(BF16)
    wvt = w_qkv[:, 2 * dq:].T.astype(BF16)
    return pl.pallas_call(
        _qkv_kernel,
        grid=(bsz, s // tk),
        in_specs=[
            pl.BlockSpec((1, tk, d), lambda b, i: (b, i, 0)),
            pl.BlockSpec((1, 3, d), lambda b, i: (b, 0, 0)),
            _const_spec((d, dq), True),
            _const_spec((d, dq), True),
            _const_spec((dv, d), True),
        ],
        out_specs=[
            pl.BlockSpec((1, tk, dq), lambda b, i: (b, i, 0)),
            pl.BlockSpec((1, tk, dq), lambda b, i: (b, i, 0)),
            pl.BlockSpec((1, 1, dv, tk), lambda b, i: (b, i, 0, 0)),
        ],
        out_shape=[
            jax.ShapeDtypeStruct((bsz, s, dq), BF16),
            jax.ShapeDtypeStruct((bsz, s, dq), BF16),
            jax.ShapeDtypeStruct((bsz, s // tk, dv, tk), BF16),
        ],
        compiler_params=pltpu.CompilerParams(
            dimension_semantics=("arbitrary", "arbitrary"),
            vmem_limit_bytes=_vmem_limit(40 * 1024 * 1024)),
        name="qkv_proj",
    )(x, mod, wq, wk, wvt)


def _bias_tiles_kernel(rb_ref, o_ref):
    h = pl.program_id(0)
    t = o_ref.shape[2]
    far = rb_ref[h, REL_BUCKETS - 1]
    kk = lax.broadcasted_iota(jnp.int32, (t, t), 0)
    qq = lax.broadcasted_iota(jnp.int32, (t, t), 1)
    for dlt in range(o_ref.shape[1]):
        rel = dlt * t + qq - kk
        val = jnp.full((t, t), (rb_ref[h, 0] - far) * LOG2_E, F32)
        for bkt in range(1, REL_BUCKETS):
            val = jnp.where(rel >= T5_THRESHOLDS[bkt], (rb_ref[h, bkt] - far) * LOG2_E, val)
        o_ref[0, dlt] = jnp.where(rel >= 0, val, MASK_VALUE)


def _bias_tiles(rel_bias):
    t = ATTN_TILE
    n_near = -(-(T5_LAST_BUCKET_START + t - 1) // t)
    return pl.pallas_call(
        _bias_tiles_kernel,
        grid=(N_HEADS,),
        in_specs=[pl.BlockSpec(memory_space=pltpu.SMEM)],
        out_specs=pl.BlockSpec((1, n_near, t, t), lambda h: (h, 0, 0, 0)),
        out_shape=jax.ShapeDtypeStruct((N_HEADS, n_near, t, t), F32),
        compiler_params=pltpu.CompilerParams(dimension_semantics=("arbitrary",)),
        name="t5_bias_tiles",
    )(rel_bias.T)


_TAB_QI, _TAB_KJ, _TAB_BIAS = range(3)
_PIPE_DEPTH = 1
_UNROLL_CHOICES = (20, 18, 16, 14, 12)
ATTN_COL_BLOCK = 256


def _attn_schedule(pairs, inert):
    cols = [inert] * _PIPE_DEPTH + list(pairs) + [inert] * _PIPE_DEPTH
    n_ticks = len(cols) - _PIPE_DEPTH
    unroll = min(_UNROLL_CHOICES, key=lambda u: (-n_ticks % u, -u))
    cols += [inert] * (-n_ticks % unroll)
    return np.asarray(cols, np.int32).T, unroll


def _attn_schedules(nq, n_near):
    far = [(qi, kj, 0) for qi in range(nq) for kj in range(qi - n_near + 1)]
    near = [(qi, qi - d, d) for qi in range(nq) for d in reversed(range(min(n_near, qi + 1)))]
    return _attn_schedule(far, (nq, 0, 0)), _attn_schedule(near, (nq, 0, 0))


def _attn_kernel(far_ref, near_ref, q_ref, k_ref, vt_ref, bias_ref, lam_ref, g_ref, o_ref,
                 s0_ref, s1_ref, mt0_ref, mt1_ref, qc_ref, m_ref, l_ref, acc_ref,
                 *, far_unroll, near_unroll):
    s_refs, mt_refs = (s0_ref, s1_ref), (mt0_ref, mt1_ref)
    n_cb, t, cb = s0_ref.shape
    nq = m_ref.shape[0] - 1

    for slot in range(2):
        s_refs[slot][...] = jnp.zeros(s_refs[slot].shape, F32)
        mt_refs[slot][...] = jnp.zeros(mt_refs[slot].shape, F32)
    m_ref[...] = jnp.full(m_ref.shape, MASK_VALUE, F32)
    l_ref[...] = jnp.zeros(l_ref.shape, F32)
    acc_ref[...] = jnp.zeros(acc_ref.shape, F32)

    def mask_q(qi, carry):
        q = q_ref[0, pl.ds(pl.multiple_of(qi * t, t), t), :]
        lane = lax.broadcasted_iota(jnp.int32, q.shape, 1)
        zero = jnp.zeros_like(q)
        qcat = jnp.concatenate([jnp.where(lane < HEAD_DIM, q, zero),
                                jnp.where(lane >= HEAD_DIM, q, zero)], axis=0)
        qc_ref[qi] = qcat.reshape(n_cb, cb, q.shape[1])
        return carry

    lax.fori_loop(0, nq, mask_q, 0)

    def q_tile_index(qi):
        return jnp.where(qi == nq, 0, qi)

    def tick(tab_ref, n, par, with_bias):
        col_a, col_b = n + 1, n
        cur = 1 - par

        qi_a = q_tile_index(tab_ref[_TAB_QI, col_a])
        qi_b = tab_ref[_TAB_QI, col_b]
        bias_idx = tab_ref[_TAB_BIAS, col_a]
        vt = vt_ref[0, tab_ref[_TAB_KJ, col_b]]
        k = k_ref[0, pl.ds(pl.multiple_of(tab_ref[_TAB_KJ, col_a] * t, t), t), :]

        for c in range(n_cb):
            st = lax.dot_general(k, qc_ref[qi_a, c], (((1,), (1,)), ((), ())),
                                 preferred_element_type=F32)
            if with_bias:
                st = st + bias_ref[0, bias_idx, :, pl.ds((c * cb) % t, cb)]
            s_refs[par][c] = st
            mt_refs[par][c] = jnp.max(st, axis=0, keepdims=True)

            m_old = m_ref[qi_b, c]
            m_new = jnp.maximum(m_old, mt_refs[cur][c])
            m_ref[qi_b, c] = m_new
            p = jnp.exp2(s_refs[cur][c] - m_new)
            pv = jnp.dot(vt, p.astype(BF16), preferred_element_type=F32)
            alpha = jnp.exp2(m_old - m_new)
            acc_ref[qi_b, c] = alpha * acc_ref[qi_b, c] + pv
            l_ref[qi_b, c] = alpha * l_ref[qi_b, c] + jnp.sum(p, axis=0, keepdims=True)

    def run(tab_ref, unroll, with_bias):
        n_ticks = tab_ref.shape[1] - _PIPE_DEPTH
        assert n_ticks % unroll == 0 and unroll % 2 == 0

        def body(i, carry):
            for u in range(unroll):
                tick(tab_ref, unroll * i + u, u % 2, with_bias)
            return carry

        lax.fori_loop(0, n_ticks // unroll, body, 0)

    run(far_ref, far_unroll, False)
    run(near_ref, near_unroll, True)

    lam_v = lam_ref[...]
    lam = (jnp.exp(jnp.sum(lam_v[0:1] * lam_v[1:2], axis=-1, keepdims=True))
           - jnp.exp(jnp.sum(lam_v[2:3] * lam_v[3:4], axis=-1, keepdims=True)) + LAMBDA_INIT)

    def finish(qi, carry):
        def normalised(c):
            return acc_ref[qi, c] * (1.0 / l_ref[qi, c])
        half = n_cb // 2
        o = jnp.concatenate([normalised(c) - lam * normalised(half + c) for c in range(half)],
                            axis=1).T
        o = (o * lax.rsqrt(jnp.mean(o * o, axis=-1, keepdims=True) + LN_EPS)
             * g_ref[...] * (1.0 - LAMBDA_INIT))
        o_ref[0, pl.ds(pl.multiple_of(qi * t, t), t), :] = o.astype(BF16)
        return carry

    lax.fori_loop(0, nq, finish, 0)


def _attention(q, k, vt, bias, lam_vecs, g_sub):
    bsz, s, dq = q.shape
    t = ATTN_TILE
    nq = s // t
    n_near = bias.shape[1]
    hw = 2 * HEAD_DIM
    (far_tab, far_unroll), (near_tab, near_unroll) = _attn_schedules(nq, n_near)
    cb = ATTN_COL_BLOCK
    n_cb = 2 * t // cb
    return pl.pallas_call(
        functools.partial(_attn_kernel, far_unroll=far_unroll, near_unroll=near_unroll),
        grid=(bsz, N_HEADS),
        in_specs=[
            pl.BlockSpec(memory_space=pltpu.SMEM),
            pl.BlockSpec(memory_space=pltpu.SMEM),
            pl.BlockSpec((1, s, hw), lambda b, h: (b, 0, h)),
            pl.BlockSpec((1, s, hw), lambda b, h: (b, 0, h)),
            pl.BlockSpec((1, nq, V_DIM, t), lambda b, h: (b, 0, h, 0)),
            pl.BlockSpec((1, n_near, t, t), lambda b, h: (h, 0, 0, 0)),
            _const_spec((4, HEAD_DIM)),
            _const_spec((1, V_DIM)),
        ],
        out_specs=pl.BlockSpec((1, s, V_DIM), lambda b, h: (b, 0, h)),
        out_shape=jax.ShapeDtypeStruct((bsz, s, N_HEADS * V_DIM), BF16),
        scratch_shapes=[
            pltpu.VMEM((n_cb, t, cb), F32),
            pltpu.VMEM((n_cb, t, cb), F32),
            pltpu.VMEM((n_cb, 1, cb), F32),
            pltpu.VMEM((n_cb, 1, cb), F32),
            pltpu.VMEM((nq, n_cb, cb, hw), BF16),
            pltpu.VMEM((nq + 1, n_cb, 1, cb), F32),
            pltpu.VMEM((nq + 1, n_cb, 1, cb), F32),
            pltpu.VMEM((nq + 1, n_cb, V_DIM, cb), F32),
        ],
        compiler_params=pltpu.CompilerParams(
            dimension_semantics=("arbitrary", "arbitrary"),
            vmem_limit_bytes=_vmem_limit(52 * 1024 * 1024)),
        name="diff_attention",
    )(jnp.asarray(far_tab), jnp.asarray(near_tab), q, k, vt, bias, lam_vecs,
      g_sub.reshape(1, V_DIM))


def _out_proj_mlp_kernel(a_ref, x_ref, amod_ref, wo_ref, mg_ref, mb_ref,
                         mod_ref, w1_ref, w2_ref, pg_ref, pb_ref, o_ref):
    gate = amod_ref[0][2:3]
    y = jnp.dot(a_ref[0], wo_ref[...], preferred_element_type=F32)
    x = _layer_norm(ALPHA * x_ref[0] + gate * y, mg_ref[...], mb_ref[...])
    o_ref[0] = _mlp_sublayer(x, mod_ref[0], w1_ref, w2_ref, pg_ref, pb_ref)


def _out_proj_mlp(a, x, amod, wo, mg, mb, mod, w1, w2, pg, pb):
    bsz, s, d = x.shape
    da = a.shape[2]
    d_ff = w1.shape[1]
    ts = SEQ_TILE
    row = lambda v: v.reshape(1, -1)
    return pl.pallas_call(
        _out_proj_mlp_kernel,
        grid=(bsz, s // ts),
        in_specs=[
            pl.BlockSpec((1, ts, da), lambda b, i: (b, i, 0)),
            pl.BlockSpec((1, ts, d), lambda b, i: (b, i, 0)),
            pl.BlockSpec((1, 3, d), lambda b, i: (b, 0, 0)),
            _const_spec((da, d), True),
            _const_spec((1, d)),
            _const_spec((1, d)),
            pl.BlockSpec((1, 3, d), lambda b, i: (b, 0, 0)),
            _const_spec((d, d_ff), True),
            _const_spec((d_ff, d), True),
            _const_spec((1, d)),
            _const_spec((1, d)),
        ],
        out_specs=pl.BlockSpec((1, ts, d), lambda b, i: (b, i, 0)),
        out_shape=jax.ShapeDtypeStruct((bsz, s, d), F32),
        compiler_params=pltpu.CompilerParams(
            dimension_semantics=("arbitrary", "arbitrary"),
            vmem_limit_bytes=_vmem_limit(54 * 1024 * 1024)),
        name="out_proj_mlp",
    )(a, x, amod, wo.astype(BF16), row(mg), row(mb),
      mod, w1.astype(BF16), w2.astype(BF16), row(pg), row(pb))


def kernel(x, c, conv_mod_w, conv_mod_b, conv_pw1_w, conv_pw1_b, conv_dw_w, conv_dw_b, conv_norm_g, conv_norm_b, conv_pw2_w, conv_pw2_b, attn_mod_w, attn_mod_b, attn_qkv_w, attn_lam_q1, attn_lam_k1, attn_lam_q2, attn_lam_k2, attn_subln_g, attn_out_w, rel_bias, mlp_mod_w, mlp_mod_b, mlp_w1, mlp_w2, post_mix_g, post_mix_b, post_mlp_g, post_mlp_b):
    assert x.shape[1] % SEQ_TILE == 0 and x.shape[1] % ATTN_TILE == 0
    assert SEQ_TILE % CONV_ROWS == 0 and CONV_HALO >= CONV_WIDTH - 1
    conv_mod = _ada_mod(c, conv_mod_w, conv_mod_b)
    attn_mod = _ada_mod(c, attn_mod_w, attn_mod_b)
    mlp_mod = _ada_mod(c, mlp_mod_w, mlp_mod_b)

    x = _conv_mixer(x, conv_mod[0], conv_pw1_w[0], conv_pw1_b[0], conv_dw_w[0], conv_dw_b[0],
                    conv_norm_g[0], conv_norm_b[0], conv_pw2_w[0], conv_pw2_b[0],
                    post_mix_g[0], post_mix_b[0])
    x = _mlp(x, mlp_mod[0], mlp_w1[0], mlp_w2[0], post_mlp_g[0], post_mlp_b[0])

    q, k, vt = _qkv(x, attn_mod[0], attn_qkv_w[0])
    bias = _bias_tiles(rel_bias)
    lam_vecs = jnp.stack([attn_lam_q1[0], attn_lam_k1[0], attn_lam_q2[0], attn_lam_k2[0]])
    a = _attention(q, k, vt, bias, lam_vecs, attn_subln_g[0])
    return _out_proj_mlp(a, x, attn_mod[0], attn_out_w[0], post_mix_g[1], post_mix_b[1],
                         mlp_mod[1], mlp_w1[1], mlp_w2[1], post_mlp_g[1], post_mlp_b[1])
```

```python
import functools
import math

import jax
import jax.numpy as jnp
import numpy as np
from jax import lax
from jax.experimental import pallas as pl
from jax.experimental.pallas import tpu as pltpu

DEPTH = 2
CONV_WIDTH = 31
N_HEADS = 8
HEAD_DIM = 64
V_DIM = 2 * HEAD_DIM
REL_BUCKETS = 32
REL_MAX_DIST = 128
ALPHA = (2 * DEPTH) ** 0.25
LN_EPS = 1e-5
ATTN_LAYER = 1
LAMBDA_INIT = 0.8 - 0.6 * math.exp(-0.3 * ATTN_LAYER)
LOG2_E = math.log2(math.e)

V7X_SUBLANES = 8
V7X_LANES = 128
V7X_VMEM_BYTES = 64 * 1024 * 1024

MASK_VALUE = -1e30
CONV_HALO = 32
CONV_ROWS = 64
SEQ_TILE = 512
ATTN_TILE = 512
FF_CHUNK = 1024

F32 = jnp.float32
BF16 = jnp.bfloat16


def _t5_thresholds():
    max_exact = REL_BUCKETS // 2
    buckets = []
    for n in range(2 * REL_MAX_DIST):
        if n < max_exact:
            buckets.append(n)
        else:
            v = math.log(n / max_exact) / math.log(REL_MAX_DIST / max_exact) * (REL_BUCKETS - max_exact)
            buckets.append(min(max_exact + int(v), REL_BUCKETS - 1))
    assert all(b1 >= b0 for b0, b1 in zip(buckets, buckets[1:]))
    assert buckets[-1] == REL_BUCKETS - 1
    return [buckets.index(b) for b in range(REL_BUCKETS)]


T5_THRESHOLDS = _t5_thresholds()
T5_LAST_BUCKET_START = T5_THRESHOLDS[REL_BUCKETS - 1]


def _vmem_limit(nbytes):
    return int(min(nbytes, V7X_VMEM_BYTES - 8 * 1024 * 1024))


def _layer_norm(z, g, b):
    mu = jnp.mean(z, axis=-1, keepdims=True)
    zc = z - mu
    var = jnp.mean(zc * zc, axis=-1, keepdims=True)
    return zc * lax.rsqrt(var + LN_EPS) * g + b


def _const_spec(shape, single_buffer=False):
    nd = len(shape)
    kwargs = {"pipeline_mode": pl.Buffered(1)} if single_buffer else {}
    return pl.BlockSpec(shape, lambda *_: (0,) * nd, **kwargs)


def _ada_mod_kernel(c_ref, w_ref, b_ref, o_ref):
    c = c_ref[...]
    sc = c * jax.nn.sigmoid(c)
    o_ref[0] = jnp.dot(sc, w_ref[0], preferred_element_type=F32,
                       precision=lax.Precision.HIGHEST) + b_ref[0]


def _ada_mod(c, w, b):
    n, d, d3 = w.shape
    bsz = c.shape[0]
    nblk = d3 // d
    out = pl.pallas_call(
        _ada_mod_kernel,
        grid=(n, nblk),
        in_specs=[
            pl.BlockSpec((bsz, d), lambda i, j: (0, 0)),
            pl.BlockSpec((1, d, d), lambda i, j: (i, 0, j)),
            pl.BlockSpec((1, 1, d), lambda i, j: (i, 0, j)),
        ],
        out_specs=pl.BlockSpec((1, bsz, d), lambda i, j: (i, 0, j)),
        out_shape=jax.ShapeDtypeStruct((n, bsz, d3), F32),
        compiler_params=pltpu.CompilerParams(
            dimension_semantics=("arbitrary", "arbitrary"),
            vmem_limit_bytes=_vmem_limit(32 * 1024 * 1024)),
        name="ada_mod",
    )(c, w, b.reshape(n, 1, d3))
    return out.reshape(n, bsz, 3, d)


def _conv_mixer_kernel(x_ref, mod_ref, w1_ref, b1_ref, wdw_ref, bdw_ref, gcn_ref, bcn_ref,
                       w2_ref, b2_ref, pg_ref, pb_ref, o_ref, ext_ref, y_ref):
    ts, d = x_ref.shape[1], x_ref.shape[2]
    n_lb = ext_ref.shape[0]
    s_idx = pl.program_id(1)

    @pl.when(s_idx == 0)
    def _():
        ext_ref[:, 0:CONV_HALO, :] = jnp.zeros((n_lb, CONV_HALO, V7X_LANES), F32)

    mod = mod_ref[0]
    shift, scale, gate = mod[0:1], mod[1:2], mod[2:3]
    x = x_ref[0]
    h = (x * (1.0 + scale) + shift).astype(BF16)
    a = jnp.dot(h, w1_ref[...], preferred_element_type=F32) + b1_ref[...]
    u = a[:, :d] * jax.nn.sigmoid(a[:, d:])
    for lb in range(n_lb):
        ext_ref[lb, CONV_HALO:CONV_HALO + ts, :] = u[:, lb * V7X_LANES:(lb + 1) * V7X_LANES]

    off0 = CONV_HALO - (CONV_WIDTH - 1)

    n_grp = CONV_ROWS // V7X_SUBLANES

    def row_block(rb, carry):
        t0 = pl.multiple_of(rb * CONV_ROWS, CONV_ROWS)

        def lane_block(lb, carry2):
            w_all = wdw_ref[lb]
            acc = [jnp.broadcast_to(bdw_ref[lb], (V7X_SUBLANES, V7X_LANES))] * n_grp
            for r in range(V7X_SUBLANES):
                taps = [j for j in range(CONV_WIDTH) if (off0 + j) % V7X_SUBLANES == r]
                first = off0 + taps[0]
                n_load = n_grp + (taps[-1] - taps[0]) // V7X_SUBLANES
                groups = [ext_ref[lb, pl.ds(t0 + (first + V7X_SUBLANES * g), V7X_SUBLANES), :]
                          for g in range(n_load)]
                for j in taps:
                    w_row = jnp.broadcast_to(w_all[j:j + 1], (V7X_SUBLANES, V7X_LANES))
                    g0 = (off0 + j - first) // V7X_SUBLANES
                    acc = [acc[i] + w_row * groups[g0 + i] for i in range(n_grp)]
            y_ref[lb, pl.ds(t0, CONV_ROWS), :] = jnp.concatenate(acc, axis=0)
            return carry2

        lax.fori_loop(0, n_lb, lane_block, 0)
        return carry

    lax.fori_loop(0, ts // CONV_ROWS, row_block, 0)
    ext_ref[:, 0:CONV_HALO, :] = ext_ref[:, ts:ts + CONV_HALO, :]

    y = y_ref[...]
    mu = jnp.sum(jnp.sum(y, axis=0, keepdims=True), axis=2, keepdims=True) * (1.0 / d)
    yc = y - mu
    var = jnp.sum(jnp.sum(yc * yc, axis=0, keepdims=True), axis=2, keepdims=True) * (1.0 / d)
    y = yc * lax.rsqrt(var + LN_EPS) * gcn_ref[...] + bcn_ref[...]
    y = (y * jax.nn.sigmoid(y)).astype(BF16)
    v = jnp.concatenate([y[lb] for lb in range(n_lb)], axis=-1)
    y = jnp.dot(v, w2_ref[...], preferred_element_type=F32) + b2_ref[...]
    o_ref[0] = _layer_norm(ALPHA * x_ref[0] + gate * y, pg_ref[...], pb_ref[...])


def _conv_mixer(x, mod, w1, b1, wdw, bdw, gcn, bcn, w2, b2, pg, pb):
    bsz, s, d = x.shape
    ts = SEQ_TILE
    n_lb = d // V7X_LANES
    row = lambda v: v.reshape(1, -1)
    slab = lambda v: v.reshape(-1, n_lb, V7X_LANES).transpose(1, 0, 2)
    return pl.pallas_call(
        _conv_mixer_kernel,
        grid=(bsz, s // ts),
        in_specs=[
            pl.BlockSpec((1, ts, d), lambda b, i: (b, i, 0)),
            pl.BlockSpec((1, 3, d), lambda b, i: (b, 0, 0)),
            _const_spec((d, 2 * d), True),
            _const_spec((1, 2 * d)),
            _const_spec((n_lb, CONV_WIDTH, V7X_LANES)),
            _const_spec((n_lb, 1, V7X_LANES)),
            _const_spec((n_lb, 1, V7X_LANES)),
            _const_spec((n_lb, 1, V7X_LANES)),
            _const_spec((d, d), True),
            _const_spec((1, d)),
            _const_spec((1, d)),
            _const_spec((1, d)),
        ],
        out_specs=pl.BlockSpec((1, ts, d), lambda b, i: (b, i, 0)),
        out_shape=jax.ShapeDtypeStruct((bsz, s, d), F32),
        scratch_shapes=[
            pltpu.VMEM((n_lb, ts + CONV_HALO, V7X_LANES), F32),
            pltpu.VMEM((n_lb, ts, V7X_LANES), F32),
        ],
        compiler_params=pltpu.CompilerParams(
            dimension_semantics=("arbitrary", "arbitrary"),
            vmem_limit_bytes=_vmem_limit(48 * 1024 * 1024)),
        name="conv_mixer",
    )(x, mod, w1.astype(BF16), row(b1), slab(wdw), slab(bdw), slab(gcn), slab(bcn),
      w2.astype(BF16), row(b2), row(pg), row(pb))


def _mlp_sublayer(x, mod, w1_ref, w2_ref, pg_ref, pb_ref):
    d_ff = w1_ref.shape[1]
    shift, scale, gate = mod[0:1], mod[1:2], mod[2:3]
    h = (x * (1.0 + scale) + shift).astype(BF16)
    y = None
    for c0 in range(0, d_ff, FF_CHUNK):
        a = jnp.dot(h, w1_ref[:, c0:c0 + FF_CHUNK], preferred_element_type=F32)
        a = jnp.maximum(a, 0.0)
        part = jnp.dot((a * a).astype(BF16), w2_ref[c0:c0 + FF_CHUNK, :], preferred_element_type=F32)
        y = part if y is None else y + part
    return _layer_norm(ALPHA * x + gate * y, pg_ref[...], pb_ref[...])


def _mlp_kernel(x_ref, mod_ref, w1_ref, w2_ref, pg_ref, pb_ref, o_ref):
    o_ref[0] = _mlp_sublayer(x_ref[0], mod_ref[0], w1_ref, w2_ref, pg_ref, pb_ref)


def _mlp(x, mod, w1, w2, pg, pb):
    bsz, s, d = x.shape
    d_ff = w1.shape[1]
    ts = SEQ_TILE
    row = lambda v: v.reshape(1, -1)
    return pl.pallas_call(
        _mlp_kernel,
        grid=(bsz, s // ts),
        in_specs=[
            pl.BlockSpec((1, ts, d), lambda b, i: (b, i, 0)),
            pl.BlockSpec((1, 3, d), lambda b, i: (b, 0, 0)),
            _const_spec((d, d_ff), True),
            _const_spec((d_ff, d), True),
            _const_spec((1, d)),
            _const_spec((1, d)),
        ],
        out_specs=pl.BlockSpec((1, ts, d), lambda b, i: (b, i, 0)),
        out_shape=jax.ShapeDtypeStruct((bsz, s, d), F32),
        compiler_params=pltpu.CompilerParams(
            dimension_semantics=("arbitrary", "arbitrary"),
            vmem_limit_bytes=_vmem_limit(52 * 1024 * 1024)),
        name="sq_relu_mlp",
    )(x, mod, w1.astype(BF16), w2.astype(BF16), row(pg), row(pb))


def _qkv_kernel(x_ref, mod_ref, wq_ref, wk_ref, wvt_ref, q_ref, k_ref, vt_ref):
    mod = mod_ref[0]
    shift, scale = mod[0:1], mod[1:2]
    h = (x_ref[0] * (1.0 + scale) + shift).astype(BF16)
    q = jnp.dot(h, wq_ref[...], preferred_element_type=F32)
    q_ref[0] = (q * (HEAD_DIM ** -0.5 * LOG2_E)).astype(BF16)
    k_ref[0] = jnp.dot(h, wk_ref[...], preferred_element_type=F32).astype(BF16)
    vt = lax.dot_general(wvt_ref[...], h, (((1,), (1,)), ((), ())), preferred_element_type=F32)
    vt_ref[0, 0] = vt.astype(BF16)


def _qkv(x, mod, w_qkv):
    bsz, s, d = x.shape
    tk = ATTN_TILE
    dq = N_HEADS * 2 * HEAD_DIM
    dv = N_HEADS * V_DIM
    wq = w_qkv[:, :dq].astype(BF16)
    wk = w_qkv[:, dq:2 * dq].astype(BF16)
    wvt = w_qkv[:, 2 * dq:].T.astype(BF16)
    return pl.pallas_call(
        _qkv_kernel,
        grid=(bsz, s // tk),
        in_specs=[
            pl.BlockSpec((1, tk, d), lambda b, i: (b, i, 0)),
            pl.BlockSpec((1, 3, d), lambda b, i: (b, 0, 0)),
            _const_spec((d, dq), True),
            _const_spec((d, dq), True),
            _const_spec((dv, d), True),
        ],
        out_specs=[
            pl.BlockSpec((1, tk, dq), lambda b, i: (b, i, 0)),
            pl.BlockSpec((1, tk, dq), lambda b, i: (b, i, 0)),
            pl.BlockSpec((1, 1, dv, tk), lambda b, i: (b, i, 0, 0)),
        ],
        out_shape=[
            jax.ShapeDtypeStruct((bsz, s, dq), BF16),
            jax.ShapeDtypeStruct((bsz, s, dq), BF16),
            jax.ShapeDtypeStruct((bsz, s // tk, dv, tk), BF16),
        ],
        compiler_params=pltpu.CompilerParams(
            dimension_semantics=("arbitrary", "arbitrary"),
            vmem_limit_bytes=_vmem_limit(40 * 1024 * 1024)),
        name="qkv_proj",
    )(x, mod, wq, wk, wvt)


def _bias_tiles_kernel(rb_ref, o_ref):
    h = pl.program_id(0)
    t = o_ref.shape[2]
    far = rb_ref[h, REL_BUCKETS - 1]
    kk = lax.broadcasted_iota(jnp.int32, (t, t), 0)
    qq = lax.broadcasted_iota(jnp.int32, (t, t), 1)
    for dlt in range(o_ref.shape[1]):
        rel = dlt * t + qq - kk
        val = jnp.full((t, t), (rb_ref[h, 0] - far) * LOG2_E, F32)
        for bkt in range(1, REL_BUCKETS):
            val = jnp.where(rel >= T5_THRESHOLDS[bkt], (rb_ref[h, bkt] - far) * LOG2_E, val)
        o_ref[0, dlt] = jnp.where(rel >= 0, val, MASK_VALUE)


def _bias_tiles(rel_bias):
    t = ATTN_TILE
    n_near = -(-(T5_LAST_BUCKET_START + t - 1) // t)
    return pl.pallas_call(
        _bias_tiles_kernel,
        grid=(N_HEADS,),
        in_specs=[pl.BlockSpec(memory_space=pltpu.SMEM)],
        out_specs=pl.BlockSpec((1, n_near, t, t), lambda h: (h, 0, 0, 0)),
        out_shape=jax.ShapeDtypeStruct((N_HEADS, n_near, t, t), F32),
        compiler_params=pltpu.CompilerParams(dimension_semantics=("arbitrary",)),
        name="t5_bias_tiles",
    )(rel_bias.T)


_TAB_QI, _TAB_KJ, _TAB_BIAS = range(3)
_PIPE_DEPTH = 1
_UNROLL_CHOICES = (20, 18, 16, 14, 12)
ATTN_COL_BLOCK = 256


def _attn_schedule(pairs, inert):
    cols = [inert] * _PIPE_DEPTH + list(pairs) + [inert] * _PIPE_DEPTH
    n_ticks = len(cols) - _PIPE_DEPTH
    unroll = min(_UNROLL_CHOICES, key=lambda u: (-n_ticks % u, -u))
    cols += [inert] * (-n_ticks % unroll)
    return np.asarray(cols, np.int32).T, unroll


_NEAR_PERIOD = 2


def _near_distance(col):
    return (col + 1) % _NEAR_PERIOD


def _near_extent(dist, c, t, cb):
    if dist is None:
        return t, t
    q_lo = (c * cb) % t
    if dist == 0:
        return min(t, q_lo + cb), 0
    lo = dist * t + q_lo - T5_LAST_BUCKET_START + 1
    return t, min(t, max(0, lo // V7X_SUBLANES * V7X_SUBLANES))


def _attn_schedules(nq, n_near):
    assert n_near == _NEAR_PERIOD
    far = [(qi, kj, 0) for qi in range(nq) for kj in range(qi - n_near + 1)]
    near = [(qi, qi - d, d) for qi in range(nq) for d in reversed(range(min(n_near, qi + 1)))]
    assert all(d == _near_distance(_PIPE_DEPTH + i) for i, (_, _, d) in enumerate(near))
    return _attn_schedule(far, (nq, 0, 0)), _attn_schedule(near, (nq, 0, 0))


def _attn_kernel(far_ref, near_ref, q_ref, k_ref, vt_ref, bias_ref, lam_ref, g_ref, o_ref,
                 s0_ref, s1_ref, mt0_ref, mt1_ref, qc_ref, m_ref, l_ref, acc_ref,
                 *, far_unroll, near_unroll):
    s_refs, mt_refs = (s0_ref, s1_ref), (mt0_ref, mt1_ref)
    n_cb, t, cb = s0_ref.shape
    nq = m_ref.shape[0] - 1

    for slot in range(2):
        s_refs[slot][...] = jnp.zeros(s_refs[slot].shape, F32)
        mt_refs[slot][...] = jnp.zeros(mt_refs[slot].shape, F32)
    m_ref[...] = jnp.full(m_ref.shape, MASK_VALUE, F32)
    l_ref[...] = jnp.zeros(l_ref.shape, F32)
    acc_ref[...] = jnp.zeros(acc_ref.shape, F32)

    def mask_q(qi, carry):
        q = q_ref[0, pl.ds(pl.multiple_of(qi * t, t), t), :]
        lane = lax.broadcasted_iota(jnp.int32, q.shape, 1)
        zero = jnp.zeros_like(q)
        qcat = jnp.concatenate([jnp.where(lane < HEAD_DIM, q, zero),
                                jnp.where(lane >= HEAD_DIM, q, zero)], axis=0)
        qc_ref[qi] = qcat.reshape(n_cb, cb, q.shape[1])
        return carry

    lax.fori_loop(0, nq, mask_q, 0)

    def q_tile_index(qi):
        return jnp.where(qi == nq, 0, qi)

    def tick(tab_ref, n, u, near):
        col_a, col_b = n + 1, n
        par, cur = u % 2, 1 - u % 2
        dist_a, dist_b = (_near_distance(u + 1), _near_distance(u)) if near else (None, None)

        qi_a = q_tile_index(tab_ref[_TAB_QI, col_a])
        qi_b = tab_ref[_TAB_QI, col_b]
        vt = vt_ref[0, tab_ref[_TAB_KJ, col_b]]
        k = k_ref[0, pl.ds(pl.multiple_of(tab_ref[_TAB_KJ, col_a] * t, t), t), :]

        for c in range(n_cb):
            keys_a, bias_lo = _near_extent(dist_a, c, t, cb)
            st = lax.dot_general(k[:keys_a], qc_ref[qi_a, c], (((1,), (1,)), ((), ())),
                                 preferred_element_type=F32)
            if bias_lo < keys_a:
                bias = bias_ref[0, dist_a, bias_lo:keys_a, pl.ds((c * cb) % t, cb)]
                st = jnp.concatenate([st[:bias_lo], st[bias_lo:] + bias], axis=0) if bias_lo else st + bias
            s_refs[par][c, 0:keys_a] = st
            mt_refs[par][c] = jnp.max(st, axis=0, keepdims=True)

            keys_b, _ = _near_extent(dist_b, c, t, cb)
            m_old = m_ref[qi_b, c]
            m_new = jnp.maximum(m_old, mt_refs[cur][c])
            m_ref[qi_b, c] = m_new
            p = jnp.exp2(s_refs[cur][c, 0:keys_b] - m_new)
            pv = jnp.dot(vt[:, :keys_b], p.astype(BF16), preferred_element_type=F32)
            alpha = jnp.exp2(m_old - m_new)
            acc_ref[qi_b, c] = alpha * acc_ref[qi_b, c] + pv
            l_ref[qi_b, c] = alpha * l_ref[qi_b, c] + jnp.sum(p, axis=0, keepdims=True)

    def run(tab_ref, unroll, near):
        n_ticks = tab_ref.shape[1] - _PIPE_DEPTH
        assert n_ticks % unroll == 0 and unroll % _NEAR_PERIOD == 0

        def body(i, carry):
            for u in range(unroll):
                tick(tab_ref, unroll * i + u, u, near)
            return carry

        lax.fori_loop(0, n_ticks // unroll, body, 0)

    run(far_ref, far_unroll, False)
    run(near_ref, near_unroll, True)

    lam_v = lam_ref[...]
    lam = (jnp.exp(jnp.sum(lam_v[0:1] * lam_v[1:2], axis=-1, keepdims=True))
           - jnp.exp(jnp.sum(lam_v[2:3] * lam_v[3:4], axis=-1, keepdims=True)) + LAMBDA_INIT)

    def finish(qi, carry):
        def normalised(c):
            return acc_ref[qi, c] * (1.0 / l_ref[qi, c])
        half = n_cb // 2
        o = jnp.concatenate([normalised(c) - lam * normalised(half + c) for c in range(half)],
                            axis=1).T
        o = (o * lax.rsqrt(jnp.mean(o * o, axis=-1, keepdims=True) + LN_EPS)
             * g_ref[...] * (1.0 - LAMBDA_INIT))
        o_ref[0, pl.ds(pl.multiple_of(qi * t, t), t), :] = o.astype(BF16)
        return carry

    lax.fori_loop(0, nq, finish, 0)


def _attention(q, k, vt, bias, lam_vecs, g_sub):
    bsz, s, dq = q.shape
    t = ATTN_TILE
    nq = s // t
    n_near = bias.shape[1]
    hw = 2 * HEAD_DIM
    (far_tab, far_unroll), (near_tab, near_unroll) = _attn_schedules(nq, n_near)
    cb = ATTN_COL_BLOCK
    n_cb = 2 * t // cb
    return pl.pallas_call(
        functools.partial(_attn_kernel, far_unroll=far_unroll, near_unroll=near_unroll),
        grid=(bsz, N_HEADS),
        in_specs=[
            pl.BlockSpec(memory_space=pltpu.SMEM),
            pl.BlockSpec(memory_space=pltpu.SMEM),
            pl.BlockSpec((1, s, hw), lambda b, h: (b, 0, h)),
            pl.BlockSpec((1, s, hw), lambda b, h: (b, 0, h)),
            pl.BlockSpec((1, nq, V_DIM, t), lambda b, h: (b, 0, h, 0)),
            pl.BlockSpec((1, n_near, t, t), lambda b, h: (h, 0, 0, 0)),
            _const_spec((4, HEAD_DIM)),
            _const_spec((1, V_DIM)),
        ],
        out_specs=pl.BlockSpec((1, s, V_DIM), lambda b, h: (b, 0, h)),
        out_shape=jax.ShapeDtypeStruct((bsz, s, N_HEADS * V_DIM), BF16),
        scratch_shapes=[
            pltpu.VMEM((n_cb, t, cb), F32),
            pltpu.VMEM((n_cb, t, cb), F32),
            pltpu.VMEM((n_cb, 1, cb), F32),
            pltpu.VMEM((n_cb, 1, cb), F32),
            pltpu.VMEM((nq, n_cb, cb, hw), BF16),
            pltpu.VMEM((nq + 1, n_cb, 1, cb), F32),
            pltpu.VMEM((nq + 1, n_cb, 1, cb), F32),
            pltpu.VMEM((nq + 1, n_cb, V_DIM, cb), F32),
        ],
        compiler_params=pltpu.CompilerParams(
            dimension_semantics=("arbitrary", "arbitrary"),
            vmem_limit_bytes=_vmem_limit(52 * 1024 * 1024)),
        name="diff_attention",
    )(jnp.asarray(far_tab), jnp.asarray(near_tab), q, k, vt, bias, lam_vecs,
      g_sub.reshape(1, V_DIM))


def _out_proj_mlp_kernel(a_ref, x_ref, amod_ref, wo_ref, mg_ref, mb_ref,
                         mod_ref, w1_ref, w2_ref, pg_ref, pb_ref, o_ref):
    gate = amod_ref[0][2:3]
    y = jnp.dot(a_ref[0], wo_ref[...], preferred_element_type=F32)
    x = _layer_norm(ALPHA * x_ref[0] + gate * y, mg_ref[...], mb_ref[...])
    o_ref[0] = _mlp_sublayer(x, mod_ref[0], w1_ref, w2_ref, pg_ref, pb_ref)


def _out_proj_mlp(a, x, amod, wo, mg, mb, mod, w1, w2, pg, pb):
    bsz, s, d = x.shape
    da = a.shape[2]
    d_ff = w1.shape[1]
    ts = SEQ_TILE
    row = lambda v: v.reshape(1, -1)
    return pl.pallas_call(
        _out_proj_mlp_kernel,
        grid=(bsz, s // ts),
        in_specs=[
            pl.BlockSpec((1, ts, da), lambda b, i: (b, i, 0)),
            pl.BlockSpec((1, ts, d), lambda b, i: (b, i, 0)),
            pl.BlockSpec((1, 3, d), lambda b, i: (b, 0, 0)),
            _const_spec((da, d), True),
            _const_spec((1, d)),
            _const_spec((1, d)),
            pl.BlockSpec((1, 3, d), lambda b, i: (b, 0, 0)),
            _const_spec((d, d_ff), True),
            _const_spec((d_ff, d), True),
            _const_spec((1, d)),
            _const_spec((1, d)),
        ],
        out_specs=pl.BlockSpec((1, ts, d), lambda b, i: (b, i, 0)),
        out_shape=jax.ShapeDtypeStruct((bsz, s, d), F32),
        compiler_params=pltpu.CompilerParams(
            dimension_semantics=("arbitrary", "arbitrary"),
            vmem_limit_bytes=_vmem_limit(54 * 1024 * 1024)),
        name="out_proj_mlp",
    )(a, x, amod, wo.astype(BF16), row(mg), row(mb),
      mod, w1.astype(BF16), w2.astype(BF16), row(pg), row(pb))


def kernel(x, c, conv_mod_w, conv_mod_b, conv_pw1_w, conv_pw1_b, conv_dw_w, conv_dw_b, conv_norm_g, conv_norm_b, conv_pw2_w, conv_pw2_b, attn_mod_w, attn_mod_b, attn_qkv_w, attn_lam_q1, attn_lam_k1, attn_lam_q2, attn_lam_k2, attn_subln_g, attn_out_w, rel_bias, mlp_mod_w, mlp_mod_b, mlp_w1, mlp_w2, post_mix_g, post_mix_b, post_mlp_g, post_mlp_b):
    assert x.shape[1] % SEQ_TILE == 0 and x.shape[1] % ATTN_TILE == 0
    assert SEQ_TILE % CONV_ROWS == 0 and CONV_HALO >= CONV_WIDTH - 1
    conv_mod = _ada_mod(c, conv_mod_w, conv_mod_b)
    attn_mod = _ada_mod(c, attn_mod_w, attn_mod_b)
    mlp_mod = _ada_mod(c, mlp_mod_w, mlp_mod_b)

    x = _conv_mixer(x, conv_mod[0], conv_pw1_w[0], conv_pw1_b[0], conv_dw_w[0], conv_dw_b[0],
                    conv_norm_g[0], conv_norm_b[0], conv_pw2_w[0], conv_pw2_b[0],
                    post_mix_g[0], post_mix_b[0])
    x = _mlp(x, mlp_mod[0], mlp_w1[0], mlp_w2[0], post_mlp_g[0], post_mlp_b[0])

    q, k, vt = _qkv(x, attn_mod[0], attn_qkv_w[0])
    bias = _bias_tiles(rel_bias)
    lam_vecs = jnp.stack([attn_lam_q1[0], attn_lam_k1[0], attn_lam_q2[0], attn_lam_k2[0]])
    a = _attention(q, k, vt, bias, lam_vecs, attn_subln_g[0])
    return _out_proj_mlp(a, x, attn_mod[0], attn_out_w[0], post_mix_g[1], post_mix_b[1],
                         mlp_mod[1], mlp_w1[1], mlp_w2[1], post_mlp_g[1], post_mlp_b[1])
```

```python
import functools
import math

import jax
import jax.numpy as jnp
import numpy as np
from jax import lax
from jax.experimental import pallas as pl
from jax.experimental.pallas import tpu as pltpu

DEPTH = 2
CONV_WIDTH = 31
N_HEADS = 8
HEAD_DIM = 64
V_DIM = 2 * HEAD_DIM
REL_BUCKETS = 32
REL_MAX_DIST = 128
ALPHA = (2 * DEPTH) ** 0.25
LN_EPS = 1e-5
ATTN_LAYER = 1
LAMBDA_INIT = 0.8 - 0.6 * math.exp(-0.3 * ATTN_LAYER)
LOG2_E = math.log2(math.e)

V7X_SUBLANES = 8
V7X_LANES = 128
V7X_VMEM_BYTES = 64 * 1024 * 1024
V7X_VMEM_RESERVE_BYTES = 8 * 1024 * 1024

MASK_VALUE = -1e30
CONV_HALO = 32
CONV_ROWS = 64
SEQ_TILE = 512
ATTN_TILE = 512
FF_CHUNK = 1024

F32 = jnp.float32
BF16 = jnp.bfloat16


def _t5_thresholds():
    max_exact = REL_BUCKETS // 2
    buckets = []
    for n in range(2 * REL_MAX_DIST):
        if n < max_exact:
            buckets.append(n)
        else:
            v = math.log(n / max_exact) / math.log(REL_MAX_DIST / max_exact) * (REL_BUCKETS - max_exact)
            buckets.append(min(max_exact + int(v), REL_BUCKETS - 1))
    assert all(b1 >= b0 for b0, b1 in zip(buckets, buckets[1:]))
    assert buckets[-1] == REL_BUCKETS - 1
    return [buckets.index(b) for b in range(REL_BUCKETS)]


T5_THRESHOLDS = _t5_thresholds()
T5_LAST_BUCKET_START = T5_THRESHOLDS[REL_BUCKETS - 1]


def _vmem_limit(nbytes):
    return int(min(nbytes, V7X_VMEM_BYTES - V7X_VMEM_RESERVE_BYTES))


def _layer_norm(z, g, b):
    mu = jnp.mean(z, axis=-1, keepdims=True)
    zc = z - mu
    var = jnp.mean(zc * zc, axis=-1, keepdims=True)
    return zc * lax.rsqrt(var + LN_EPS) * g + b


def _const_spec(shape, single_buffer=False):
    nd = len(shape)
    kwargs = {"pipeline_mode": pl.Buffered(1)} if single_buffer else {}
    return pl.BlockSpec(shape, lambda *_: (0,) * nd, **kwargs)


def _ada_mod_kernel(c_ref, w_ref, b_ref, o_ref):
    c = c_ref[...]
    sc = c * jax.nn.sigmoid(c)
    o_ref[0] = jnp.dot(sc, w_ref[0], preferred_element_type=F32,
                       precision=lax.Precision.HIGHEST) + b_ref[0]


def _ada_mod(c, w, b):
    n, d, d3 = w.shape
    bsz = c.shape[0]
    nblk = d3 // d
    out = pl.pallas_call(
        _ada_mod_kernel,
        grid=(n, nblk),
        in_specs=[
            pl.BlockSpec((bsz, d), lambda i, j: (0, 0)),
            pl.BlockSpec((1, d, d), lambda i, j: (i, 0, j)),
            pl.BlockSpec((1, 1, d), lambda i, j: (i, 0, j)),
        ],
        out_specs=pl.BlockSpec((1, bsz, d), lambda i, j: (i, 0, j)),
        out_shape=jax.ShapeDtypeStruct((n, bsz, d3), F32),
        compiler_params=pltpu.CompilerParams(
            dimension_semantics=("arbitrary", "arbitrary"),
            vmem_limit_bytes=_vmem_limit(32 * 1024 * 1024)),
        name="ada_mod",
    )(c, w, b.reshape(n, 1, d3))
    return out.reshape(n, bsz, 3, d)


def _conv_mixer_kernel(x_ref, mod_ref, w1_ref, b1_ref, wdw_ref, bdw_ref, gcn_ref, bcn_ref,
                       w2_ref, b2_ref, pg_ref, pb_ref, o_ref, ext_ref, y_ref):
    ts, d = x_ref.shape[1], x_ref.shape[2]
    n_lb = ext_ref.shape[0]
    s_idx = pl.program_id(1)

    @pl.when(s_idx == 0)
    def _():
        ext_ref[:, 0:CONV_HALO, :] = jnp.zeros((n_lb, CONV_HALO, V7X_LANES), F32)

    mod = mod_ref[0]
    shift, scale, gate = mod[0:1], mod[1:2], mod[2:3]
    x = x_ref[0]
    h = (x * (1.0 + scale) + shift).astype(BF16)
    a = jnp.dot(h, w1_ref[...], preferred_element_type=F32) + b1_ref[...]
    u = a[:, :d] * jax.nn.sigmoid(a[:, d:])
    for lb in range(n_lb):
        ext_ref[lb, CONV_HALO:CONV_HALO + ts, :] = u[:, lb * V7X_LANES:(lb + 1) * V7X_LANES]

    off0 = CONV_HALO - (CONV_WIDTH - 1)

    n_grp = CONV_ROWS // V7X_SUBLANES

    def row_block(rb, carry):
        t0 = pl.multiple_of(rb * CONV_ROWS, CONV_ROWS)

        def lane_block(lb, carry2):
            w_all = wdw_ref[lb]
            acc = [jnp.broadcast_to(bdw_ref[lb], (V7X_SUBLANES, V7X_LANES))] * n_grp
            for r in range(V7X_SUBLANES):
                taps = [j for j in range(CONV_WIDTH) if (off0 + j) % V7X_SUBLANES == r]
                first = off0 + taps[0]
                n_load = n_grp + (taps[-1] - taps[0]) // V7X_SUBLANES
                groups = [ext_ref[lb, pl.ds(t0 + (first + V7X_SUBLANES * g), V7X_SUBLANES), :]
                          for g in range(n_load)]
                for j in taps:
                    w_row = jnp.broadcast_to(w_all[j:j + 1], (V7X_SUBLANES, V7X_LANES))
                    g0 = (off0 + j - first) // V7X_SUBLANES
                    acc = [acc[i] + w_row * groups[g0 + i] for i in range(n_grp)]
            y_ref[lb, pl.ds(t0, CONV_ROWS), :] = jnp.concatenate(acc, axis=0)
            return carry2

        lax.fori_loop(0, n_lb, lane_block, 0)
        return carry

    lax.fori_loop(0, ts // CONV_ROWS, row_block, 0)
    ext_ref[:, 0:CONV_HALO, :] = ext_ref[:, ts:ts + CONV_HALO, :]

    y = y_ref[...]
    mu = jnp.sum(jnp.sum(y, axis=0, keepdims=True), axis=2, keepdims=True) * (1.0 / d)
    yc = y - mu
    var = jnp.sum(jnp.sum(yc * yc, axis=0, keepdims=True), axis=2, keepdims=True) * (1.0 / d)
    y = yc * lax.rsqrt(var + LN_EPS) * gcn_ref[...] + bcn_ref[...]
    y = (y * jax.nn.sigmoid(y)).astype(BF16)
    v = jnp.concatenate([y[lb] for lb in range(n_lb)], axis=-1)
    y = jnp.dot(v, w2_ref[...], preferred_element_type=F32) + b2_ref[...]
    o_ref[0] = _layer_norm(ALPHA * x_ref[0] + gate * y, pg_ref[...], pb_ref[...])


def _conv_mixer(x, mod, w1, b1, wdw, bdw, gcn, bcn, w2, b2, pg, pb):
    bsz, s, d = x.shape
    ts = SEQ_TILE
    n_lb = d // V7X_LANES
    row = lambda v: v.reshape(1, -1)
    slab = lambda v: v.reshape(-1, n_lb, V7X_LANES).transpose(1, 0, 2)
    return pl.pallas_call(
        _conv_mixer_kernel,
        grid=(bsz, s // ts),
        in_specs=[
            pl.BlockSpec((1, ts, d), lambda b, i: (b, i, 0)),
            pl.BlockSpec((1, 3, d), lambda b, i: (b, 0, 0)),
            _const_spec((d, 2 * d), True),
            _const_spec((1, 2 * d)),
            _const_spec((n_lb, CONV_WIDTH, V7X_LANES)),
            _const_spec((n_lb, 1, V7X_LANES)),
            _const_spec((n_lb, 1, V7X_LANES)),
            _const_spec((n_lb, 1, V7X_LANES)),
            _const_spec((d, d), True),
            _const_spec((1, d)),
            _const_spec((1, d)),
            _const_spec((1, d)),
        ],
        out_specs=pl.BlockSpec((1, ts, d), lambda b, i: (b, i, 0)),
        out_shape=jax.ShapeDtypeStruct((bsz, s, d), F32),
        scratch_shapes=[
            pltpu.VMEM((n_lb, ts + CONV_HALO, V7X_LANES), F32),
            pltpu.VMEM((n_lb, ts, V7X_LANES), F32),
        ],
        compiler_params=pltpu.CompilerParams(
            dimension_semantics=("arbitrary", "arbitrary"),
            vmem_limit_bytes=_vmem_limit(48 * 1024 * 1024)),
        name="conv_mixer",
    )(x, mod, w1.astype(BF16), row(b1), slab(wdw), slab(bdw), slab(gcn), slab(bcn),
      w2.astype(BF16), row(b2), row(pg), row(pb))


def _mlp_sublayer(x, mod, w1_ref, w2_ref, pg_ref, pb_ref):
    d_ff = w1_ref.shape[1]
    shift, scale, gate = mod[0:1], mod[1:2], mod[2:3]
    h = (x * (1.0 + scale) + shift).astype(BF16)
    y = None
    for c0 in range(0, d_ff, FF_CHUNK):
        a = jnp.dot(h, w1_ref[:, c0:c0 + FF_CHUNK], preferred_element_type=F32)
        a = jnp.maximum(a, 0.0)
        part = jnp.dot((a * a).astype(BF16), w2_ref[c0:c0 + FF_CHUNK, :], preferred_element_type=F32)
        y = part if y is None else y + part
    return _layer_norm(ALPHA * x + gate * y, pg_ref[...], pb_ref[...])


def _mlp_kernel(x_ref, mod_ref, w1_ref, w2_ref, pg_ref, pb_ref, o_ref):
    o_ref[0] = _mlp_sublayer(x_ref[0], mod_ref[0], w1_ref, w2_ref, pg_ref, pb_ref)


def _mlp(x, mod, w1, w2, pg, pb):
    bsz, s, d = x.shape
    d_ff = w1.shape[1]
    ts = SEQ_TILE
    row = lambda v: v.reshape(1, -1)
    return pl.pallas_call(
        _mlp_kernel,
        grid=(bsz, s // ts),
        in_specs=[
            pl.BlockSpec((1, ts, d), lambda b, i: (b, i, 0)),
            pl.BlockSpec((1, 3, d), lambda b, i: (b, 0, 0)),
            _const_spec((d, d_ff), True),
            _const_spec((d_ff, d), True),
            _const_spec((1, d)),
            _const_spec((1, d)),
        ],
        out_specs=pl.BlockSpec((1, ts, d), lambda b, i: (b, i, 0)),
        out_shape=jax.ShapeDtypeStruct((bsz, s, d), F32),
        compiler_params=pltpu.CompilerParams(
            dimension_semantics=("arbitrary", "arbitrary"),
            vmem_limit_bytes=_vmem_limit(52 * 1024 * 1024)),
        name="sq_relu_mlp",
    )(x, mod, w1.astype(BF16), w2.astype(BF16), row(pg), row(pb))


def _qkv_kernel(x_ref, mod_ref, wq_ref, wk_ref, wvt_ref, q_ref, k_ref, vt_ref):
    mod = mod_ref[0]
    shift, scale = mod[0:1], mod[1:2]
    h = (x_ref[0] * (1.0 + scale) + shift).astype(BF16)
    q = jnp.dot(h, wq_ref[...], preferred_element_type=F32)
    q_ref[0] = (q * (HEAD_DIM ** -0.5 * LOG2_E)).astype(BF16)
    k_ref[0] = jnp.dot(h, wk_ref[...], preferred_element_type=F32).astype(BF16)
    vt = lax.dot_general(wvt_ref[...], h, (((1,), (1,)), ((), ())), preferred_element_type=F32)
    vt_ref[0, 0] = vt.astype(BF16)


def _qkv(x, mod, w_qkv):
    bsz, s, d = x.shape
    tk = ATTN_TILE
    dq = N_HEADS * 2 * HEAD_DIM
    dv = N_HEADS * V_DIM
    wq = w_qkv[:, :dq].astype(BF16)
    wk = w_qkv[:, dq:2 * dq].astype(BF16)
    wvt = w_qkv[:, 2 * dq:].T.astype(BF16)
    return pl.pallas_call(
        _qkv_kernel,
        grid=(bsz, s // tk),
        in_specs=[
            pl.BlockSpec((1, tk, d), lambda b, i: (b, i, 0)),
            pl.BlockSpec((1, 3, d), lambda b, i: (b, 0, 0)),
            _const_spec((d, dq), True),
            _const_spec((d, dq), True),
            _const_spec((dv, d), True),
        ],
        out_specs=[
            pl.BlockSpec((1, tk, dq), lambda b, i: (b, i, 0)),
            pl.BlockSpec((1, tk, dq), lambda b, i: (b, i, 0)),
            pl.BlockSpec((1, 1, dv, tk), lambda b, i: (b, i, 0, 0)),
        ],
        out_shape=[
            jax.ShapeDtypeStruct((bsz, s, dq), BF16),
            jax.ShapeDtypeStruct((bsz, s, dq), BF16),
            jax.ShapeDtypeStruct((bsz, s // tk, dv, tk), BF16),
        ],
        compiler_params=pltpu.CompilerParams(
            dimension_semantics=("arbitrary", "arbitrary"),
            vmem_limit_bytes=_vmem_limit(40 * 1024 * 1024)),
        name="qkv_proj",
    )(x, mod, wq, wk, wvt)


def _bias_tiles_kernel(rb_ref, o_ref):
    h = pl.program_id(0)
    t = o_ref.shape[2]
    far = rb_ref[h, REL_BUCKETS - 1]
    kk = lax.broadcasted_iota(jnp.int32, (t, t), 0)
    qq = lax.broadcasted_iota(jnp.int32, (t, t), 1)
    for dlt in range(o_ref.shape[1]):
        rel = dlt * t + qq - kk
        val = jnp.full((t, t), (rb_ref[h, 0] - far) * LOG2_E, F32)
        for bkt in range(1, REL_BUCKETS):
            val = jnp.where(rel >= T5_THRESHOLDS[bkt], (rb_ref[h, bkt] - far) * LOG2_E, val)
        o_ref[0, dlt] = jnp.where(rel >= 0, val, MASK_VALUE)


def _bias_tiles(rel_bias):
    t = ATTN_TILE
    n_near = -(-(T5_LAST_BUCKET_START + t - 1) // t)
    return pl.pallas_call(
        _bias_tiles_kernel,
        grid=(N_HEADS,),
        in_specs=[pl.BlockSpec(memory_space=pltpu.SMEM)],
        out_specs=pl.BlockSpec((1, n_near, t, t), lambda h: (h, 0, 0, 0)),
        out_shape=jax.ShapeDtypeStruct((N_HEADS, n_near, t, t), F32),
        compiler_params=pltpu.CompilerParams(dimension_semantics=("arbitrary",)),
        name="t5_bias_tiles",
    )(rel_bias.T)


_TAB_QI, _TAB_KJ = range(2)
_PIPE_DEPTH = 1
_UNROLL_CHOICES = (20, 18, 16, 14, 12)
ATTN_COL_BLOCK = 256
ATTN_STAGE_SKEW = 1


def _attn_schedule(pairs, inert):
    cols = [inert] * _PIPE_DEPTH + list(pairs) + [inert] * _PIPE_DEPTH
    n_ticks = len(cols) - _PIPE_DEPTH
    unroll = min(_UNROLL_CHOICES, key=lambda u: (-n_ticks % u, -u))
    cols += [inert] * (-n_ticks % unroll)
    return np.asarray(cols, np.int32).T, unroll


_NEAR_PERIOD = 2


def _near_distance(col):
    return (col + 1) % _NEAR_PERIOD


def _near_extent(dist, c, t, cb):
    if dist is None:
        return t, t
    q_lo = (c * cb) % t
    if dist == 0:
        return min(t, q_lo + cb), 0
    lo = dist * t + q_lo - T5_LAST_BUCKET_START + 1
    return t, min(t, max(0, lo // V7X_SUBLANES * V7X_SUBLANES))


def _attn_schedules(nq, n_near):
    assert n_near == _NEAR_PERIOD
    far = [(qi, kj) for qi in range(nq) for kj in range(qi - n_near + 1)]
    near = [(qi, qi - d) for qi in range(nq) for d in reversed(range(min(n_near, qi + 1)))]
    assert all(qi - kj == _near_distance(_PIPE_DEPTH + i) for i, (qi, kj) in enumerate(near))
    return _attn_schedule(far, (nq, 0)), _attn_schedule(near, (nq, 0))


def _attn_kernel(far_ref, near_ref, q_ref, k_ref, vt_ref, bias_ref, lam_ref, g_ref, o_ref,
                 s0_ref, s1_ref, mt0_ref, mt1_ref, qc_ref, m_ref, l_ref, acc_ref,
                 *, far_unroll, near_unroll):
    s_refs, mt_refs = (s0_ref, s1_ref), (mt0_ref, mt1_ref)
    n_cb, t, cb = s0_ref.shape
    nq = m_ref.shape[0] - 1

    for slot in range(2):
        s_refs[slot][...] = jnp.zeros(s_refs[slot].shape, F32)
        mt_refs[slot][...] = jnp.zeros(mt_refs[slot].shape, F32)
    m_ref[...] = jnp.full(m_ref.shape, MASK_VALUE, F32)
    l_ref[...] = jnp.zeros(l_ref.shape, F32)
    acc_ref[...] = jnp.zeros(acc_ref.shape, F32)

    def mask_q(qi, carry):
        q = q_ref[0, pl.ds(pl.multiple_of(qi * t, t), t), :]
        lane = lax.broadcasted_iota(jnp.int32, q.shape, 1)
        zero = jnp.zeros_like(q)
        qcat = jnp.concatenate([jnp.where(lane < HEAD_DIM, q, zero),
                                jnp.where(lane >= HEAD_DIM, q, zero)], axis=0)
        qc_ref[qi] = qcat.reshape(n_cb, cb, q.shape[1])
        return carry

    lax.fori_loop(0, nq, mask_q, 0)

    def q_tile_index(qi):
        return jnp.where(qi == nq, 0, qi)

    def tick(tab_ref, n, u, near):
        col_a, col_b = n + 1, n
        par, cur = u % 2, 1 - u % 2
        dist_a, dist_b = (_near_distance(u + 1), _near_distance(u)) if near else (None, None)

        qi_a = q_tile_index(tab_ref[_TAB_QI, col_a])
        qi_b = tab_ref[_TAB_QI, col_b]
        vt = vt_ref[0, tab_ref[_TAB_KJ, col_b]]
        k = k_ref[0, pl.ds(pl.multiple_of(tab_ref[_TAB_KJ, col_a] * t, t), t), :]

        def stage_a(c):
            keys_a, bias_lo = _near_extent(dist_a, c, t, cb)
            st = lax.dot_general(k[:keys_a], qc_ref[qi_a, c], (((1,), (1,)), ((), ())),
                                 preferred_element_type=F32)
            if bias_lo < keys_a:
                bias = bias_ref[0, dist_a, bias_lo:keys_a, pl.ds((c * cb) % t, cb)]
                st = jnp.concatenate([st[:bias_lo], st[bias_lo:] + bias], axis=0) if bias_lo else st + bias
            s_refs[par][c, 0:keys_a] = st
            mt_refs[par][c] = jnp.max(st, axis=0, keepdims=True)

        def stage_b(c):
            keys_b, _ = _near_extent(dist_b, c, t, cb)
            m_old = m_ref[qi_b, c]
            m_new = jnp.maximum(m_old, mt_refs[cur][c])
            m_ref[qi_b, c] = m_new
            p = jnp.exp2(s_refs[cur][c, 0:keys_b] - m_new)
            pv = jnp.dot(vt[:, :keys_b], p.astype(BF16), preferred_element_type=F32)
            alpha = jnp.exp2(m_old - m_new)
            acc_ref[qi_b, c] = alpha * acc_ref[qi_b, c] + pv
            l_ref[qi_b, c] = alpha * l_ref[qi_b, c] + jnp.sum(p, axis=0, keepdims=True)

        for c in range(n_cb + ATTN_STAGE_SKEW):
            if c < n_cb:
                stage_a(c)
            if c >= ATTN_STAGE_SKEW:
                stage_b(c - ATTN_STAGE_SKEW)

    def run(tab_ref, unroll, near):
        n_ticks = tab_ref.shape[1] - _PIPE_DEPTH
        assert n_ticks % unroll == 0 and unroll % _NEAR_PERIOD == 0

        def body(i, carry):
            for u in range(unroll):
                tick(tab_ref, unroll * i + u, u, near)
            return carry

        lax.fori_loop(0, n_ticks // unroll, body, 0)

    run(far_ref, far_unroll, False)
    run(near_ref, near_unroll, True)

    lam_v = lam_ref[...]
    lam = (jnp.exp(jnp.sum(lam_v[0:1] * lam_v[1:2], axis=-1, keepdims=True))
           - jnp.exp(jnp.sum(lam_v[2:3] * lam_v[3:4], axis=-1, keepdims=True)) + LAMBDA_INIT)

    def finish(qi, carry):
        def normalised(c):
            return acc_ref[qi, c] * (1.0 / l_ref[qi, c])
        half = n_cb // 2
        o = jnp.concatenate([normalised(c) - lam * normalised(half + c) for c in range(half)],
                            axis=1).T
        o = (o * lax.rsqrt(jnp.mean(o * o, axis=-1, keepdims=True) + LN_EPS)
             * g_ref[...] * (1.0 - LAMBDA_INIT))
        o_ref[0, pl.ds(pl.multiple_of(qi * t, t), t), :] = o.astype(BF16)
        return carry

    lax.fori_loop(0, nq, finish, 0)


def _attention(q, k, vt, bias, lam_vecs, g_sub):
    bsz, s, dq = q.shape
    t = ATTN_TILE
    nq = s // t
    n_near = bias.shape[1]
    hw = 2 * HEAD_DIM
    (far_tab, far_unroll), (near_tab, near_unroll) = _attn_schedules(nq, n_near)
    cb = ATTN_COL_BLOCK
    n_cb = 2 * t // cb
    return pl.pallas_call(
        functools.partial(_attn_kernel, far_unroll=far_unroll, near_unroll=near_unroll),
        grid=(bsz, N_HEADS),
        in_specs=[
            pl.BlockSpec(memory_space=pltpu.SMEM),
            pl.BlockSpec(memory_space=pltpu.SMEM),
            pl.BlockSpec((1, s, hw), lambda b, h: (b, 0, h)),
            pl.BlockSpec((1, s, hw), lambda b, h: (b, 0, h)),
            pl.BlockSpec((1, nq, V_DIM, t), lambda b, h: (b, 0, h, 0)),
            pl.BlockSpec((1, n_near, t, t), lambda b, h: (h, 0, 0, 0)),
            _const_spec((4, HEAD_DIM)),
            _const_spec((1, V_DIM)),
        ],
        out_specs=pl.BlockSpec((1, s, V_DIM), lambda b, h: (b, 0, h)),
        out_shape=jax.ShapeDtypeStruct((bsz, s, N_HEADS * V_DIM), BF16),
        scratch_shapes=[
            pltpu.VMEM((n_cb, t, cb), F32),
            pltpu.VMEM((n_cb, t, cb), F32),
            pltpu.VMEM((n_cb, 1, cb), F32),
            pltpu.VMEM((n_cb, 1, cb), F32),
            pltpu.VMEM((nq, n_cb, cb, hw), BF16),
            pltpu.VMEM((nq + 1, n_cb, 1, cb), F32),
            pltpu.VMEM((nq + 1, n_cb, 1, cb), F32),
            pltpu.VMEM((nq + 1, n_cb, V_DIM, cb), F32),
        ],
        compiler_params=pltpu.CompilerParams(
            dimension_semantics=("arbitrary", "arbitrary"),
            vmem_limit_bytes=_vmem_limit(52 * 1024 * 1024)),
        name="diff_attention",
    )(jnp.asarray(far_tab), jnp.asarray(near_tab), q, k, vt, bias, lam_vecs,
      g_sub.reshape(1, V_DIM))


def _out_proj_mlp_kernel(a_ref, x_ref, amod_ref, wo_ref, mg_ref, mb_ref,
                         mod_ref, w1_ref, w2_ref, pg_ref, pb_ref, o_ref):
    gate = amod_ref[0][2:3]
    y = jnp.dot(a_ref[0], wo_ref[...], preferred_element_type=F32)
    x = _layer_norm(ALPHA * x_ref[0] + gate * y, mg_ref[...], mb_ref[...])
    o_ref[0] = _mlp_sublayer(x, mod_ref[0], w1_ref, w2_ref, pg_ref, pb_ref)


def _out_proj_mlp(a, x, amod, wo, mg, mb, mod, w1, w2, pg, pb):
    bsz, s, d = x.shape
    da = a.shape[2]
    d_ff = w1.shape[1]
    ts = SEQ_TILE
    row = lambda v: v.reshape(1, -1)
    return pl.pallas_call(
        _out_proj_mlp_kernel,
        grid=(bsz, s // ts),
        in_specs=[
            pl.BlockSpec((1, ts, da), lambda b, i: (b, i, 0)),
            pl.BlockSpec((1, ts, d), lambda b, i: (b, i, 0)),
            pl.BlockSpec((1, 3, d), lambda b, i: (b, 0, 0)),
            _const_spec((da, d), True),
            _const_spec((1, d)),
            _const_spec((1, d)),
            pl.BlockSpec((1, 3, d), lambda b, i: (b, 0, 0)),
            _const_spec((d, d_ff), True),
            _const_spec((d_ff, d), True),
            _const_spec((1, d)),
            _const_spec((1, d)),
        ],
        out_specs=pl.BlockSpec((1, ts, d), lambda b, i: (b, i, 0)),
        out_shape=jax.ShapeDtypeStruct((bsz, s, d), F32),
        compiler_params=pltpu.CompilerParams(
            dimension_semantics=("arbitrary", "arbitrary"),
            vmem_limit_bytes=_vmem_limit(54 * 1024 * 1024)),
        name="out_proj_mlp",
    )(a, x, amod, wo.astype(BF16), row(mg), row(mb),
      mod, w1.astype(BF16), w2.astype(BF16), row(pg), row(pb))


def kernel(x, c, conv_mod_w, conv_mod_b, conv_pw1_w, conv_pw1_b, conv_dw_w, conv_dw_b, conv_norm_g, conv_norm_b, conv_pw2_w, conv_pw2_b, attn_mod_w, attn_mod_b, attn_qkv_w, attn_lam_q1, attn_lam_k1, attn_lam_q2, attn_lam_k2, attn_subln_g, attn_out_w, rel_bias, mlp_mod_w, mlp_mod_b, mlp_w1, mlp_w2, post_mix_g, post_mix_b, post_mlp_g, post_mlp_b):
    assert x.shape[1] % SEQ_TILE == 0 and x.shape[1] % ATTN_TILE == 0
    assert SEQ_TILE % CONV_ROWS == 0 and CONV_HALO >= CONV_WIDTH - 1
    conv_mod = _ada_mod(c, conv_mod_w, conv_mod_b)
    attn_mod = _ada_mod(c, attn_mod_w, attn_mod_b)
    mlp_mod = _ada_mod(c, mlp_mod_w, mlp_mod_b)

    x = _conv_mixer(x, conv_mod[0], conv_pw1_w[0], conv_pw1_b[0], conv_dw_w[0], conv_dw_b[0],
                    conv_norm_g[0], conv_norm_b[0], conv_pw2_w[0], conv_pw2_b[0],
                    post_mix_g[0], post_mix_b[0])
    x = _mlp(x, mlp_mod[0], mlp_w1[0], mlp_w2[0], post_mlp_g[0], post_mlp_b[0])

    q, k, vt = _qkv(x, attn_mod[0], attn_qkv_w[0])
    bias = _bias_tiles(rel_bias)
    lam_vecs = jnp.stack([attn_lam_q1[0], attn_lam_k1[0], attn_lam_q2[0], attn_lam_k2[0]])
    a = _attention(q, k, vt, bias, lam_vecs, attn_subln_g[0])
    return _out_proj_mlp(a, x, attn_mod[0], attn_out_w[0], post_mix_g[1], post_mix_b[1],
                         mlp_mod[1], mlp_w1[1], mlp_w2[1], post_mlp_g[1], post_mlp_b[1])
```

```python
import functools
import math

import jax
import jax.numpy as jnp
import numpy as np
from jax import lax
from jax.experimental import pallas as pl
from jax.experimental.pallas import tpu as pltpu

DEPTH = 2
CONV_WIDTH = 31
N_HEADS = 8
HEAD_DIM = 64
V_DIM = 2 * HEAD_DIM
REL_BUCKETS = 32
REL_MAX_DIST = 128
ALPHA = (2 * DEPTH) ** 0.25
LN_EPS = 1e-5
ATTN_LAYER = 1
LAMBDA_INIT = 0.8 - 0.6 * math.exp(-0.3 * ATTN_LAYER)
LOG2_E = math.log2(math.e)

V7X_SUBLANES = 8
V7X_LANES = 128
V7X_VMEM_BYTES = 64 * 1024 * 1024
V7X_VMEM_RESERVE_BYTES = 8 * 1024 * 1024

MASK_VALUE = -1e30
CONV_HALO = 32
CONV_ROWS = 512
SEQ_TILE = 512
ATTN_TILE = 512
FF_CHUNK = 1024

F32 = jnp.float32
BF16 = jnp.bfloat16


def _t5_thresholds():
    max_exact = REL_BUCKETS // 2
    buckets = []
    for n in range(2 * REL_MAX_DIST):
        if n < max_exact:
            buckets.append(n)
        else:
            v = math.log(n / max_exact) / math.log(REL_MAX_DIST / max_exact) * (REL_BUCKETS - max_exact)
            buckets.append(min(max_exact + int(v), REL_BUCKETS - 1))
    assert all(b1 >= b0 for b0, b1 in zip(buckets, buckets[1:]))
    assert buckets[-1] == REL_BUCKETS - 1
    return [buckets.index(b) for b in range(REL_BUCKETS)]


T5_THRESHOLDS = _t5_thresholds()
T5_LAST_BUCKET_START = T5_THRESHOLDS[REL_BUCKETS - 1]


def _vmem_limit(nbytes):
    return int(min(nbytes, V7X_VMEM_BYTES - V7X_VMEM_RESERVE_BYTES))


def _layer_norm(z, g, b):
    mu = jnp.mean(z, axis=-1, keepdims=True)
    zc = z - mu
    var = jnp.mean(zc * zc, axis=-1, keepdims=True)
    return zc * lax.rsqrt(var + LN_EPS) * g + b


def _const_spec(shape, single_buffer=False):
    nd = len(shape)
    kwargs = {"pipeline_mode": pl.Buffered(1)} if single_buffer else {}
    return pl.BlockSpec(shape, lambda *_: (0,) * nd, **kwargs)


def _ada_mod_kernel(c_ref, w_ref, b_ref, o_ref):
    c = c_ref[...]
    sc = c * jax.nn.sigmoid(c)
    o_ref[0] = jnp.dot(sc, w_ref[0], preferred_element_type=F32,
                       precision=lax.Precision.HIGHEST) + b_ref[0]


def _ada_mod(c, w, b):
    n, d, d3 = w.shape
    bsz = c.shape[0]
    nblk = d3 // d
    out = pl.pallas_call(
        _ada_mod_kernel,
        grid=(n, nblk),
        in_specs=[
            pl.BlockSpec((bsz, d), lambda i, j: (0, 0)),
            pl.BlockSpec((1, d, d), lambda i, j: (i, 0, j)),
            pl.BlockSpec((1, 1, d), lambda i, j: (i, 0, j)),
        ],
        out_specs=pl.BlockSpec((1, bsz, d), lambda i, j: (i, 0, j)),
        out_shape=jax.ShapeDtypeStruct((n, bsz, d3), F32),
        compiler_params=pltpu.CompilerParams(
            dimension_semantics=("arbitrary", "arbitrary"),
            vmem_limit_bytes=_vmem_limit(32 * 1024 * 1024)),
        name="ada_mod",
    )(c, w, b.reshape(n, 1, d3))
    return out.reshape(n, bsz, 3, d)


def _conv_mixer_kernel(x_ref, mod_ref, w1_ref, b1_ref, wdw_ref, bdw_ref, gcn_ref, bcn_ref,
                       w2_ref, b2_ref, pg_ref, pb_ref, o_ref, ext_ref, y_ref):
    ts, d = x_ref.shape[1], x_ref.shape[2]
    n_lb = ext_ref.shape[0]
    s_idx = pl.program_id(1)

    @pl.when(s_idx == 0)
    def _():
        ext_ref[:, 0:CONV_HALO, :] = jnp.zeros((n_lb, CONV_HALO, V7X_LANES), F32)

    mod = mod_ref[0]
    shift, scale, gate = mod[0:1], mod[1:2], mod[2:3]
    x = x_ref[0]
    h = (x * (1.0 + scale) + shift).astype(BF16)
    a = jnp.dot(h, w1_ref[...], preferred_element_type=F32) + b1_ref[...]
    u = a[:, :d] * jax.nn.sigmoid(a[:, d:])
    for lb in range(n_lb):
        ext_ref[lb, CONV_HALO:CONV_HALO + ts, :] = u[:, lb * V7X_LANES:(lb + 1) * V7X_LANES]

    off0 = CONV_HALO - (CONV_WIDTH - 1)

    n_grp = CONV_ROWS // V7X_SUBLANES

    def row_block(rb, carry):
        t0 = pl.multiple_of(rb * CONV_ROWS, CONV_ROWS)

        def lane_block(lb, carry2):
            w_all = wdw_ref[lb]
            acc = [jnp.broadcast_to(bdw_ref[lb], (V7X_SUBLANES, V7X_LANES))] * n_grp
            for r in range(V7X_SUBLANES):
                taps = [j for j in range(CONV_WIDTH) if (off0 + j) % V7X_SUBLANES == r]
                first = off0 + taps[0]
                n_load = n_grp + (taps[-1] - taps[0]) // V7X_SUBLANES
                groups = [ext_ref[lb, pl.ds(t0 + (first + V7X_SUBLANES * g), V7X_SUBLANES), :]
                          for g in range(n_load)]
                for j in taps:
                    w_row = jnp.broadcast_to(w_all[j:j + 1], (V7X_SUBLANES, V7X_LANES))
                    g0 = (off0 + j - first) // V7X_SUBLANES
                    acc = [acc[i] + w_row * groups[g0 + i] for i in range(n_grp)]
            y_ref[lb, pl.ds(t0, CONV_ROWS), :] = jnp.concatenate(acc, axis=0)
            return carry2

        lax.fori_loop(0, n_lb, lane_block, 0)
        return carry

    lax.fori_loop(0, ts // CONV_ROWS, row_block, 0)
    ext_ref[:, 0:CONV_HALO, :] = ext_ref[:, ts:ts + CONV_HALO, :]

    y = y_ref[...]
    mu = jnp.sum(jnp.sum(y, axis=0, keepdims=True), axis=2, keepdims=True) * (1.0 / d)
    yc = y - mu
    var = jnp.sum(jnp.sum(yc * yc, axis=0, keepdims=True), axis=2, keepdims=True) * (1.0 / d)
    y = yc * lax.rsqrt(var + LN_EPS) * gcn_ref[...] + bcn_ref[...]
    y = (y * jax.nn.sigmoid(y)).astype(BF16)
    v = jnp.concatenate([y[lb] for lb in range(n_lb)], axis=-1)
    y = jnp.dot(v, w2_ref[...], preferred_element_type=F32) + b2_ref[...]
    o_ref[0] = _layer_norm(ALPHA * x_ref[0] + gate * y, pg_ref[...], pb_ref[...])


def _conv_mixer(x, mod, w1, b1, wdw, bdw, gcn, bcn, w2, b2, pg, pb):
    bsz, s, d = x.shape
    ts = SEQ_TILE
    n_lb = d // V7X_LANES
    row = lambda v: v.reshape(1, -1)
    slab = lambda v: v.reshape(-1, n_lb, V7X_LANES).transpose(1, 0, 2)
    return pl.pallas_call(
        _conv_mixer_kernel,
        grid=(bsz, s // ts),
        in_specs=[
            pl.BlockSpec((1, ts, d), lambda b, i: (b, i, 0)),
            pl.BlockSpec((1, 3, d), lambda b, i: (b, 0, 0)),
            _const_spec((d, 2 * d), True),
            _const_spec((1, 2 * d)),
            _const_spec((n_lb, CONV_WIDTH, V7X_LANES)),
            _const_spec((n_lb, 1, V7X_LANES)),
            _const_spec((n_lb, 1, V7X_LANES)),
            _const_spec((n_lb, 1, V7X_LANES)),
            _const_spec((d, d), True),
            _const_spec((1, d)),
            _const_spec((1, d)),
            _const_spec((1, d)),
        ],
        out_specs=pl.BlockSpec((1, ts, d), lambda b, i: (b, i, 0)),
        out_shape=jax.ShapeDtypeStruct((bsz, s, d), F32),
        scratch_shapes=[
            pltpu.VMEM((n_lb, ts + CONV_HALO, V7X_LANES), F32),
            pltpu.VMEM((n_lb, ts, V7X_LANES), F32),
        ],
        compiler_params=pltpu.CompilerParams(
            dimension_semantics=("arbitrary", "arbitrary"),
            vmem_limit_bytes=_vmem_limit(48 * 1024 * 1024)),
        name="conv_mixer",
    )(x, mod, w1.astype(BF16), row(b1), slab(wdw), slab(bdw), slab(gcn), slab(bcn),
      w2.astype(BF16), row(b2), row(pg), row(pb))


def _mlp_sublayer(x, mod, w1_ref, w2_ref, pg_ref, pb_ref):
    d_ff = w1_ref.shape[1]
    shift, scale, gate = mod[0:1], mod[1:2], mod[2:3]
    h = (x * (1.0 + scale) + shift).astype(BF16)
    y = None
    for c0 in range(0, d_ff, FF_CHUNK):
        a = jnp.dot(h, w1_ref[:, c0:c0 + FF_CHUNK], preferred_element_type=F32)
        a = jnp.maximum(a, 0.0)
        part = jnp.dot((a * a).astype(BF16), w2_ref[c0:c0 + FF_CHUNK, :], preferred_element_type=F32)
        y = part if y is None else y + part
    return _layer_norm(ALPHA * x + gate * y, pg_ref[...], pb_ref[...])


def _mlp_kernel(x_ref, mod_ref, w1_ref, w2_ref, pg_ref, pb_ref, o_ref):
    o_ref[0] = _mlp_sublayer(x_ref[0], mod_ref[0], w1_ref, w2_ref, pg_ref, pb_ref)


def _mlp(x, mod, w1, w2, pg, pb):
    bsz, s, d = x.shape
    d_ff = w1.shape[1]
    ts = SEQ_TILE
    row = lambda v: v.reshape(1, -1)
    return pl.pallas_call(
        _mlp_kernel,
        grid=(bsz, s // ts),
        in_specs=[
            pl.BlockSpec((1, ts, d), lambda b, i: (b, i, 0)),
            pl.BlockSpec((1, 3, d), lambda b, i: (b, 0, 0)),
            _const_spec((d, d_ff), True),
            _const_spec((d_ff, d), True),
            _const_spec((1, d)),
            _const_spec((1, d)),
        ],
        out_specs=pl.BlockSpec((1, ts, d), lambda b, i: (b, i, 0)),
        out_shape=jax.ShapeDtypeStruct((bsz, s, d), F32),
        compiler_params=pltpu.CompilerParams(
            dimension_semantics=("arbitrary", "arbitrary"),
            vmem_limit_bytes=_vmem_limit(52 * 1024 * 1024)),
        name="sq_relu_mlp",
    )(x, mod, w1.astype(BF16), w2.astype(BF16), row(pg), row(pb))


def _qkv_kernel(x_ref, mod_ref, wq_ref, wk_ref, wvt_ref, q_ref, k_ref, vt_ref):
    mod = mod_ref[0]
    shift, scale = mod[0:1], mod[1:2]
    h = (x_ref[0] * (1.0 + scale) + shift).astype(BF16)
    q = jnp.dot(h, wq_ref[...], preferred_element_type=F32)
    q_ref[0] = (q * (HEAD_DIM ** -0.5 * LOG2_E)).astype(BF16)
    k_ref[0] = jnp.dot(h, wk_ref[...], preferred_element_type=F32).astype(BF16)
    vt = lax.dot_general(wvt_ref[...], h, (((1,), (1,)), ((), ())), preferred_element_type=F32)
    vt_ref[0, 0] = vt.astype(BF16)


def _qkv(x, mod, w_qkv):
    bsz, s, d = x.shape
    tk = ATTN_TILE
    dq = N_HEADS * 2 * HEAD_DIM
    dv = N_HEADS * V_DIM
    wq = w_qkv[:, :dq].astype(BF16)
    wk = w_qkv[:, dq:2 * dq].astype(BF16)
    wvt = w_qkv[:, 2 * dq:].T.astype(BF16)
    return pl.pallas_call(
        _qkv_kernel,
        grid=(bsz, s // tk),
        in_specs=[
            pl.BlockSpec((1, tk, d), lambda b, i: (b, i, 0)),
            pl.BlockSpec((1, 3, d), lambda b, i: (b, 0, 0)),
            _const_spec((d, dq), True),
            _const_spec((d, dq), True),
            _const_spec((dv, d), True),
        ],
        out_specs=[
            pl.BlockSpec((1, tk, dq), lambda b, i: (b, i, 0)),
            pl.BlockSpec((1, tk, dq), lambda b, i: (b, i, 0)),
            pl.BlockSpec((1, 1, dv, tk), lambda b, i: (b, i, 0, 0)),
        ],
        out_shape=[
            jax.ShapeDtypeStruct((bsz, s, dq), BF16),
            jax.ShapeDtypeStruct((bsz, s, dq), BF16),
            jax.ShapeDtypeStruct((bsz, s // tk, dv, tk), BF16),
        ],
        compiler_params=pltpu.CompilerParams(
            dimension_semantics=("arbitrary", "arbitrary"),
            vmem_limit_bytes=_vmem_limit(40 * 1024 * 1024)),
        name="qkv_proj",
    )(x, mod, wq, wk, wvt)


def _bias_tiles_kernel(rb_ref, o_ref):
    h = pl.program_id(0)
    t = o_ref.shape[2]
    far = rb_ref[h, REL_BUCKETS - 1]
    kk = lax.broadcasted_iota(jnp.int32, (t, t), 0)
    qq = lax.broadcasted_iota(jnp.int32, (t, t), 1)
    for dlt in range(o_ref.shape[1]):
        rel = dlt * t + qq - kk
        val = jnp.full((t, t), (rb_ref[h, 0] - far) * LOG2_E, F32)
        for bkt in range(1, REL_BUCKETS):
            val = jnp.where(rel >= T5_THRESHOLDS[bkt], (rb_ref[h, bkt] - far) * LOG2_E, val)
        o_ref[0, dlt] = jnp.where(rel >= 0, val, MASK_VALUE)


def _bias_tiles(rel_bias):
    t = ATTN_TILE
    n_near = -(-(T5_LAST_BUCKET_START + t - 1) // t)
    return pl.pallas_call(
        _bias_tiles_kernel,
        grid=(N_HEADS,),
        in_specs=[pl.BlockSpec(memory_space=pltpu.SMEM)],
        out_specs=pl.BlockSpec((1, n_near, t, t), lambda h: (h, 0, 0, 0)),
        out_shape=jax.ShapeDtypeStruct((N_HEADS, n_near, t, t), F32),
        compiler_params=pltpu.CompilerParams(dimension_semantics=("arbitrary",)),
        name="t5_bias_tiles",
    )(rel_bias.T)


_TAB_QI, _TAB_KJ = range(2)
_PIPE_DEPTH = 1
_UNROLL_CHOICES = (20, 18, 16, 14, 12)
ATTN_COL_BLOCK = 256
ATTN_STAGE_SKEW = 1


def _attn_schedule(pairs, inert):
    cols = [inert] * _PIPE_DEPTH + list(pairs) + [inert] * _PIPE_DEPTH
    n_ticks = len(cols) - _PIPE_DEPTH
    unroll = min(_UNROLL_CHOICES, key=lambda u: (-n_ticks % u, -u))
    cols += [inert] * (-n_ticks % unroll)
    return np.asarray(cols, np.int32).T, unroll


_NEAR_PERIOD = 2


def _near_distance(col):
    return (col + 1) % _NEAR_PERIOD


def _near_extent(dist, c, t, cb):
    if dist is None:
        return t, t
    q_lo = (c * cb) % t
    if dist == 0:
        return min(t, q_lo + cb), 0
    lo = dist * t + q_lo - T5_LAST_BUCKET_START + 1
    return t, min(t, max(0, lo // V7X_SUBLANES * V7X_SUBLANES))


def _attn_schedules(nq, n_near):
    assert n_near == _NEAR_PERIOD
    far = [(qi, kj) for qi in range(nq) for kj in range(qi - n_near + 1)]
    near = [(qi, qi - d) for qi in range(nq) for d in reversed(range(min(n_near, qi + 1)))]
    assert all(qi - kj == _near_distance(_PIPE_DEPTH + i) for i, (qi, kj) in enumerate(near))
    near_tab, near_unroll = _attn_schedule(near, (nq, 0))
    visited = range(near_tab.shape[1] - _PIPE_DEPTH)
    assert all(near_tab[_TAB_QI, j] < nq for j in visited if _near_distance(j) == 0)
    return _attn_schedule(far, (nq, 0)), (near_tab, near_unroll)


def _attn_kernel(far_ref, near_ref, q_ref, k_ref, vt_ref, bias_ref, lam_ref, g_ref, o_ref,
                 s0_ref, s1_ref, mt0_ref, mt1_ref, qc_ref, m_ref, l_ref, acc_ref,
                 *, far_unroll, near_unroll):
    s_refs, mt_refs = (s0_ref, s1_ref), (mt0_ref, mt1_ref)
    n_cb, t, cb = s0_ref.shape
    nq = m_ref.shape[0] - 1

    s_refs[1][...] = jnp.zeros(s_refs[1].shape, F32)
    mt_refs[1][...] = jnp.zeros(mt_refs[1].shape, F32)
    m_ref[...] = jnp.full(m_ref.shape, MASK_VALUE, F32)
    l_ref[...] = jnp.zeros(l_ref.shape, F32)
    acc_ref[...] = jnp.zeros(acc_ref.shape, F32)

    def mask_q(qi, carry):
        q = q_ref[0, pl.ds(pl.multiple_of(qi * t, t), t), :]
        lane = lax.broadcasted_iota(jnp.int32, q.shape, 1)
        zero = jnp.zeros_like(q)
        qcat = jnp.concatenate([jnp.where(lane < HEAD_DIM, q, zero),
                                jnp.where(lane >= HEAD_DIM, q, zero)], axis=0)
        qc_ref[qi] = qcat.reshape(n_cb, cb, q.shape[1])
        return carry

    lax.fori_loop(0, nq, mask_q, 0, unroll=4)

    def q_tile_index(qi):
        return jnp.where(qi == nq, 0, qi)

    lam_v = lam_ref[...]
    lam = (jnp.exp(jnp.sum(lam_v[0:1] * lam_v[1:2], axis=-1, keepdims=True))
           - jnp.exp(jnp.sum(lam_v[2:3] * lam_v[3:4], axis=-1, keepdims=True)) + LAMBDA_INIT)

    def finish(qi):
        def normalised(c):
            return acc_ref[qi, c] * (1.0 / l_ref[qi, c])
        half = n_cb // 2
        o = jnp.concatenate([normalised(c) - lam * normalised(half + c) for c in range(half)],
                            axis=1).T
        o = (o * lax.rsqrt(jnp.mean(o * o, axis=-1, keepdims=True) + LN_EPS)
             * g_ref[...] * (1.0 - LAMBDA_INIT))
        o_ref[0, pl.ds(pl.multiple_of(qi * t, t), t), :] = o.astype(BF16)

    def tick(tab_ref, n, u, near):
        col_a, col_b = n + 1, n
        par, cur = u % 2, 1 - u % 2
        dist_a, dist_b = (_near_distance(u + 1), _near_distance(u)) if near else (None, None)

        qi_a = q_tile_index(tab_ref[_TAB_QI, col_a])
        qi_b = tab_ref[_TAB_QI, col_b]
        vt = vt_ref[0, tab_ref[_TAB_KJ, col_b]]
        k = k_ref[0, pl.ds(pl.multiple_of(tab_ref[_TAB_KJ, col_a] * t, t), t), :]

        def stage_a(c):
            keys_a, bias_lo = _near_extent(dist_a, c, t, cb)
            st = lax.dot_general(k[:keys_a], qc_ref[qi_a, c], (((1,), (1,)), ((), ())),
                                 preferred_element_type=F32)
            if bias_lo < keys_a:
                bias = bias_ref[0, dist_a, bias_lo:keys_a, pl.ds((c * cb) % t, cb)]
                st = jnp.concatenate([st[:bias_lo], st[bias_lo:] + bias], axis=0) if bias_lo else st + bias
            s_refs[par][c, 0:keys_a] = st
            mt_refs[par][c] = jnp.max(st, axis=0, keepdims=True)

        def stage_b(c):
            keys_b, _ = _near_extent(dist_b, c, t, cb)
            m_old = m_ref[qi_b, c]
            m_new = jnp.maximum(m_old, mt_refs[cur][c])
            m_ref[qi_b, c] = m_new
            p = jnp.exp2(s_refs[cur][c, 0:keys_b] - m_new)
            pv = jnp.dot(vt[:, :keys_b], p.astype(BF16), preferred_element_type=F32)
            alpha = jnp.exp2(m_old - m_new)
            acc_ref[qi_b, c] = alpha * acc_ref[qi_b, c] + pv
            l_ref[qi_b, c] = alpha * l_ref[qi_b, c] + jnp.sum(p, axis=0, keepdims=True)

        for c in range(n_cb + ATTN_STAGE_SKEW):
            if c < n_cb:
                stage_a(c)
            if c >= ATTN_STAGE_SKEW:
                stage_b(c - ATTN_STAGE_SKEW)
        if dist_b == 0:
            finish(qi_b)

    def run(tab_ref, unroll, near):
        n_ticks = tab_ref.shape[1] - _PIPE_DEPTH
        assert n_ticks % unroll == 0 and unroll % _NEAR_PERIOD == 0

        def body(i, carry):
            for u in range(unroll):
                tick(tab_ref, unroll * i + u, u, near)
            return carry

        lax.fori_loop(0, n_ticks // unroll, body, 0)

    run(far_ref, far_unroll, False)
    run(near_ref, near_unroll, True)


def _attention(q, k, vt, bias, lam_vecs, g_sub):
    bsz, s, dq = q.shape
    t = ATTN_TILE
    nq = s // t
    n_near = bias.shape[1]
    hw = 2 * HEAD_DIM
    (far_tab, far_unroll), (near_tab, near_unroll) = _attn_schedules(nq, n_near)
    cb = ATTN_COL_BLOCK
    n_cb = 2 * t // cb
    return pl.pallas_call(
        functools.partial(_attn_kernel, far_unroll=far_unroll, near_unroll=near_unroll),
        grid=(bsz, N_HEADS),
        in_specs=[
            pl.BlockSpec(memory_space=pltpu.SMEM),
            pl.BlockSpec(memory_space=pltpu.SMEM),
            pl.BlockSpec((1, s, hw), lambda b, h: (b, 0, h)),
            pl.BlockSpec((1, s, hw), lambda b, h: (b, 0, h)),
            pl.BlockSpec((1, nq, V_DIM, t), lambda b, h: (b, 0, h, 0)),
            pl.BlockSpec((1, n_near, t, t), lambda b, h: (h, 0, 0, 0)),
            _const_spec((4, HEAD_DIM)),
            _const_spec((1, V_DIM)),
        ],
        out_specs=pl.BlockSpec((1, s, V_DIM), lambda b, h: (b, 0, h)),
        out_shape=jax.ShapeDtypeStruct((bsz, s, N_HEADS * V_DIM), BF16),
        scratch_shapes=[
            pltpu.VMEM((n_cb, t, cb), F32),
            pltpu.VMEM((n_cb, t, cb), F32),
            pltpu.VMEM((n_cb, 1, cb), F32),
            pltpu.VMEM((n_cb, 1, cb), F32),
            pltpu.VMEM((nq, n_cb, cb, hw), BF16),
            pltpu.VMEM((nq + 1, n_cb, 1, cb), F32),
            pltpu.VMEM((nq + 1, n_cb, 1, cb), F32),
            pltpu.VMEM((nq + 1, n_cb, V_DIM, cb), F32),
        ],
        compiler_params=pltpu.CompilerParams(
            dimension_semantics=("arbitrary", "arbitrary"),
            vmem_limit_bytes=_vmem_limit(52 * 1024 * 1024)),
        name="diff_attention",
    )(jnp.asarray(far_tab), jnp.asarray(near_tab), q, k, vt, bias, lam_vecs,
      g_sub.reshape(1, V_DIM))


def _out_proj_mlp_kernel(a_ref, x_ref, amod_ref, wo_ref, mg_ref, mb_ref,
                         mod_ref, w1_ref, w2_ref, pg_ref, pb_ref, o_ref):
    gate = amod_ref[0][2:3]
    y = jnp.dot(a_ref[0], wo_ref[...], preferred_element_type=F32)
    x = _layer_norm(ALPHA * x_ref[0] + gate * y, mg_ref[...], mb_ref[...])
    o_ref[0] = _mlp_sublayer(x, mod_ref[0], w1_ref, w2_ref, pg_ref, pb_ref)


def _out_proj_mlp(a, x, amod, wo, mg, mb, mod, w1, w2, pg, pb):
    bsz, s, d = x.shape
    da = a.shape[2]
    d_ff = w1.shape[1]
    ts = SEQ_TILE
    row = lambda v: v.reshape(1, -1)
    return pl.pallas_call(
        _out_proj_mlp_kernel,
        grid=(bsz, s // ts),
        in_specs=[
            pl.BlockSpec((1, ts, da), lambda b, i: (b, i, 0)),
            pl.BlockSpec((1, ts, d), lambda b, i: (b, i, 0)),
            pl.BlockSpec((1, 3, d), lambda b, i: (b, 0, 0)),
            _const_spec((da, d), True),
            _const_spec((1, d)),
            _const_spec((1, d)),
            pl.BlockSpec((1, 3, d), lambda b, i: (b, 0, 0)),
            _const_spec((d, d_ff), True),
            _const_spec((d_ff, d), True),
            _const_spec((1, d)),
            _const_spec((1, d)),
        ],
        out_specs=pl.BlockSpec((1, ts, d), lambda b, i: (b, i, 0)),
        out_shape=jax.ShapeDtypeStruct((bsz, s, d), F32),
        compiler_params=pltpu.CompilerParams(
            dimension_semantics=("arbitrary", "arbitrary"),
            vmem_limit_bytes=_vmem_limit(54 * 1024 * 1024)),
        name="out_proj_mlp",
    )(a, x, amod, wo.astype(BF16), row(mg), row(mb),
      mod, w1.astype(BF16), w2.astype(BF16), row(pg), row(pb))


def kernel(x, c, conv_mod_w, conv_mod_b, conv_pw1_w, conv_pw1_b, conv_dw_w, conv_dw_b, conv_norm_g, conv_norm_b, conv_pw2_w, conv_pw2_b, attn_mod_w, attn_mod_b, attn_qkv_w, attn_lam_q1, attn_lam_k1, attn_lam_q2, attn_lam_k2, attn_subln_g, attn_out_w, rel_bias, mlp_mod_w, mlp_mod_b, mlp_w1, mlp_w2, post_mix_g, post_mix_b, post_mlp_g, post_mlp_b):
    assert x.shape[1] % SEQ_TILE == 0 and x.shape[1] % ATTN_TILE == 0
    assert SEQ_TILE % CONV_ROWS == 0 and CONV_HALO >= CONV_WIDTH - 1
    conv_mod = _ada_mod(c, conv_mod_w, conv_mod_b)
    attn_mod = _ada_mod(c, attn_mod_w, attn_mod_b)
    mlp_mod = _ada_mod(c, mlp_mod_w, mlp_mod_b)

    x = _conv_mixer(x, conv_mod[0], conv_pw1_w[0], conv_pw1_b[0], conv_dw_w[0], conv_dw_b[0],
                    conv_norm_g[0], conv_norm_b[0], conv_pw2_w[0], conv_pw2_b[0],
                    post_mix_g[0], post_mix_b[0])
    x = _mlp(x, mlp_mod[0], mlp_w1[0], mlp_w2[0], post_mlp_g[0], post_mlp_b[0])

    q, k, vt = _qkv(x, attn_mod[0], attn_qkv_w[0])
    bias = _bias_tiles(rel_bias)
    lam_vecs = jnp.stack([attn_lam_q1[0], attn_lam_k1[0], attn_lam_q2[0], attn_lam_k2[0]])
    a = _attention(q, k, vt, bias, lam_vecs, attn_subln_g[0])
    return _out_proj_mlp(a, x, attn_mod[0], attn_out_w[0], post_mix_g[1], post_mix_b[1],
                         mlp_mod[1], mlp_w1[1], mlp_w2[1], post_mlp_g[1], post_mlp_b[1])
```

```python
import functools
import math

import jax
import jax.numpy as jnp
import numpy as np
from jax import lax
from jax.experimental import pallas as pl
from jax.experimental.pallas import tpu as pltpu

DEPTH = 2
CONV_WIDTH = 31
N_HEADS = 8
HEAD_DIM = 64
V_DIM = 2 * HEAD_DIM
REL_BUCKETS = 32
REL_MAX_DIST = 128
ALPHA = (2 * DEPTH) ** 0.25
LN_EPS = 1e-5
ATTN_LAYER = 1
LAMBDA_INIT = 0.8 - 0.6 * math.exp(-0.3 * ATTN_LAYER)
LOG2_E = math.log2(math.e)

V7X_SUBLANES = 8
V7X_LANES = 128
V7X_VMEM_BYTES = 64 * 1024 * 1024
V7X_VMEM_RESERVE_BYTES = 8 * 1024 * 1024

MASK_VALUE = -1e30
CONV_HALO = 32
CONV_ROWS = 512
SEQ_TILE = 1024
CONV_TILE = 1024
ATTN_TILE = 512
FF_CHUNK = 1024

F32 = jnp.float32
BF16 = jnp.bfloat16


def _t5_thresholds():
    max_exact = REL_BUCKETS // 2
    buckets = []
    for n in range(2 * REL_MAX_DIST):
        if n < max_exact:
            buckets.append(n)
        else:
            v = math.log(n / max_exact) / math.log(REL_MAX_DIST / max_exact) * (REL_BUCKETS - max_exact)
            buckets.append(min(max_exact + int(v), REL_BUCKETS - 1))
    assert all(b1 >= b0 for b0, b1 in zip(buckets, buckets[1:]))
    assert buckets[-1] == REL_BUCKETS - 1
    return [buckets.index(b) for b in range(REL_BUCKETS)]


T5_THRESHOLDS = _t5_thresholds()
T5_LAST_BUCKET_START = T5_THRESHOLDS[REL_BUCKETS - 1]


def _vmem_limit(nbytes):
    return int(min(nbytes, V7X_VMEM_BYTES - V7X_VMEM_RESERVE_BYTES))


def _layer_norm(z, g, b):
    mu = jnp.mean(z, axis=-1, keepdims=True)
    zc = z - mu
    var = jnp.mean(zc * zc, axis=-1, keepdims=True)
    return zc * lax.rsqrt(var + LN_EPS) * g + b


def _const_spec(shape, single_buffer=False):
    nd = len(shape)
    kwargs = {"pipeline_mode": pl.Buffered(1)} if single_buffer else {}
    return pl.BlockSpec(shape, lambda *_: (0,) * nd, **kwargs)


def _ada_mod_kernel(c_ref, w_ref, b_ref, o_ref):
    c = c_ref[...]
    sc = c * jax.nn.sigmoid(c)
    o_ref[0] = jnp.dot(sc, w_ref[0], preferred_element_type=F32,
                       precision=lax.Precision.HIGHEST) + b_ref[0]


def _ada_mod(c, w, b):
    n, d, d3 = w.shape
    bsz = c.shape[0]
    nblk = d3 // d
    out = pl.pallas_call(
        _ada_mod_kernel,
        grid=(n, nblk),
        in_specs=[
            pl.BlockSpec((bsz, d), lambda i, j: (0, 0)),
            pl.BlockSpec((1, d, d), lambda i, j: (i, 0, j)),
            pl.BlockSpec((1, 1, d), lambda i, j: (i, 0, j)),
        ],
        out_specs=pl.BlockSpec((1, bsz, d), lambda i, j: (i, 0, j)),
        out_shape=jax.ShapeDtypeStruct((n, bsz, d3), F32),
        compiler_params=pltpu.CompilerParams(
            dimension_semantics=("arbitrary", "arbitrary"),
            vmem_limit_bytes=_vmem_limit(32 * 1024 * 1024)),
        name="ada_mod",
    )(c, w, b.reshape(n, 1, d3))
    return out.reshape(n, bsz, 3, d)


def _conv_mixer_kernel(x_ref, mod_ref, w1_ref, b1_ref, wdw_ref, bdw_ref, gcn_ref, bcn_ref,
                       w2_ref, b2_ref, pg_ref, pb_ref, o_ref, ext_ref, y_ref):
    ts, d = x_ref.shape[1], x_ref.shape[2]
    n_lb = ext_ref.shape[0]
    s_idx = pl.program_id(1)

    @pl.when(s_idx == 0)
    def _():
        ext_ref[:, 0:CONV_HALO, :] = jnp.zeros((n_lb, CONV_HALO, V7X_LANES), F32)

    mod = mod_ref[0]
    shift, scale, gate = mod[0:1], mod[1:2], mod[2:3]
    x = x_ref[0]
    h = (x * (1.0 + scale) + shift).astype(BF16)
    a = jnp.dot(h, w1_ref[...], preferred_element_type=F32) + b1_ref[...]
    u = a[:, :d] * jax.nn.sigmoid(a[:, d:])
    for lb in range(n_lb):
        ext_ref[lb, CONV_HALO:CONV_HALO + ts, :] = u[:, lb * V7X_LANES:(lb + 1) * V7X_LANES]

    off0 = CONV_HALO - (CONV_WIDTH - 1)

    n_grp = CONV_ROWS // V7X_SUBLANES

    def row_block(rb, carry):
        t0 = pl.multiple_of(rb * CONV_ROWS, CONV_ROWS)

        def lane_block(lb, carry2):
            w_all = wdw_ref[lb]
            acc = [jnp.broadcast_to(bdw_ref[lb], (V7X_SUBLANES, V7X_LANES))] * n_grp
            for r in range(V7X_SUBLANES):
                taps = [j for j in range(CONV_WIDTH) if (off0 + j) % V7X_SUBLANES == r]
                first = off0 + taps[0]
                n_load = n_grp + (taps[-1] - taps[0]) // V7X_SUBLANES
                groups = [ext_ref[lb, pl.ds(t0 + (first + V7X_SUBLANES * g), V7X_SUBLANES), :]
                          for g in range(n_load)]
                for j in taps:
                    w_row = jnp.broadcast_to(w_all[j:j + 1], (V7X_SUBLANES, V7X_LANES))
                    g0 = (off0 + j - first) // V7X_SUBLANES
                    acc = [acc[i] + w_row * groups[g0 + i] for i in range(n_grp)]
            y_ref[lb, pl.ds(t0, CONV_ROWS), :] = jnp.concatenate(acc, axis=0)
            return carry2

        lax.fori_loop(0, n_lb, lane_block, 0)
        return carry

    lax.fori_loop(0, ts // CONV_ROWS, row_block, 0)
    ext_ref[:, 0:CONV_HALO, :] = ext_ref[:, ts:ts + CONV_HALO, :]

    y = y_ref[...]
    mu = jnp.sum(jnp.sum(y, axis=0, keepdims=True), axis=2, keepdims=True) * (1.0 / d)
    yc = y - mu
    var = jnp.sum(jnp.sum(yc * yc, axis=0, keepdims=True), axis=2, keepdims=True) * (1.0 / d)
    y = yc * lax.rsqrt(var + LN_EPS) * gcn_ref[...] + bcn_ref[...]
    y = (y * jax.nn.sigmoid(y)).astype(BF16)
    v = jnp.concatenate([y[lb] for lb in range(n_lb)], axis=-1)
    y = jnp.dot(v, w2_ref[...], preferred_element_type=F32) + b2_ref[...]
    o_ref[0] = _layer_norm(ALPHA * x_ref[0] + gate * y, pg_ref[...], pb_ref[...])


def _conv_mixer(x, mod, w1, b1, wdw, bdw, gcn, bcn, w2, b2, pg, pb):
    bsz, s, d = x.shape
    ts = CONV_TILE
    n_lb = d // V7X_LANES
    row = lambda v: v.reshape(1, -1)
    slab = lambda v: v.reshape(-1, n_lb, V7X_LANES).transpose(1, 0, 2)
    return pl.pallas_call(
        _conv_mixer_kernel,
        grid=(bsz, s // ts),
        in_specs=[
            pl.BlockSpec((1, ts, d), lambda b, i: (b, i, 0)),
            pl.BlockSpec((1, 3, d), lambda b, i: (b, 0, 0)),
            _const_spec((d, 2 * d), True),
            _const_spec((1, 2 * d)),
            _const_spec((n_lb, CONV_WIDTH, V7X_LANES)),
            _const_spec((n_lb, 1, V7X_LANES)),
            _const_spec((n_lb, 1, V7X_LANES)),
            _const_spec((n_lb, 1, V7X_LANES)),
            _const_spec((d, d), True),
            _const_spec((1, d)),
            _const_spec((1, d)),
            _const_spec((1, d)),
        ],
        out_specs=pl.BlockSpec((1, ts, d), lambda b, i: (b, i, 0)),
        out_shape=jax.ShapeDtypeStruct((bsz, s, d), F32),
        scratch_shapes=[
            pltpu.VMEM((n_lb, ts + CONV_HALO, V7X_LANES), F32),
            pltpu.VMEM((n_lb, ts, V7X_LANES), F32),
        ],
        compiler_params=pltpu.CompilerParams(
            dimension_semantics=("arbitrary", "arbitrary"),
            vmem_limit_bytes=_vmem_limit(56 * 1024 * 1024)),
        name="conv_mixer",
    )(x, mod, w1.astype(BF16), row(b1), slab(wdw), slab(bdw), slab(gcn), slab(bcn),
      w2.astype(BF16), row(b2), row(pg), row(pb))


def _mlp_sublayer(x, mod, w1_ref, w2_ref, pg_ref, pb_ref):
    d_ff = w1_ref.shape[1]
    shift, scale, gate = mod[0:1], mod[1:2], mod[2:3]
    h = (x * (1.0 + scale) + shift).astype(BF16)
    y = None
    for c0 in range(0, d_ff, FF_CHUNK):
        a = jnp.dot(h, w1_ref[:, c0:c0 + FF_CHUNK], preferred_element_type=F32)
        a = jnp.maximum(a, 0.0)
        part = jnp.dot((a * a).astype(BF16), w2_ref[c0:c0 + FF_CHUNK, :], preferred_element_type=F32)
        y = part if y is None else y + part
    return _layer_norm(ALPHA * x + gate * y, pg_ref[...], pb_ref[...])


def _mlp_kernel(x_ref, mod_ref, w1_ref, w2_ref, pg_ref, pb_ref, o_ref):
    o_ref[0] = _mlp_sublayer(x_ref[0], mod_ref[0], w1_ref, w2_ref, pg_ref, pb_ref)


def _mlp(x, mod, w1, w2, pg, pb):
    bsz, s, d = x.shape
    d_ff = w1.shape[1]
    ts = SEQ_TILE
    row = lambda v: v.reshape(1, -1)
    return pl.pallas_call(
        _mlp_kernel,
        grid=(bsz, s // ts),
        in_specs=[
            pl.BlockSpec((1, ts, d), lambda b, i: (b, i, 0)),
            pl.BlockSpec((1, 3, d), lambda b, i: (b, 0, 0)),
            _const_spec((d, d_ff), True),
            _const_spec((d_ff, d), True),
            _const_spec((1, d)),
            _const_spec((1, d)),
        ],
        out_specs=pl.BlockSpec((1, ts, d), lambda b, i: (b, i, 0)),
        out_shape=jax.ShapeDtypeStruct((bsz, s, d), F32),
        compiler_params=pltpu.CompilerParams(
            dimension_semantics=("arbitrary", "arbitrary"),
            vmem_limit_bytes=_vmem_limit(52 * 1024 * 1024)),
        name="sq_relu_mlp",
    )(x, mod, w1.astype(BF16), w2.astype(BF16), row(pg), row(pb))


def _qkv_kernel(x_ref, mod_ref, wq_ref, wk_ref, wvt_ref, q_ref, k_ref, vt_ref):
    mod = mod_ref[0]
    shift, scale = mod[0:1], mod[1:2]
    h = (x_ref[0] * (1.0 + scale) + shift).astype(BF16)
    q = jnp.dot(h, wq_ref[...], preferred_element_type=F32)
    q_ref[0] = (q * (HEAD_DIM ** -0.5 * LOG2_E)).astype(BF16)
    k_ref[0] = jnp.dot(h, wk_ref[...], preferred_element_type=F32).astype(BF16)
    vt = lax.dot_general(wvt_ref[...], h, (((1,), (1,)), ((), ())), preferred_element_type=F32)
    vt_ref[0, 0] = vt.astype(BF16)


def _qkv(x, mod, w_qkv):
    bsz, s, d = x.shape
    tk = ATTN_TILE
    dq = N_HEADS * 2 * HEAD_DIM
    dv = N_HEADS * V_DIM
    wq = w_qkv[:, :dq].astype(BF16)
    wk = w_qkv[:, dq:2 * dq].astype(BF16)
    wvt = w_qkv[:, 2 * dq:].T.astype(BF16)
    return pl.pallas_call(
        _qkv_kernel,
        grid=(bsz, s // tk),
        in_specs=[
            pl.BlockSpec((1, tk, d), lambda b, i: (b, i, 0)),
            pl.BlockSpec((1, 3, d), lambda b, i: (b, 0, 0)),
            _const_spec((d, dq), True),
            _const_spec((d, dq), True),
            _const_spec((dv, d), True),
        ],
        out_specs=[
            pl.BlockSpec((1, tk, dq), lambda b, i: (b, i, 0)),
            pl.BlockSpec((1, tk, dq), lambda b, i: (b, i, 0)),
            pl.BlockSpec((1, 1, dv, tk), lambda b, i: (b, i, 0, 0)),
        ],
        out_shape=[
            jax.ShapeDtypeStruct((bsz, s, dq), BF16),
            jax.ShapeDtypeStruct((bsz, s, dq), BF16),
            jax.ShapeDtypeStruct((bsz, s // tk, dv, tk), BF16),
        ],
        compiler_params=pltpu.CompilerParams(
            dimension_semantics=("arbitrary", "arbitrary"),
            vmem_limit_bytes=_vmem_limit(40 * 1024 * 1024)),
        name="qkv_proj",
    )(x, mod, wq, wk, wvt)


def _bias_tiles_kernel(rb_ref, o_ref):
    h = pl.program_id(0)
    t = o_ref.shape[2]
    far = rb_ref[h, REL_BUCKETS - 1]
    kk = lax.broadcasted_iota(jnp.int32, (t, t), 0)
    qq = lax.broadcasted_iota(jnp.int32, (t, t), 1)
    for dlt in range(o_ref.shape[1]):
        rel = dlt * t + qq - kk
        val = jnp.full((t, t), (rb_ref[h, 0] - far) * LOG2_E, F32)
        for bkt in range(1, REL_BUCKETS):
            val = jnp.where(rel >= T5_THRESHOLDS[bkt], (rb_ref[h, bkt] - far) * LOG2_E, val)
        o_ref[0, dlt] = jnp.where(rel >= 0, val, MASK_VALUE)


def _bias_tiles(rel_bias):
    t = ATTN_TILE
    n_near = -(-(T5_LAST_BUCKET_START + t - 1) // t)
    return pl.pallas_call(
        _bias_tiles_kernel,
        grid=(N_HEADS,),
        in_specs=[pl.BlockSpec(memory_space=pltpu.SMEM)],
        out_specs=pl.BlockSpec((1, n_near, t, t), lambda h: (h, 0, 0, 0)),
        out_shape=jax.ShapeDtypeStruct((N_HEADS, n_near, t, t), F32),
        compiler_params=pltpu.CompilerParams(dimension_semantics=("arbitrary",)),
        name="t5_bias_tiles",
    )(rel_bias.T)


_TAB_QI, _TAB_KJ = range(2)
_PIPE_DEPTH = 1
_UNROLL_CHOICES = (20, 18, 16, 14, 12)
ATTN_COL_BLOCK = 256
ATTN_STAGE_SKEW = 1


def _attn_schedule(pairs, inert):
    cols = [inert] * _PIPE_DEPTH + list(pairs) + [inert] * _PIPE_DEPTH
    n_ticks = len(cols) - _PIPE_DEPTH
    unroll = min(_UNROLL_CHOICES, key=lambda u: (-n_ticks % u, -u))
    cols += [inert] * (-n_ticks % unroll)
    return np.asarray(cols, np.int32).T, unroll


_NEAR_PERIOD = 2


def _near_distance(col):
    return (col + 1) % _NEAR_PERIOD


def _near_extent(dist, c, t, cb):
    if dist is None:
        return t, t
    q_lo = (c * cb) % t
    if dist == 0:
        return min(t, q_lo + cb), 0
    lo = dist * t + q_lo - T5_LAST_BUCKET_START + 1
    return t, min(t, max(0, lo // V7X_SUBLANES * V7X_SUBLANES))


def _attn_schedules(nq, n_near):
    assert n_near == _NEAR_PERIOD
    far = [(qi, kj) for qi in range(nq) for kj in range(qi - n_near + 1)]
    near = [(qi, qi - d) for qi in range(nq) for d in reversed(range(min(n_near, qi + 1)))]
    assert all(qi - kj == _near_distance(_PIPE_DEPTH + i) for i, (qi, kj) in enumerate(near))
    near_tab, near_unroll = _attn_schedule(near, (nq, 0))
    visited = range(near_tab.shape[1] - _PIPE_DEPTH)
    assert all(near_tab[_TAB_QI, j] < nq for j in visited if _near_distance(j) == 0)
    return _attn_schedule(far, (nq, 0)), (near_tab, near_unroll)


def _attn_kernel(far_ref, near_ref, q_ref, k_ref, vt_ref, bias_ref, lam_ref, g_ref, o_ref,
                 s0_ref, s1_ref, mt0_ref, mt1_ref, qc_ref, m_ref, l_ref, acc_ref,
                 *, far_unroll, near_unroll):
    s_refs, mt_refs = (s0_ref, s1_ref), (mt0_ref, mt1_ref)
    n_cb, t, cb = s0_ref.shape
    nq = m_ref.shape[0] - 1

    s_refs[1][...] = jnp.zeros(s_refs[1].shape, F32)
    mt_refs[1][...] = jnp.zeros(mt_refs[1].shape, F32)
    m_ref[...] = jnp.full(m_ref.shape, MASK_VALUE, F32)
    l_ref[...] = jnp.zeros(l_ref.shape, F32)
    acc_ref[...] = jnp.zeros(acc_ref.shape, F32)

    def mask_q(qi, carry):
        q = q_ref[0, pl.ds(pl.multiple_of(qi * t, t), t), :]
        lane = lax.broadcasted_iota(jnp.int32, q.shape, 1)
        zero = jnp.zeros_like(q)
        qcat = jnp.concatenate([jnp.where(lane < HEAD_DIM, q, zero),
                                jnp.where(lane >= HEAD_DIM, q, zero)], axis=0)
        qc_ref[qi] = qcat.reshape(n_cb, cb, q.shape[1])
        return carry

    lax.fori_loop(0, nq, mask_q, 0, unroll=4)

    def q_tile_index(qi):
        return jnp.where(qi == nq, 0, qi)

    lam_v = lam_ref[...]
    lam = (jnp.exp(jnp.sum(lam_v[0:1] * lam_v[1:2], axis=-1, keepdims=True))
           - jnp.exp(jnp.sum(lam_v[2:3] * lam_v[3:4], axis=-1, keepdims=True)) + LAMBDA_INIT)

    def finish(qi):
        def normalised(c):
            return acc_ref[qi, c] * (1.0 / l_ref[qi, c])
        half = n_cb // 2
        o = jnp.concatenate([normalised(c) - lam * normalised(half + c) for c in range(half)],
                            axis=1).T
        o = (o * lax.rsqrt(jnp.mean(o * o, axis=-1, keepdims=True) + LN_EPS)
             * g_ref[...] * (1.0 - LAMBDA_INIT))
        o_ref[0, pl.ds(pl.multiple_of(qi * t, t), t), :] = o.astype(BF16)

    def tick(tab_ref, n, u, near):
        col_a, col_b = n + 1, n
        par, cur = u % 2, 1 - u % 2
        dist_a, dist_b = (_near_distance(u + 1), _near_distance(u)) if near else (None, None)

        qi_a = q_tile_index(tab_ref[_TAB_QI, col_a])
        qi_b = tab_ref[_TAB_QI, col_b]
        vt = vt_ref[0, tab_ref[_TAB_KJ, col_b]]
        k = k_ref[0, pl.ds(pl.multiple_of(tab_ref[_TAB_KJ, col_a] * t, t), t), :]

        def stage_a(c):
            keys_a, bias_lo = _near_extent(dist_a, c, t, cb)
            st = lax.dot_general(k[:keys_a], qc_ref[qi_a, c], (((1,), (1,)), ((), ())),
                                 preferred_element_type=F32)
            if bias_lo < keys_a:
                bias = bias_ref[0, dist_a, bias_lo:keys_a, pl.ds((c * cb) % t, cb)]
                st = jnp.concatenate([st[:bias_lo], st[bias_lo:] + bias], axis=0) if bias_lo else st + bias
            s_refs[par][c, 0:keys_a] = st
            mt_refs[par][c] = jnp.max(st, axis=0, keepdims=True)

        def stage_b(c):
            keys_b, _ = _near_extent(dist_b, c, t, cb)
            m_old = m_ref[qi_b, c]
            m_new = jnp.maximum(m_old, mt_refs[cur][c])
            m_ref[qi_b, c] = m_new
            p = jnp.exp2(s_refs[cur][c, 0:keys_b] - m_new)
            pv = jnp.dot(vt[:, :keys_b], p.astype(BF16), preferred_element_type=F32)
            alpha = jnp.exp2(m_old - m_new)
            acc_ref[qi_b, c] = alpha * acc_ref[qi_b, c] + pv
            l_ref[qi_b, c] = alpha * l_ref[qi_b, c] + jnp.sum(p, axis=0, keepdims=True)

        for c in range(n_cb + ATTN_STAGE_SKEW):
            if c < n_cb:
                stage_a(c)
            if c >= ATTN_STAGE_SKEW:
                stage_b(c - ATTN_STAGE_SKEW)
        if dist_b == 0:
            finish(qi_b)

    def run(tab_ref, unroll, near):
        n_ticks = tab_ref.shape[1] - _PIPE_DEPTH
        assert n_ticks % unroll == 0 and unroll % _NEAR_PERIOD == 0

        def body(i, carry):
            for u in range(unroll):
                tick(tab_ref, unroll * i + u, u, near)
            return carry

        lax.fori_loop(0, n_ticks // unroll, body, 0)

    run(far_ref, far_unroll, False)
    run(near_ref, near_unroll, True)


def _attention(q, k, vt, bias, lam_vecs, g_sub):
    bsz, s, dq = q.shape
    t = ATTN_TILE
    nq = s // t
    n_near = bias.shape[1]
    hw = 2 * HEAD_DIM
    (far_tab, far_unroll), (near_tab, near_unroll) = _attn_schedules(nq, n_near)
    cb = ATTN_COL_BLOCK
    n_cb = 2 * t // cb
    return pl.pallas_call(
        functools.partial(_attn_kernel, far_unroll=far_unroll, near_unroll=near_unroll),
        grid=(bsz, N_HEADS),
        in_specs=[
            pl.BlockSpec(memory_space=pltpu.SMEM),
            pl.BlockSpec(memory_space=pltpu.SMEM),
            pl.BlockSpec((1, s, hw), lambda b, h: (b, 0, h)),
            pl.BlockSpec((1, s, hw), lambda b, h: (b, 0, h)),
            pl.BlockSpec((1, nq, V_DIM, t), lambda b, h: (b, 0, h, 0)),
            pl.BlockSpec((1, n_near, t, t), lambda b, h: (h, 0, 0, 0)),
            _const_spec((4, HEAD_DIM)),
            _const_spec((1, V_DIM)),
        ],
        out_specs=pl.BlockSpec((1, s, V_DIM), lambda b, h: (b, 0, h)),
        out_shape=jax.ShapeDtypeStruct((bsz, s, N_HEADS * V_DIM), BF16),
        scratch_shapes=[
            pltpu.VMEM((n_cb, t, cb), F32),
            pltpu.VMEM((n_cb, t, cb), F32),
            pltpu.VMEM((n_cb, 1, cb), F32),
            pltpu.VMEM((n_cb, 1, cb), F32),
            pltpu.VMEM((nq, n_cb, cb, hw), BF16),
            pltpu.VMEM((nq + 1, n_cb, 1, cb), F32),
            pltpu.VMEM((nq + 1, n_cb, 1, cb), F32),
            pltpu.VMEM((nq + 1, n_cb, V_DIM, cb), F32),
        ],
        compiler_params=pltpu.CompilerParams(
            dimension_semantics=("arbitrary", "arbitrary"),
            vmem_limit_bytes=_vmem_limit(52 * 1024 * 1024)),
        name="diff_attention",
    )(jnp.asarray(far_tab), jnp.asarray(near_tab), q, k, vt, bias, lam_vecs,
      g_sub.reshape(1, V_DIM))


def _out_proj_mlp_kernel(a_ref, x_ref, amod_ref, wo_ref, mg_ref, mb_ref,
                         mod_ref, w1_ref, w2_ref, pg_ref, pb_ref, o_ref):
    gate = amod_ref[0][2:3]
    y = jnp.dot(a_ref[0], wo_ref[...], preferred_element_type=F32)
    x = _layer_norm(ALPHA * x_ref[0] + gate * y, mg_ref[...], mb_ref[...])
    o_ref[0] = _mlp_sublayer(x, mod_ref[0], w1_ref, w2_ref, pg_ref, pb_ref)


def _out_proj_mlp(a, x, amod, wo, mg, mb, mod, w1, w2, pg, pb):
    bsz, s, d = x.shape
    da = a.shape[2]
    d_ff = w1.shape[1]
    ts = SEQ_TILE
    row = lambda v: v.reshape(1, -1)
    return pl.pallas_call(
        _out_proj_mlp_kernel,
        grid=(bsz, s // ts),
        in_specs=[
            pl.BlockSpec((1, ts, da), lambda b, i: (b, i, 0)),
            pl.BlockSpec((1, ts, d), lambda b, i: (b, i, 0)),
            pl.BlockSpec((1, 3, d), lambda b, i: (b, 0, 0)),
            _const_spec((da, d), True),
            _const_spec((1, d)),
            _const_spec((1, d)),
            pl.BlockSpec((1, 3, d), lambda b, i: (b, 0, 0)),
            _const_spec((d, d_ff), True),
            _const_spec((d_ff, d), True),
            _const_spec((1, d)),
            _const_spec((1, d)),
        ],
        out_specs=pl.BlockSpec((1, ts, d), lambda b, i: (b, i, 0)),
        out_shape=jax.ShapeDtypeStruct((bsz, s, d), F32),
        compiler_params=pltpu.CompilerParams(
            dimension_semantics=("arbitrary", "arbitrary"),
            vmem_limit_bytes=_vmem_limit(54 * 1024 * 1024)),
        name="out_proj_mlp",
    )(a, x, amod, wo.astype(BF16), row(mg), row(mb),
      mod, w1.astype(BF16), w2.astype(BF16), row(pg), row(pb))


def kernel(x, c, conv_mod_w, conv_mod_b, conv_pw1_w, conv_pw1_b, conv_dw_w, conv_dw_b, conv_norm_g, conv_norm_b, conv_pw2_w, conv_pw2_b, attn_mod_w, attn_mod_b, attn_qkv_w, attn_lam_q1, attn_lam_k1, attn_lam_q2, attn_lam_k2, attn_subln_g, attn_out_w, rel_bias, mlp_mod_w, mlp_mod_b, mlp_w1, mlp_w2, post_mix_g, post_mix_b, post_mlp_g, post_mlp_b):
    assert x.shape[1] % SEQ_TILE == 0 and x.shape[1] % ATTN_TILE == 0
    assert x.shape[1] % CONV_TILE == 0 and CONV_TILE % CONV_ROWS == 0 and CONV_HALO >= CONV_WIDTH - 1
    conv_mod = _ada_mod(c, conv_mod_w, conv_mod_b)
    attn_mod = _ada_mod(c, attn_mod_w, attn_mod_b)
    mlp_mod = _ada_mod(c, mlp_mod_w, mlp_mod_b)

    x = _conv_mixer(x, conv_mod[0], conv_pw1_w[0], conv_pw1_b[0], conv_dw_w[0], conv_dw_b[0],
                    conv_norm_g[0], conv_norm_b[0], conv_pw2_w[0], conv_pw2_b[0],
                    post_mix_g[0], post_mix_b[0])
    x = _mlp(x, mlp_mod[0], mlp_w1[0], mlp_w2[0], post_mlp_g[0], post_mlp_b[0])

    q, k, vt = _qkv(x, attn_mod[0], attn_qkv_w[0])
    bias = _bias_tiles(rel_bias)
    lam_vecs = jnp.stack([attn_lam_q1[0], attn_lam_k1[0], attn_lam_q2[0], attn_lam_k2[0]])
    a = _attention(q, k, vt, bias, lam_vecs, attn_subln_g[0])
    return _out_proj_mlp(a, x, attn_mod[0], attn_out_w[0], post_mix_g[1], post_mix_b[1],
                         mlp_mod[1], mlp_w1[1], mlp_w2[1], post_mlp_g[1], post_mlp_b[1])
```

```python
import functools
import math

import jax
import jax.numpy as jnp
import numpy as np
from jax import lax
from jax.experimental import pallas as pl
from jax.experimental.pallas import tpu as pltpu

DEPTH = 2
CONV_WIDTH = 31
N_HEADS = 8
HEAD_DIM = 64
V_DIM = 2 * HEAD_DIM
REL_BUCKETS = 32
REL_MAX_DIST = 128
ALPHA = (2 * DEPTH) ** 0.25
LN_EPS = 1e-5
ATTN_LAYER = 1
LAMBDA_INIT = 0.8 - 0.6 * math.exp(-0.3 * ATTN_LAYER)
LOG2_E = math.log2(math.e)

V7X_SUBLANES = 8
V7X_LANES = 128
V7X_VMEM_BYTES = 64 * 1024 * 1024
V7X_VMEM_RESERVE_BYTES = 8 * 1024 * 1024

MASK_VALUE = -1e30
CONV_HALO = 32
CONV_ROWS = 512
SEQ_TILE = 1024
CONV_TILE = 1024
ATTN_TILE = 512
FF_CHUNK = 1024

F32 = jnp.float32
BF16 = jnp.bfloat16


def _t5_thresholds():
    max_exact = REL_BUCKETS // 2
    buckets = []
    for n in range(2 * REL_MAX_DIST):
        if n < max_exact:
            buckets.append(n)
        else:
            v = math.log(n / max_exact) / math.log(REL_MAX_DIST / max_exact) * (REL_BUCKETS - max_exact)
            buckets.append(min(max_exact + int(v), REL_BUCKETS - 1))
    assert all(b1 >= b0 for b0, b1 in zip(buckets, buckets[1:]))
    assert buckets[-1] == REL_BUCKETS - 1
    return [buckets.index(b) for b in range(REL_BUCKETS)]


T5_THRESHOLDS = _t5_thresholds()
T5_LAST_BUCKET_START = T5_THRESHOLDS[REL_BUCKETS - 1]


def _vmem_limit(nbytes):
    return int(min(nbytes, V7X_VMEM_BYTES - V7X_VMEM_RESERVE_BYTES))


def _layer_norm(z, g, b):
    mu = jnp.mean(z, axis=-1, keepdims=True)
    zc = z - mu
    var = jnp.mean(zc * zc, axis=-1, keepdims=True)
    return zc * lax.rsqrt(var + LN_EPS) * g + b


def _const_spec(shape, single_buffer=False):
    nd = len(shape)
    kwargs = {"pipeline_mode": pl.Buffered(1)} if single_buffer else {}
    return pl.BlockSpec(shape, lambda *_: (0,) * nd, **kwargs)


def _ada_mod_kernel(c_ref, w_ref, b_ref, o_ref):
    c = c_ref[...]
    sc = c * jax.nn.sigmoid(c)
    o_ref[0] = jnp.dot(sc, w_ref[0], preferred_element_type=F32,
                       precision=lax.Precision.HIGHEST) + b_ref[0]


def _ada_mod(c, w, b):
    n, d, d3 = w.shape
    bsz = c.shape[0]
    nblk = d3 // d
    out = pl.pallas_call(
        _ada_mod_kernel,
        grid=(n, nblk),
        in_specs=[
            pl.BlockSpec((bsz, d), lambda i, j: (0, 0)),
            pl.BlockSpec((1, d, d), lambda i, j: (i, 0, j)),
            pl.BlockSpec((1, 1, d), lambda i, j: (i, 0, j)),
        ],
        out_specs=pl.BlockSpec((1, bsz, d), lambda i, j: (i, 0, j)),
        out_shape=jax.ShapeDtypeStruct((n, bsz, d3), F32),
        compiler_params=pltpu.CompilerParams(
            dimension_semantics=("arbitrary", "arbitrary"),
            vmem_limit_bytes=_vmem_limit(32 * 1024 * 1024)),
        name="ada_mod",
    )(c, w, b.reshape(n, 1, d3))
    return out.reshape(n, bsz, 3, d)


def _conv_mixer_kernel(x_ref, mod_ref, w1_ref, b1_ref, wdw_ref, bdw_ref, gcn_ref, bcn_ref,
                       w2_ref, b2_ref, pg_ref, pb_ref, o_ref, ext_ref, y_ref):
    ts, d = x_ref.shape[1], x_ref.shape[2]
    n_lb = ext_ref.shape[0]
    s_idx = pl.program_id(1)

    @pl.when(s_idx == 0)
    def _():
        ext_ref[:, 0:CONV_HALO, :] = jnp.zeros((n_lb, CONV_HALO, V7X_LANES), F32)

    mod = mod_ref[0]
    shift, scale, gate = mod[0:1], mod[1:2], mod[2:3]
    x = x_ref[0]
    h = (x * (1.0 + scale) + shift).astype(BF16)
    a = jnp.dot(h, w1_ref[...], preferred_element_type=F32) + b1_ref[...]
    u = a[:, :d] * jax.nn.sigmoid(a[:, d:])
    for lb in range(n_lb):
        ext_ref[lb, CONV_HALO:CONV_HALO + ts, :] = u[:, lb * V7X_LANES:(lb + 1) * V7X_LANES]

    off0 = CONV_HALO - (CONV_WIDTH - 1)

    n_grp = CONV_ROWS // V7X_SUBLANES

    def row_block(rb, carry):
        t0 = pl.multiple_of(rb * CONV_ROWS, CONV_ROWS)

        def lane_block(lb, carry2):
            w_all = wdw_ref[lb]
            acc = [jnp.broadcast_to(bdw_ref[lb], (V7X_SUBLANES, V7X_LANES))] * n_grp
            for r in range(V7X_SUBLANES):
                taps = [j for j in range(CONV_WIDTH) if (off0 + j) % V7X_SUBLANES == r]
                first = off0 + taps[0]
                n_load = n_grp + (taps[-1] - taps[0]) // V7X_SUBLANES
                groups = [ext_ref[lb, pl.ds(t0 + (first + V7X_SUBLANES * g), V7X_SUBLANES), :]
                          for g in range(n_load)]
                for j in taps:
                    w_row = jnp.broadcast_to(w_all[j:j + 1], (V7X_SUBLANES, V7X_LANES))
                    g0 = (off0 + j - first) // V7X_SUBLANES
                    acc = [acc[i] + w_row * groups[g0 + i] for i in range(n_grp)]
            y_ref[lb, pl.ds(t0, CONV_ROWS), :] = jnp.concatenate(acc, axis=0)
            return carry2

        lax.fori_loop(0, n_lb, lane_block, 0)
        return carry

    lax.fori_loop(0, ts // CONV_ROWS, row_block, 0)
    ext_ref[:, 0:CONV_HALO, :] = ext_ref[:, ts:ts + CONV_HALO, :]

    y = y_ref[...]
    mu = jnp.sum(jnp.sum(y, axis=0, keepdims=True), axis=2, keepdims=True) * (1.0 / d)
    yc = y - mu
    var = jnp.sum(jnp.sum(yc * yc, axis=0, keepdims=True), axis=2, keepdims=True) * (1.0 / d)
    y = yc * lax.rsqrt(var + LN_EPS) * gcn_ref[...] + bcn_ref[...]
    y = (y * jax.nn.sigmoid(y)).astype(BF16)
    v = jnp.concatenate([y[lb] for lb in range(n_lb)], axis=-1)
    y = jnp.dot(v, w2_ref[...], preferred_element_type=F32) + b2_ref[...]
    o_ref[0] = _layer_norm(ALPHA * x_ref[0] + gate * y, pg_ref[...], pb_ref[...])


def _conv_mixer(x, mod, w1, b1, wdw, bdw, gcn, bcn, w2, b2, pg, pb):
    bsz, s, d = x.shape
    ts = CONV_TILE
    n_lb = d // V7X_LANES
    row = lambda v: v.reshape(1, -1)
    slab = lambda v: v.reshape(-1, n_lb, V7X_LANES).transpose(1, 0, 2)
    return pl.pallas_call(
        _conv_mixer_kernel,
        grid=(bsz, s // ts),
        in_specs=[
            pl.BlockSpec((1, ts, d), lambda b, i: (b, i, 0)),
            pl.BlockSpec((1, 3, d), lambda b, i: (b, 0, 0)),
            _const_spec((d, 2 * d), True),
            _const_spec((1, 2 * d)),
            _const_spec((n_lb, CONV_WIDTH, V7X_LANES)),
            _const_spec((n_lb, 1, V7X_LANES)),
            _const_spec((n_lb, 1, V7X_LANES)),
            _const_spec((n_lb, 1, V7X_LANES)),
            _const_spec((d, d), True),
            _const_spec((1, d)),
            _const_spec((1, d)),
            _const_spec((1, d)),
        ],
        out_specs=pl.BlockSpec((1, ts, d), lambda b, i: (b, i, 0)),
        out_shape=jax.ShapeDtypeStruct((bsz, s, d), F32),
        scratch_shapes=[
            pltpu.VMEM((n_lb, ts + CONV_HALO, V7X_LANES), F32),
            pltpu.VMEM((n_lb, ts, V7X_LANES), F32),
        ],
        compiler_params=pltpu.CompilerParams(
            dimension_semantics=("arbitrary", "arbitrary"),
            vmem_limit_bytes=_vmem_limit(56 * 1024 * 1024)),
        name="conv_mixer",
    )(x, mod, w1.astype(BF16), row(b1), slab(wdw), slab(bdw), slab(gcn), slab(bcn),
      w2.astype(BF16), row(b2), row(pg), row(pb))


def _mlp_sublayer(x, mod, w1_ref, w2_ref, pg_ref, pb_ref):
    d_ff = w1_ref.shape[1]
    shift, scale, gate = mod[0:1], mod[1:2], mod[2:3]
    h = (x * (1.0 + scale) + shift).astype(BF16)
    y = None
    for c0 in range(0, d_ff, FF_CHUNK):
        a = jnp.dot(h, w1_ref[:, c0:c0 + FF_CHUNK], preferred_element_type=F32)
        a = jnp.maximum(a, 0.0)
        part = jnp.dot((a * a).astype(BF16), w2_ref[c0:c0 + FF_CHUNK, :], preferred_element_type=F32)
        y = part if y is None else y + part
    return _layer_norm(ALPHA * x + gate * y, pg_ref[...], pb_ref[...])


def _mlp_kernel(x_ref, mod_ref, w1_ref, w2_ref, pg_ref, pb_ref, o_ref):
    o_ref[0] = _mlp_sublayer(x_ref[0], mod_ref[0], w1_ref, w2_ref, pg_ref, pb_ref)


def _mlp(x, mod, w1, w2, pg, pb):
    bsz, s, d = x.shape
    d_ff = w1.shape[1]
    ts = SEQ_TILE
    row = lambda v: v.reshape(1, -1)
    return pl.pallas_call(
        _mlp_kernel,
        grid=(bsz, s // ts),
        in_specs=[
            pl.BlockSpec((1, ts, d), lambda b, i: (b, i, 0)),
            pl.BlockSpec((1, 3, d), lambda b, i: (b, 0, 0)),
            _const_spec((d, d_ff), True),
            _const_spec((d_ff, d), True),
            _const_spec((1, d)),
            _const_spec((1, d)),
        ],
        out_specs=pl.BlockSpec((1, ts, d), lambda b, i: (b, i, 0)),
        out_shape=jax.ShapeDtypeStruct((bsz, s, d), F32),
        compiler_params=pltpu.CompilerParams(
            dimension_semantics=("arbitrary", "arbitrary"),
            vmem_limit_bytes=_vmem_limit(52 * 1024 * 1024)),
        name="sq_relu_mlp",
    )(x, mod, w1.astype(BF16), w2.astype(BF16), row(pg), row(pb))


def _qkv_kernel(x_ref, mod_ref, wq_ref, wk_ref, wvt_ref, qc_ref, k_ref, vt_ref):
    mod = mod_ref[0]
    shift, scale = mod[0:1], mod[1:2]
    h = (x_ref[0] * (1.0 + scale) + shift).astype(BF16)
    q = (jnp.dot(h, wq_ref[...], preferred_element_type=F32) * (HEAD_DIM ** -0.5 * LOG2_E)).astype(BF16)
    n_cb, cb, hw = qc_ref.shape[3:]
    lane = lax.broadcasted_iota(jnp.int32, (q.shape[0], hw), 1)
    for hd in range(qc_ref.shape[2]):
        qh = q[:, hd * hw:(hd + 1) * hw]
        zero = jnp.zeros_like(qh)
        qcat = jnp.concatenate([jnp.where(lane < HEAD_DIM, qh, zero),
                                jnp.where(lane >= HEAD_DIM, qh, zero)], axis=0)
        qc_ref[0, 0, hd] = qcat.reshape(n_cb, cb, hw)
    k_ref[0] = jnp.dot(h, wk_ref[...], preferred_element_type=F32).astype(BF16)
    vt = lax.dot_general(wvt_ref[...], h, (((1,), (1,)), ((), ())), preferred_element_type=F32)
    vt_ref[0, 0] = vt.astype(BF16)


def _qkv(x, mod, w_qkv):
    bsz, s, d = x.shape
    tk = ATTN_TILE
    cb, hw = ATTN_COL_BLOCK, 2 * HEAD_DIM
    n_cb = 2 * tk // cb
    dq = N_HEADS * hw
    dv = N_HEADS * V_DIM
    wq = w_qkv[:, :dq].astype(BF16)
    wk = w_qkv[:, dq:2 * dq].astype(BF16)
    wvt = w_qkv[:, 2 * dq:].T.astype(BF16)
    return pl.pallas_call(
        _qkv_kernel,
        grid=(bsz, s // tk),
        in_specs=[
            pl.BlockSpec((1, tk, d), lambda b, i: (b, i, 0)),
            pl.BlockSpec((1, 3, d), lambda b, i: (b, 0, 0)),
            _const_spec((d, dq), True),
            _const_spec((d, dq), True),
            _const_spec((dv, d), True),
        ],
        out_specs=[
            pl.BlockSpec((1, 1, N_HEADS, n_cb, cb, hw), lambda b, i: (b, i, 0, 0, 0, 0)),
            pl.BlockSpec((1, tk, dq), lambda b, i: (b, i, 0)),
            pl.BlockSpec((1, 1, dv, tk), lambda b, i: (b, i, 0, 0)),
        ],
        out_shape=[
            jax.ShapeDtypeStruct((bsz, s // tk, N_HEADS, n_cb, cb, hw), BF16),
            jax.ShapeDtypeStruct((bsz, s, dq), BF16),
            jax.ShapeDtypeStruct((bsz, s // tk, dv, tk), BF16),
        ],
        compiler_params=pltpu.CompilerParams(
            dimension_semantics=("arbitrary", "arbitrary"),
            vmem_limit_bytes=_vmem_limit(40 * 1024 * 1024)),
        name="qkv_proj",
    )(x, mod, wq, wk, wvt)


def _bias_tiles_kernel(rb_ref, o_ref):
    h = pl.program_id(0)
    t = o_ref.shape[2]
    far = rb_ref[h, REL_BUCKETS - 1]
    kk = lax.broadcasted_iota(jnp.int32, (t, t), 0)
    qq = lax.broadcasted_iota(jnp.int32, (t, t), 1)
    for dlt in range(o_ref.shape[1]):
        rel = dlt * t + qq - kk
        val = jnp.full((t, t), (rb_ref[h, 0] - far) * LOG2_E, F32)
        for bkt in range(1, REL_BUCKETS):
            val = jnp.where(rel >= T5_THRESHOLDS[bkt], (rb_ref[h, bkt] - far) * LOG2_E, val)
        o_ref[0, dlt] = jnp.where(rel >= 0, val, MASK_VALUE)


def _bias_tiles(rel_bias):
    t = ATTN_TILE
    n_near = -(-(T5_LAST_BUCKET_START + t - 1) // t)
    return pl.pallas_call(
        _bias_tiles_kernel,
        grid=(N_HEADS,),
        in_specs=[pl.BlockSpec(memory_space=pltpu.SMEM)],
        out_specs=pl.BlockSpec((1, n_near, t, t), lambda h: (h, 0, 0, 0)),
        out_shape=jax.ShapeDtypeStruct((N_HEADS, n_near, t, t), F32),
        compiler_params=pltpu.CompilerParams(dimension_semantics=("arbitrary",)),
        name="t5_bias_tiles",
    )(rel_bias.T)


_TAB_QI, _TAB_KJ = range(2)
_PIPE_DEPTH = 1
_UNROLL_CHOICES = (20, 18, 16, 14, 12)
ATTN_COL_BLOCK = 256
ATTN_STAGE_SKEW = 1


def _attn_schedule(pairs, inert):
    cols = [inert] * _PIPE_DEPTH + list(pairs) + [inert] * _PIPE_DEPTH
    n_ticks = len(cols) - _PIPE_DEPTH
    unroll = min(_UNROLL_CHOICES, key=lambda u: (-n_ticks % u, -u))
    cols += [inert] * (-n_ticks % unroll)
    return np.asarray(cols, np.int32).T, unroll


_NEAR_PERIOD = 2


def _near_distance(col):
    return (col + 1) % _NEAR_PERIOD


def _near_extent(dist, c, t, cb):
    if dist is None:
        return t, t
    q_lo = (c * cb) % t
    if dist == 0:
        return min(t, q_lo + cb), 0
    lo = dist * t + q_lo - T5_LAST_BUCKET_START + 1
    return t, min(t, max(0, lo // V7X_SUBLANES * V7X_SUBLANES))


def _attn_schedules(nq, n_near):
    assert n_near == _NEAR_PERIOD
    far = [(qi, kj) for qi in range(nq) for kj in range(qi - n_near + 1)]
    near = [(qi, qi - d) for qi in range(nq) for d in reversed(range(min(n_near, qi + 1)))]
    assert all(qi - kj == _near_distance(_PIPE_DEPTH + i) for i, (qi, kj) in enumerate(near))
    near_tab, near_unroll = _attn_schedule(near, (nq, 0))
    visited = range(near_tab.shape[1] - _PIPE_DEPTH)
    assert all(near_tab[_TAB_QI, j] < nq for j in visited if _near_distance(j) == 0)
    return _attn_schedule(far, (nq, 0)), (near_tab, near_unroll)


def _attn_kernel(far_ref, near_ref, qc_ref, k_ref, vt_ref, bias_ref, lam_ref, g_ref, o_ref,
                 s0_ref, s1_ref, mt0_ref, mt1_ref, m_ref, l_ref, acc_ref,
                 *, far_unroll, near_unroll):
    s_refs, mt_refs = (s0_ref, s1_ref), (mt0_ref, mt1_ref)
    n_cb, t, cb = s0_ref.shape
    nq = m_ref.shape[0] - 1

    s_refs[1][...] = jnp.zeros(s_refs[1].shape, F32)
    mt_refs[1][...] = jnp.zeros(mt_refs[1].shape, F32)
    m_ref[...] = jnp.full(m_ref.shape, MASK_VALUE, F32)
    l_ref[...] = jnp.zeros(l_ref.shape, F32)
    acc_ref[...] = jnp.zeros(acc_ref.shape, F32)

    def q_tile_index(qi):
        return jnp.where(qi == nq, 0, qi)

    lam_v = lam_ref[...]
    lam = (jnp.exp(jnp.sum(lam_v[0:1] * lam_v[1:2], axis=-1, keepdims=True))
           - jnp.exp(jnp.sum(lam_v[2:3] * lam_v[3:4], axis=-1, keepdims=True)) + LAMBDA_INIT)

    def finish(qi):
        def normalised(c):
            return acc_ref[qi, c] * (1.0 / l_ref[qi, c])
        half = n_cb // 2
        o = jnp.concatenate([normalised(c) - lam * normalised(half + c) for c in range(half)],
                            axis=1).T
        o = (o * lax.rsqrt(jnp.mean(o * o, axis=-1, keepdims=True) + LN_EPS)
             * g_ref[...] * (1.0 - LAMBDA_INIT))
        o_ref[0, pl.ds(pl.multiple_of(qi * t, t), t), :] = o.astype(BF16)

    def tick(tab_ref, n, u, near):
        col_a, col_b = n + 1, n
        par, cur = u % 2, 1 - u % 2
        dist_a, dist_b = (_near_distance(u + 1), _near_distance(u)) if near else (None, None)

        qi_a = q_tile_index(tab_ref[_TAB_QI, col_a])
        qi_b = tab_ref[_TAB_QI, col_b]
        vt = vt_ref[0, tab_ref[_TAB_KJ, col_b]]
        k = k_ref[0, pl.ds(pl.multiple_of(tab_ref[_TAB_KJ, col_a] * t, t), t), :]

        def stage_a(c):
            keys_a, bias_lo = _near_extent(dist_a, c, t, cb)
            st = lax.dot_general(k[:keys_a], qc_ref[0, qi_a, 0, c], (((1,), (1,)), ((), ())),
                                 preferred_element_type=F32)
            if bias_lo < keys_a:
                bias = bias_ref[0, dist_a, bias_lo:keys_a, pl.ds((c * cb) % t, cb)]
                st = jnp.concatenate([st[:bias_lo], st[bias_lo:] + bias], axis=0) if bias_lo else st + bias
            s_refs[par][c, 0:keys_a] = st
            mt_refs[par][c] = jnp.max(st, axis=0, keepdims=True)

        def stage_b(c):
            keys_b, _ = _near_extent(dist_b, c, t, cb)
            m_old = m_ref[qi_b, c]
            m_new = jnp.maximum(m_old, mt_refs[cur][c])
            m_ref[qi_b, c] = m_new
            p = jnp.exp2(s_refs[cur][c, 0:keys_b] - m_new)
            pv = jnp.dot(vt[:, :keys_b], p.astype(BF16), preferred_element_type=F32)
            alpha = jnp.exp2(m_old - m_new)
            acc_ref[qi_b, c] = alpha * acc_ref[qi_b, c] + pv
            l_ref[qi_b, c] = alpha * l_ref[qi_b, c] + jnp.sum(p, axis=0, keepdims=True)

        for c in range(n_cb + ATTN_STAGE_SKEW):
            if c < n_cb:
                stage_a(c)
            if c >= ATTN_STAGE_SKEW:
                stage_b(c - ATTN_STAGE_SKEW)
        if dist_b == 0:
            finish(qi_b)

    def run(tab_ref, unroll, near):
        n_ticks = tab_ref.shape[1] - _PIPE_DEPTH
        assert n_ticks % unroll == 0 and unroll % _NEAR_PERIOD == 0

        def body(i, carry):
            for u in range(unroll):
                tick(tab_ref, unroll * i + u, u, near)
            return carry

        lax.fori_loop(0, n_ticks // unroll, body, 0)

    run(far_ref, far_unroll, False)
    run(near_ref, near_unroll, True)


def _attention(qc, k, vt, bias, lam_vecs, g_sub):
    bsz, nq, _, n_cb, cb, hw = qc.shape
    s = k.shape[1]
    t = s // nq
    n_near = bias.shape[1]
    (far_tab, far_unroll), (near_tab, near_unroll) = _attn_schedules(nq, n_near)
    return pl.pallas_call(
        functools.partial(_attn_kernel, far_unroll=far_unroll, near_unroll=near_unroll),
        grid=(bsz, N_HEADS),
        in_specs=[
            pl.BlockSpec(memory_space=pltpu.SMEM),
            pl.BlockSpec(memory_space=pltpu.SMEM),
            pl.BlockSpec((1, nq, 1, n_cb, cb, hw), lambda b, h: (b, 0, h, 0, 0, 0)),
            pl.BlockSpec((1, s, hw), lambda b, h: (b, 0, h)),
            pl.BlockSpec((1, nq, V_DIM, t), lambda b, h: (b, 0, h, 0)),
            pl.BlockSpec((1, n_near, t, t), lambda b, h: (h, 0, 0, 0)),
            _const_spec((4, HEAD_DIM)),
            _const_spec((1, V_DIM)),
        ],
        out_specs=pl.BlockSpec((1, s, V_DIM), lambda b, h: (b, 0, h)),
        out_shape=jax.ShapeDtypeStruct((bsz, s, N_HEADS * V_DIM), BF16),
        scratch_shapes=[
            pltpu.VMEM((n_cb, t, cb), F32),
            pltpu.VMEM((n_cb, t, cb), F32),
            pltpu.VMEM((n_cb, 1, cb), F32),
            pltpu.VMEM((n_cb, 1, cb), F32),
            pltpu.VMEM((nq + 1, n_cb, 1, cb), F32),
            pltpu.VMEM((nq + 1, n_cb, 1, cb), F32),
            pltpu.VMEM((nq + 1, n_cb, V_DIM, cb), F32),
        ],
        compiler_params=pltpu.CompilerParams(
            dimension_semantics=("arbitrary", "arbitrary"),
            vmem_limit_bytes=_vmem_limit(52 * 1024 * 1024)),
        name="diff_attention",
    )(jnp.asarray(far_tab), jnp.asarray(near_tab), qc, k, vt, bias, lam_vecs,
      g_sub.reshape(1, V_DIM))


def _out_proj_mlp_kernel(a_ref, x_ref, amod_ref, wo_ref, mg_ref, mb_ref,
                         mod_ref, w1_ref, w2_ref, pg_ref, pb_ref, o_ref):
    gate = amod_ref[0][2:3]
    y = jnp.dot(a_ref[0], wo_ref[...], preferred_element_type=F32)
    x = _layer_norm(ALPHA * x_ref[0] + gate * y, mg_ref[...], mb_ref[...])
    o_ref[0] = _mlp_sublayer(x, mod_ref[0], w1_ref, w2_ref, pg_ref, pb_ref)


def _out_proj_mlp(a, x, amod, wo, mg, mb, mod, w1, w2, pg, pb):
    bsz, s, d = x.shape
    da = a.shape[2]
    d_ff = w1.shape[1]
    ts = SEQ_TILE
    row = lambda v: v.reshape(1, -1)
    return pl.pallas_call(
        _out_proj_mlp_kernel,
        grid=(bsz, s // ts),
        in_specs=[
            pl.BlockSpec((1, ts, da), lambda b, i: (b, i, 0)),
            pl.BlockSpec((1, ts, d), lambda b, i: (b, i, 0)),
            pl.BlockSpec((1, 3, d), lambda b, i: (b, 0, 0)),
            _const_spec((da, d), True),
            _const_spec((1, d)),
            _const_spec((1, d)),
            pl.BlockSpec((1, 3, d), lambda b, i: (b, 0, 0)),
            _const_spec((d, d_ff), True),
            _const_spec((d_ff, d), True),
            _const_spec((1, d)),
            _const_spec((1, d)),
        ],
        out_specs=pl.BlockSpec((1, ts, d), lambda b, i: (b, i, 0)),
        out_shape=jax.ShapeDtypeStruct((bsz, s, d), F32),
        compiler_params=pltpu.CompilerParams(
            dimension_semantics=("arbitrary", "arbitrary"),
            vmem_limit_bytes=_vmem_limit(54 * 1024 * 1024)),
        name="out_proj_mlp",
    )(a, x, amod, wo.astype(BF16), row(mg), row(mb),
      mod, w1.astype(BF16), w2.astype(BF16), row(pg), row(pb))


def kernel(x, c, conv_mod_w, conv_mod_b, conv_pw1_w, conv_pw1_b, conv_dw_w, conv_dw_b, conv_norm_g, conv_norm_b, conv_pw2_w, conv_pw2_b, attn_mod_w, attn_mod_b, attn_qkv_w, attn_lam_q1, attn_lam_k1, attn_lam_q2, attn_lam_k2, attn_subln_g, attn_out_w, rel_bias, mlp_mod_w, mlp_mod_b, mlp_w1, mlp_w2, post_mix_g, post_mix_b, post_mlp_g, post_mlp_b):
    assert x.shape[1] % SEQ_TILE == 0 and x.shape[1] % ATTN_TILE == 0
    assert x.shape[1] % CONV_TILE == 0 and CONV_TILE % CONV_ROWS == 0 and CONV_HALO >= CONV_WIDTH - 1
    conv_mod = _ada_mod(c, conv_mod_w, conv_mod_b)
    attn_mod = _ada_mod(c, attn_mod_w, attn_mod_b)
    mlp_mod = _ada_mod(c, mlp_mod_w, mlp_mod_b)

    x = _conv_mixer(x, conv_mod[0], conv_pw1_w[0], conv_pw1_b[0], conv_dw_w[0], conv_dw_b[0],
                    conv_norm_g[0], conv_norm_b[0], conv_pw2_w[0], conv_pw2_b[0],
                    post_mix_g[0], post_mix_b[0])
    x = _mlp(x, mlp_mod[0], mlp_w1[0], mlp_w2[0], post_mlp_g[0], post_mlp_b[0])

    qc, k, vt = _qkv(x, attn_mod[0], attn_qkv_w[0])
    bias = _bias_tiles(rel_bias)
    lam_vecs = jnp.stack([attn_lam_q1[0], attn_lam_k1[0], attn_lam_q2[0], attn_lam_k2[0]])
    a = _attention(qc, k, vt, bias, lam_vecs, attn_subln_g[0])
    return _out_proj_mlp(a, x, attn_mod[0], attn_out_w[0], post_mix_g[1], post_mix_b[1],
                         mlp_mod[1], mlp_w1[1], mlp_w2[1], post_mlp_g[1], post_mlp_b[1])
```

```python
import functools
import math

import jax
import jax.numpy as jnp
import numpy as np
from jax import lax
from jax.experimental import pallas as pl
from jax.experimental.pallas import tpu as pltpu

DEPTH = 2
CONV_WIDTH = 31
N_HEADS = 8
HEAD_DIM = 64
V_DIM = 2 * HEAD_DIM
REL_BUCKETS = 32
REL_MAX_DIST = 128
ALPHA = (2 * DEPTH) ** 0.25
LN_EPS = 1e-5
ATTN_LAYER = 1
LAMBDA_INIT = 0.8 - 0.6 * math.exp(-0.3 * ATTN_LAYER)
LOG2_E = math.log2(math.e)

V7X_SUBLANES = 8
V7X_LANES = 128
V7X_VMEM_BYTES = 64 * 1024 * 1024
V7X_VMEM_RESERVE_BYTES = 8 * 1024 * 1024

MASK_VALUE = -1e30
CONV_HALO = 32
CONV_ROWS = 512
SEQ_TILE = 1024
CONV_TILE = 1024
ATTN_TILE = 512
FF_CHUNK = 1024

F32 = jnp.float32
BF16 = jnp.bfloat16


def _t5_thresholds():
    max_exact = REL_BUCKETS // 2
    buckets = []
    for n in range(2 * REL_MAX_DIST):
        if n < max_exact:
            buckets.append(n)
        else:
            v = math.log(n / max_exact) / math.log(REL_MAX_DIST / max_exact) * (REL_BUCKETS - max_exact)
            buckets.append(min(max_exact + int(v), REL_BUCKETS - 1))
    assert all(b1 >= b0 for b0, b1 in zip(buckets, buckets[1:]))
    assert buckets[-1] == REL_BUCKETS - 1
    return [buckets.index(b) for b in range(REL_BUCKETS)]


T5_THRESHOLDS = _t5_thresholds()
T5_LAST_BUCKET_START = T5_THRESHOLDS[REL_BUCKETS - 1]


def _vmem_limit(nbytes):
    return int(min(nbytes, V7X_VMEM_BYTES - V7X_VMEM_RESERVE_BYTES))


def _layer_norm(z, g, b):
    mu = jnp.mean(z, axis=-1, keepdims=True)
    zc = z - mu
    var = jnp.mean(zc * zc, axis=-1, keepdims=True)
    return zc * lax.rsqrt(var + LN_EPS) * g + b


def _const_spec(shape, single_buffer=False):
    nd = len(shape)
    kwargs = {"pipeline_mode": pl.Buffered(1)} if single_buffer else {}
    return pl.BlockSpec(shape, lambda *_: (0,) * nd, **kwargs)


def _ada_mod_kernel(c_ref, *refs, firsts):
    o_ref = refs[-1]
    c = c_ref[...]
    sc = c * jax.nn.sigmoid(c)
    i = pl.program_id(0)
    for g in range(len(firsts) - 1):
        w_ref, b_ref = refs[2 * g], refs[2 * g + 1]

        @pl.when((i >= firsts[g]) & (i < firsts[g + 1]))
        def _(w_ref=w_ref, b_ref=b_ref):
            o_ref[0] = jnp.dot(sc, w_ref[0], preferred_element_type=F32,
                               precision=lax.Precision.HIGHEST) + b_ref[0]


def _ada_mod(c, groups):
    d, d3 = groups[0][0].shape[1:]
    bsz = c.shape[0]
    nblk = d3 // d
    counts = [w.shape[0] for w, _ in groups]
    firsts = tuple(sum(counts[:g]) for g in range(len(counts) + 1))

    def held(first, count):
        def index(i, j):
            blk = jnp.where(i < first, 0, jnp.where(i >= first + count, nblk - 1, j))
            return jnp.clip(i - first, 0, count - 1), 0, blk
        return index

    in_specs, operands = [pl.BlockSpec((bsz, d), lambda i, j: (0, 0))], [c]
    for (w, b), first, count in zip(groups, firsts, counts):
        in_specs += [pl.BlockSpec((1, d, d), held(first, count)),
                     pl.BlockSpec((1, 1, d), held(first, count))]
        operands += [w, b.reshape(count, 1, d3)]
    out = pl.pallas_call(
        functools.partial(_ada_mod_kernel, firsts=firsts),
        grid=(firsts[-1], nblk),
        in_specs=in_specs,
        out_specs=pl.BlockSpec((1, bsz, d), lambda i, j: (i, 0, j)),
        out_shape=jax.ShapeDtypeStruct((firsts[-1], bsz, d3), F32),
        compiler_params=pltpu.CompilerParams(
            dimension_semantics=("arbitrary", "arbitrary"),
            vmem_limit_bytes=_vmem_limit(40 * 1024 * 1024)),
        name="ada_mod",
    )(*operands)
    out = out.reshape(firsts[-1], bsz, 3, d)
    return [out[lo:hi] for lo, hi in zip(firsts[:-1], firsts[1:])]


def _conv_mixer_kernel(x_ref, mod_ref, w1_ref, b1_ref, wdw_ref, bdw_ref, gcn_ref, bcn_ref,
                       w2_ref, b2_ref, pg_ref, pb_ref, o_ref, ext_ref, y_ref):
    ts, d = x_ref.shape[1], x_ref.shape[2]
    n_lb = ext_ref.shape[0]
    s_idx = pl.program_id(1)

    @pl.when(s_idx == 0)
    def _():
        ext_ref[:, 0:CONV_HALO, :] = jnp.zeros((n_lb, CONV_HALO, V7X_LANES), F32)

    mod = mod_ref[0]
    shift, scale, gate = mod[0:1], mod[1:2], mod[2:3]
    x = x_ref[0]
    h = (x * (1.0 + scale) + shift).astype(BF16)
    a = jnp.dot(h, w1_ref[...], preferred_element_type=F32) + b1_ref[...]
    u = a[:, :d] * jax.nn.sigmoid(a[:, d:])
    for lb in range(n_lb):
        ext_ref[lb, CONV_HALO:CONV_HALO + ts, :] = u[:, lb * V7X_LANES:(lb + 1) * V7X_LANES]

    off0 = CONV_HALO - (CONV_WIDTH - 1)

    n_grp = CONV_ROWS // V7X_SUBLANES

    def row_block(rb, carry):
        t0 = pl.multiple_of(rb * CONV_ROWS, CONV_ROWS)

        def lane_block(lb, carry2):
            w_all = wdw_ref[lb]
            acc = [jnp.broadcast_to(bdw_ref[lb], (V7X_SUBLANES, V7X_LANES))] * n_grp
            for r in range(V7X_SUBLANES):
                taps = [j for j in range(CONV_WIDTH) if (off0 + j) % V7X_SUBLANES == r]
                first = off0 + taps[0]
                n_load = n_grp + (taps[-1] - taps[0]) // V7X_SUBLANES
                groups = [ext_ref[lb, pl.ds(t0 + (first + V7X_SUBLANES * g), V7X_SUBLANES), :]
                          for g in range(n_load)]
                for j in taps:
                    w_row = jnp.broadcast_to(w_all[j:j + 1], (V7X_SUBLANES, V7X_LANES))
                    g0 = (off0 + j - first) // V7X_SUBLANES
                    acc = [acc[i] + w_row * groups[g0 + i] for i in range(n_grp)]
            y_ref[lb, pl.ds(t0, CONV_ROWS), :] = jnp.concatenate(acc, axis=0)
            return carry2

        lax.fori_loop(0, n_lb, lane_block, 0)
        return carry

    lax.fori_loop(0, ts // CONV_ROWS, row_block, 0)
    ext_ref[:, 0:CONV_HALO, :] = ext_ref[:, ts:ts + CONV_HALO, :]

    y = y_ref[...]
    mu = jnp.sum(jnp.sum(y, axis=0, keepdims=True), axis=2, keepdims=True) * (1.0 / d)
    yc = y - mu
    var = jnp.sum(jnp.sum(yc * yc, axis=0, keepdims=True), axis=2, keepdims=True) * (1.0 / d)
    y = yc * lax.rsqrt(var + LN_EPS) * gcn_ref[...] + bcn_ref[...]
    y = (y * jax.nn.sigmoid(y)).astype(BF16)
    v = jnp.concatenate([y[lb] for lb in range(n_lb)], axis=-1)
    y = jnp.dot(v, w2_ref[...], preferred_element_type=F32) + b2_ref[...]
    o_ref[0] = _layer_norm(ALPHA * x_ref[0] + gate * y, pg_ref[...], pb_ref[...])


def _conv_mixer(x, mod, w1, b1, wdw, bdw, gcn, bcn, w2, b2, pg, pb):
    bsz, s, d = x.shape
    ts = CONV_TILE
    n_lb = d // V7X_LANES
    row = lambda v: v.reshape(1, -1)
    slab = lambda v: v.reshape(-1, n_lb, V7X_LANES).transpose(1, 0, 2)
    return pl.pallas_call(
        _conv_mixer_kernel,
        grid=(bsz, s // ts),
        in_specs=[
            pl.BlockSpec((1, ts, d), lambda b, i: (b, i, 0)),
            pl.BlockSpec((1, 3, d), lambda b, i: (b, 0, 0)),
            _const_spec((d, 2 * d), True),
            _const_spec((1, 2 * d)),
            _const_spec((n_lb, CONV_WIDTH, V7X_LANES)),
            _const_spec((n_lb, 1, V7X_LANES)),
            _const_spec((n_lb, 1, V7X_LANES)),
            _const_spec((n_lb, 1, V7X_LANES)),
            _const_spec((d, d), True),
            _const_spec((1, d)),
            _const_spec((1, d)),
            _const_spec((1, d)),
        ],
        out_specs=pl.BlockSpec((1, ts, d), lambda b, i: (b, i, 0)),
        out_shape=jax.ShapeDtypeStruct((bsz, s, d), F32),
        scratch_shapes=[
            pltpu.VMEM((n_lb, ts + CONV_HALO, V7X_LANES), F32),
            pltpu.VMEM((n_lb, ts, V7X_LANES), F32),
        ],
        compiler_params=pltpu.CompilerParams(
            dimension_semantics=("arbitrary", "arbitrary"),
            vmem_limit_bytes=_vmem_limit(56 * 1024 * 1024)),
        name="conv_mixer",
    )(x, mod, w1.astype(BF16), row(b1), slab(wdw), slab(bdw), slab(gcn), slab(bcn),
      w2.astype(BF16), row(b2), row(pg), row(pb))


def _mlp_sublayer(x, mod, w1_ref, w2_ref, pg_ref, pb_ref):
    d_ff = w1_ref.shape[1]
    shift, scale, gate = mod[0:1], mod[1:2], mod[2:3]
    h = (x * (1.0 + scale) + shift).astype(BF16)
    y = None
    for c0 in range(0, d_ff, FF_CHUNK):
        a = jnp.dot(h, w1_ref[:, c0:c0 + FF_CHUNK], preferred_element_type=F32)
        a = jnp.maximum(a, 0.0)
        part = jnp.dot((a * a).astype(BF16), w2_ref[c0:c0 + FF_CHUNK, :], preferred_element_type=F32)
        y = part if y is None else y + part
    return _layer_norm(ALPHA * x + gate * y, pg_ref[...], pb_ref[...])


def _mlp_kernel(x_ref, mod_ref, w1_ref, w2_ref, pg_ref, pb_ref, o_ref):
    o_ref[0] = _mlp_sublayer(x_ref[0], mod_ref[0], w1_ref, w2_ref, pg_ref, pb_ref)


def _mlp(x, mod, w1, w2, pg, pb):
    bsz, s, d = x.shape
    d_ff = w1.shape[1]
    ts = SEQ_TILE
    row = lambda v: v.reshape(1, -1)
    return pl.pallas_call(
        _mlp_kernel,
        grid=(bsz, s // ts),
        in_specs=[
            pl.BlockSpec((1, ts, d), lambda b, i: (b, i, 0)),
            pl.BlockSpec((1, 3, d), lambda b, i: (b, 0, 0)),
            _const_spec((d, d_ff), True),
            _const_spec((d_ff, d), True),
            _const_spec((1, d)),
            _const_spec((1, d)),
        ],
        out_specs=pl.BlockSpec((1, ts, d), lambda b, i: (b, i, 0)),
        out_shape=jax.ShapeDtypeStruct((bsz, s, d), F32),
        compiler_params=pltpu.CompilerParams(
            dimension_semantics=("arbitrary", "arbitrary"),
            vmem_limit_bytes=_vmem_limit(52 * 1024 * 1024)),
        name="sq_relu_mlp",
    )(x, mod, w1.astype(BF16), w2.astype(BF16), row(pg), row(pb))


def _qkv_kernel(x_ref, mod_ref, wq_ref, wk_ref, wvt_ref, qc_ref, k_ref, vt_ref):
    mod = mod_ref[0]
    shift, scale = mod[0:1], mod[1:2]
    h = (x_ref[0] * (1.0 + scale) + shift).astype(BF16)
    q = (jnp.dot(h, wq_ref[...], preferred_element_type=F32) * (HEAD_DIM ** -0.5 * LOG2_E)).astype(BF16)
    n_cb, cb, hw = qc_ref.shape[3:]
    lane = lax.broadcasted_iota(jnp.int32, (q.shape[0], hw), 1)
    for hd in range(qc_ref.shape[2]):
        qh = q[:, hd * hw:(hd + 1) * hw]
        zero = jnp.zeros_like(qh)
        qcat = jnp.concatenate([jnp.where(lane < HEAD_DIM, qh, zero),
                                jnp.where(lane >= HEAD_DIM, qh, zero)], axis=0)
        qc_ref[0, 0, hd] = qcat.reshape(n_cb, cb, hw)
    k_ref[0] = jnp.dot(h, wk_ref[...], preferred_element_type=F32).astype(BF16)
    vt = lax.dot_general(wvt_ref[...], h, (((1,), (1,)), ((), ())), preferred_element_type=F32)
    vt_ref[0, 0] = vt.astype(BF16)


def _qkv(x, mod, w_qkv):
    bsz, s, d = x.shape
    tk = ATTN_TILE
    cb, hw = ATTN_COL_BLOCK, 2 * HEAD_DIM
    n_cb = 2 * tk // cb
    dq = N_HEADS * hw
    dv = N_HEADS * V_DIM
    wq = w_qkv[:, :dq].astype(BF16)
    wk = w_qkv[:, dq:2 * dq].astype(BF16)
    wvt = w_qkv[:, 2 * dq:].T.astype(BF16)
    return pl.pallas_call(
        _qkv_kernel,
        grid=(bsz, s // tk),
        in_specs=[
            pl.BlockSpec((1, tk, d), lambda b, i: (b, i, 0)),
            pl.BlockSpec((1, 3, d), lambda b, i: (b, 0, 0)),
            _const_spec((d, dq), True),
            _const_spec((d, dq), True),
            _const_spec((dv, d), True),
        ],
        out_specs=[
            pl.BlockSpec((1, 1, N_HEADS, n_cb, cb, hw), lambda b, i: (b, i, 0, 0, 0, 0)),
            pl.BlockSpec((1, tk, dq), lambda b, i: (b, i, 0)),
            pl.BlockSpec((1, 1, dv, tk), lambda b, i: (b, i, 0, 0)),
        ],
        out_shape=[
            jax.ShapeDtypeStruct((bsz, s // tk, N_HEADS, n_cb, cb, hw), BF16),
            jax.ShapeDtypeStruct((bsz, s, dq), BF16),
            jax.ShapeDtypeStruct((bsz, s // tk, dv, tk), BF16),
        ],
        compiler_params=pltpu.CompilerParams(
            dimension_semantics=("arbitrary", "arbitrary"),
            vmem_limit_bytes=_vmem_limit(40 * 1024 * 1024)),
        name="qkv_proj",
    )(x, mod, wq, wk, wvt)


def _bias_tiles_kernel(rb_ref, o_ref):
    h = pl.program_id(0)
    t = o_ref.shape[2]
    far = rb_ref[h, REL_BUCKETS - 1]
    kk = lax.broadcasted_iota(jnp.int32, (t, t), 0)
    qq = lax.broadcasted_iota(jnp.int32, (t, t), 1)
    for dlt in range(o_ref.shape[1]):
        rel = dlt * t + qq - kk
        val = jnp.full((t, t), (rb_ref[h, 0] - far) * LOG2_E, F32)
        for bkt in range(1, REL_BUCKETS):
            val = jnp.where(rel >= T5_THRESHOLDS[bkt], (rb_ref[h, bkt] - far) * LOG2_E, val)
        o_ref[0, dlt] = jnp.where(rel >= 0, val, MASK_VALUE)


def _bias_tiles(rel_bias):
    t = ATTN_TILE
    n_near = -(-(T5_LAST_BUCKET_START + t - 1) // t)
    return pl.pallas_call(
        _bias_tiles_kernel,
        grid=(N_HEADS,),
        in_specs=[pl.BlockSpec(memory_space=pltpu.SMEM)],
        out_specs=pl.BlockSpec((1, n_near, t, t), lambda h: (h, 0, 0, 0)),
        out_shape=jax.ShapeDtypeStruct((N_HEADS, n_near, t, t), F32),
        compiler_params=pltpu.CompilerParams(dimension_semantics=("arbitrary",)),
        name="t5_bias_tiles",
    )(rel_bias.T)


_TAB_QI, _TAB_KJ = range(2)
_PIPE_DEPTH = 1
_UNROLL_CHOICES = (20, 18, 16, 14, 12)
ATTN_COL_BLOCK = 256
ATTN_STAGE_SKEW = 1


def _attn_schedule(pairs, inert):
    cols = [inert] * _PIPE_DEPTH + list(pairs) + [inert] * _PIPE_DEPTH
    n_ticks = len(cols) - _PIPE_DEPTH
    unroll = min(_UNROLL_CHOICES, key=lambda u: (-n_ticks % u, -u))
    cols += [inert] * (-n_ticks % unroll)
    return np.asarray(cols, np.int32).T, unroll


_NEAR_PERIOD = 2


def _near_distance(col):
    return (col + 1) % _NEAR_PERIOD


def _near_extent(dist, c, t, cb):
    if dist is None:
        return t, t
    q_lo = (c * cb) % t
    if dist == 0:
        return min(t, q_lo + cb), 0
    lo = dist * t + q_lo - T5_LAST_BUCKET_START + 1
    return t, min(t, max(0, lo // V7X_SUBLANES * V7X_SUBLANES))


def _attn_schedules(nq, n_near):
    assert n_near == _NEAR_PERIOD
    far = [(qi, kj) for qi in range(nq) for kj in range(qi - n_near + 1)]
    near = [(qi, qi - d) for qi in range(nq) for d in reversed(range(min(n_near, qi + 1)))]
    assert all(qi - kj == _near_distance(_PIPE_DEPTH + i) for i, (qi, kj) in enumerate(near))
    near_tab, near_unroll = _attn_schedule(near, (nq, 0))
    visited = range(near_tab.shape[1] - _PIPE_DEPTH)
    assert all(near_tab[_TAB_QI, j] < nq for j in visited if _near_distance(j) == 0)
    return _attn_schedule(far, (nq, 0)), (near_tab, near_unroll)


def _attn_kernel(far_ref, near_ref, qc_ref, k_ref, vt_ref, bias_ref, lam_ref, g_ref, o_ref,
                 s0_ref, s1_ref, mt0_ref, mt1_ref, m_ref, l_ref, acc_ref,
                 *, far_unroll, near_unroll):
    s_refs, mt_refs = (s0_ref, s1_ref), (mt0_ref, mt1_ref)
    n_cb, t, cb = s0_ref.shape
    nq = m_ref.shape[0] - 1

    s_refs[1][...] = jnp.zeros(s_refs[1].shape, F32)
    mt_refs[1][...] = jnp.zeros(mt_refs[1].shape, F32)
    m_ref[...] = jnp.full(m_ref.shape, MASK_VALUE, F32)
    l_ref[...] = jnp.zeros(l_ref.shape, F32)
    acc_ref[...] = jnp.zeros(acc_ref.shape, F32)

    def q_tile_index(qi):
        return jnp.where(qi == nq, 0, qi)

    lam_v = lam_ref[...]
    lam = (jnp.exp(jnp.sum(lam_v[0:1] * lam_v[1:2], axis=-1, keepdims=True))
           - jnp.exp(jnp.sum(lam_v[2:3] * lam_v[3:4], axis=-1, keepdims=True)) + LAMBDA_INIT)

    def finish(qi):
        def normalised(c):
            return acc_ref[qi, c] * (1.0 / l_ref[qi, c])
        half = n_cb // 2
        o = jnp.concatenate([normalised(c) - lam * normalised(half + c) for c in range(half)],
                            axis=1).T
        o = (o * lax.rsqrt(jnp.mean(o * o, axis=-1, keepdims=True) + LN_EPS)
             * g_ref[...] * (1.0 - LAMBDA_INIT))
        o_ref[0, pl.ds(pl.multiple_of(qi * t, t), t), :] = o.astype(BF16)

    def tick(tab_ref, n, u, near):
        col_a, col_b = n + 1, n
        par, cur = u % 2, 1 - u % 2
        dist_a, dist_b = (_near_distance(u + 1), _near_distance(u)) if near else (None, None)

        qi_a = q_tile_index(tab_ref[_TAB_QI, col_a])
        qi_b = tab_ref[_TAB_QI, col_b]
        vt = vt_ref[0, tab_ref[_TAB_KJ, col_b]]
        k = k_ref[0, pl.ds(pl.multiple_of(tab_ref[_TAB_KJ, col_a] * t, t), t), :]

        def stage_a(c):
            keys_a, bias_lo = _near_extent(dist_a, c, t, cb)
            st = lax.dot_general(k[:keys_a], qc_ref[0, qi_a, 0, c], (((1,), (1,)), ((), ())),
                                 preferred_element_type=F32)
            if bias_lo < keys_a:
                bias = bias_ref[0, dist_a, bias_lo:keys_a, pl.ds((c * cb) % t, cb)]
                st = jnp.concatenate([st[:bias_lo], st[bias_lo:] + bias], axis=0) if bias_lo else st + bias
            s_refs[par][c, 0:keys_a] = st
            mt_refs[par][c] = jnp.max(st, axis=0, keepdims=True)

        def stage_b(c):
            keys_b, _ = _near_extent(dist_b, c, t, cb)
            m_old = m_ref[qi_b, c]
            m_new = jnp.maximum(m_old, mt_refs[cur][c])
            m_ref[qi_b, c] = m_new
            p = jnp.exp2(s_refs[cur][c, 0:keys_b] - m_new)
            pv = jnp.dot(vt[:, :keys_b], p.astype(BF16), preferred_element_type=F32)
            alpha = jnp.exp2(m_old - m_new)
            acc_ref[qi_b, c] = alpha * acc_ref[qi_b, c] + pv
            l_ref[qi_b, c] = alpha * l_ref[qi_b, c] + jnp.sum(p, axis=0, keepdims=True)

        for c in range(n_cb + ATTN_STAGE_SKEW):
            if c < n_cb:
                stage_a(c)
            if c >= ATTN_STAGE_SKEW:
                stage_b(c - ATTN_STAGE_SKEW)
        if dist_b == 0:
            finish(qi_b)

    def run(tab_ref, unroll, near):
        n_ticks = tab_ref.shape[1] - _PIPE_DEPTH
        assert n_ticks % unroll == 0 and unroll % _NEAR_PERIOD == 0

        def body(i, carry):
            for u in range(unroll):
                tick(tab_ref, unroll * i + u, u, near)
            return carry

        lax.fori_loop(0, n_ticks // unroll, body, 0)

    run(far_ref, far_unroll, False)
    run(near_ref, near_unroll, True)


def _attention(qc, k, vt, bias, lam_vecs, g_sub):
    bsz, nq, _, n_cb, cb, hw = qc.shape
    s = k.shape[1]
    t = s // nq
    n_near = bias.shape[1]
    (far_tab, far_unroll), (near_tab, near_unroll) = _attn_schedules(nq, n_near)
    return pl.pallas_call(
        functools.partial(_attn_kernel, far_unroll=far_unroll, near_unroll=near_unroll),
        grid=(bsz, N_HEADS),
        in_specs=[
            pl.BlockSpec(memory_space=pltpu.SMEM),
            pl.BlockSpec(memory_space=pltpu.SMEM),
            pl.BlockSpec((1, nq, 1, n_cb, cb, hw), lambda b, h: (b, 0, h, 0, 0, 0)),
            pl.BlockSpec((1, s, hw), lambda b, h: (b, 0, h)),
            pl.BlockSpec((1, nq, V_DIM, t), lambda b, h: (b, 0, h, 0)),
            pl.BlockSpec((1, n_near, t, t), lambda b, h: (h, 0, 0, 0)),
            _const_spec((4, HEAD_DIM)),
            _const_spec((1, V_DIM)),
        ],
        out_specs=pl.BlockSpec((1, s, V_DIM), lambda b, h: (b, 0, h)),
        out_shape=jax.ShapeDtypeStruct((bsz, s, N_HEADS * V_DIM), BF16),
        scratch_shapes=[
            pltpu.VMEM((n_cb, t, cb), F32),
            pltpu.VMEM((n_cb, t, cb), F32),
            pltpu.VMEM((n_cb, 1, cb), F32),
            pltpu.VMEM((n_cb, 1, cb), F32),
            pltpu.VMEM((nq + 1, n_cb, 1, cb), F32),
            pltpu.VMEM((nq + 1, n_cb, 1, cb), F32),
            pltpu.VMEM((nq + 1, n_cb, V_DIM, cb), F32),
        ],
        compiler_params=pltpu.CompilerParams(
            dimension_semantics=("arbitrary", "arbitrary"),
            vmem_limit_bytes=_vmem_limit(52 * 1024 * 1024)),
        name="diff_attention",
    )(jnp.asarray(far_tab), jnp.asarray(near_tab), qc, k, vt, bias, lam_vecs,
      g_sub.reshape(1, V_DIM))


def _out_proj_mlp_kernel(a_ref, x_ref, amod_ref, wo_ref, mg_ref, mb_ref,
                         mod_ref, w1_ref, w2_ref, pg_ref, pb_ref, o_ref):
    gate = amod_ref[0][2:3]
    y = jnp.dot(a_ref[0], wo_ref[...], preferred_element_type=F32)
    x = _layer_norm(ALPHA * x_ref[0] + gate * y, mg_ref[...], mb_ref[...])
    o_ref[0] = _mlp_sublayer(x, mod_ref[0], w1_ref, w2_ref, pg_ref, pb_ref)


def _out_proj_mlp(a, x, amod, wo, mg, mb, mod, w1, w2, pg, pb):
    bsz, s, d = x.shape
    da = a.shape[2]
    d_ff = w1.shape[1]
    ts = SEQ_TILE
    row = lambda v: v.reshape(1, -1)
    return pl.pallas_call(
        _out_proj_mlp_kernel,
        grid=(bsz, s // ts),
        in_specs=[
            pl.BlockSpec((1, ts, da), lambda b, i: (b, i, 0)),
            pl.BlockSpec((1, ts, d), lambda b, i: (b, i, 0)),
            pl.BlockSpec((1, 3, d), lambda b, i: (b, 0, 0)),
            _const_spec((da, d), True),
            _const_spec((1, d)),
            _const_spec((1, d)),
            pl.BlockSpec((1, 3, d), lambda b, i: (b, 0, 0)),
            _const_spec((d, d_ff), True),
            _const_spec((d_ff, d), True),
            _const_spec((1, d)),
            _const_spec((1, d)),
        ],
        out_specs=pl.BlockSpec((1, ts, d), lambda b, i: (b, i, 0)),
        out_shape=jax.ShapeDtypeStruct((bsz, s, d), F32),
        compiler_params=pltpu.CompilerParams(
            dimension_semantics=("arbitrary", "arbitrary"),
            vmem_limit_bytes=_vmem_limit(54 * 1024 * 1024)),
        name="out_proj_mlp",
    )(a, x, amod, wo.astype(BF16), row(mg), row(mb),
      mod, w1.astype(BF16), w2.astype(BF16), row(pg), row(pb))


def kernel(x, c, conv_mod_w, conv_mod_b, conv_pw1_w, conv_pw1_b, conv_dw_w, conv_dw_b, conv_norm_g, conv_norm_b, conv_pw2_w, conv_pw2_b, attn_mod_w, attn_mod_b, attn_qkv_w, attn_lam_q1, attn_lam_k1, attn_lam_q2, attn_lam_k2, attn_subln_g, attn_out_w, rel_bias, mlp_mod_w, mlp_mod_b, mlp_w1, mlp_w2, post_mix_g, post_mix_b, post_mlp_g, post_mlp_b):
    assert x.shape[1] % SEQ_TILE == 0 and x.shape[1] % ATTN_TILE == 0
    assert x.shape[1] % CONV_TILE == 0 and CONV_TILE % CONV_ROWS == 0 and CONV_HALO >= CONV_WIDTH - 1
    conv_mod, attn_mod, mlp_mod = _ada_mod(
        c, [(conv_mod_w, conv_mod_b), (attn_mod_w, attn_mod_b), (mlp_mod_w, mlp_mod_b)])

    x = _conv_mixer(x, conv_mod[0], conv_pw1_w[0], conv_pw1_b[0], conv_dw_w[0], conv_dw_b[0],
                    conv_norm_g[0], conv_norm_b[0], conv_pw2_w[0], conv_pw2_b[0],
                    post_mix_g[0], post_mix_b[0])
    x = _mlp(x, mlp_mod[0], mlp_w1[0], mlp_w2[0], post_mlp_g[0], post_mlp_b[0])

    qc, k, vt = _qkv(x, attn_mod[0], attn_qkv_w[0])
    bias = _bias_tiles(rel_bias)
    lam_vecs = jnp.stack([attn_lam_q1[0], attn_lam_k1[0], attn_lam_q2[0], attn_lam_k2[0]])
    a = _attention(qc, k, vt, bias, lam_vecs, attn_subln_g[0])
    return _out_proj_mlp(a, x, attn_mod[0], attn_out_w[0], post_mix_g[1], post_mix_b[1],
                         mlp_mod[1], mlp_w1[1], mlp_w2[1], post_mlp_g[1], post_mlp_b[1])
```

```python
import functools
import math

import jax
import jax.numpy as jnp
import numpy as np
from jax import lax
from jax.experimental import pallas as pl
from jax.experimental.pallas import tpu as pltpu

DEPTH = 2
CONV_WIDTH = 31
N_HEADS = 8
HEAD_DIM = 64
V_DIM = 2 * HEAD_DIM
REL_BUCKETS = 32
REL_MAX_DIST = 128
ALPHA = (2 * DEPTH) ** 0.25
LN_EPS = 1e-5
ATTN_LAYER = 1
LAMBDA_INIT = 0.8 - 0.6 * math.exp(-0.3 * ATTN_LAYER)
LOG2_E = math.log2(math.e)

V7X_SUBLANES = 8
V7X_LANES = 128
V7X_VMEM_BYTES = 64 * 1024 * 1024
V7X_VMEM_RESERVE_BYTES = 8 * 1024 * 1024

MASK_VALUE = -1e30
CONV_HALO = 32
CONV_ROWS = 512
SEQ_TILE = 1024
CONV_TILE = 1024
ATTN_TILE = 512
FF_CHUNK = 1024

F32 = jnp.float32
BF16 = jnp.bfloat16


def _t5_thresholds():
    max_exact = REL_BUCKETS // 2
    buckets = []
    for n in range(2 * REL_MAX_DIST):
        if n < max_exact:
            buckets.append(n)
        else:
            v = math.log(n / max_exact) / math.log(REL_MAX_DIST / max_exact) * (REL_BUCKETS - max_exact)
            buckets.append(min(max_exact + int(v), REL_BUCKETS - 1))
    assert all(b1 >= b0 for b0, b1 in zip(buckets, buckets[1:]))
    assert buckets[-1] == REL_BUCKETS - 1
    return [buckets.index(b) for b in range(REL_BUCKETS)]


T5_THRESHOLDS = _t5_thresholds()
T5_LAST_BUCKET_START = T5_THRESHOLDS[REL_BUCKETS - 1]


def _vmem_limit(nbytes):
    return int(min(nbytes, V7X_VMEM_BYTES - V7X_VMEM_RESERVE_BYTES))


def _layer_norm(z, g, b):
    mu = jnp.mean(z, axis=-1, keepdims=True)
    zc = z - mu
    var = jnp.mean(zc * zc, axis=-1, keepdims=True)
    return zc * lax.rsqrt(var + LN_EPS) * g + b


def _const_spec(shape, single_buffer=False):
    nd = len(shape)
    kwargs = {"pipeline_mode": pl.Buffered(1)} if single_buffer else {}
    return pl.BlockSpec(shape, lambda *_: (0,) * nd, **kwargs)


def _ada_mod_kernel(ct_ref, w_ref, b_ref, o_ref):
    d, bsz = ct_ref.shape
    rows_per_step = V7X_SUBLANES

    def body(r, accs):
        rows = pl.ds(pl.multiple_of(r * rows_per_step, rows_per_step), rows_per_step)
        w = w_ref[0, rows, :]
        ct = ct_ref[rows, :]
        sc = ct * jax.nn.sigmoid(ct)
        return tuple(acc + sc[:, bi:bi + 1] * w for bi, acc in enumerate(accs))

    zero = jnp.zeros((rows_per_step, w_ref.shape[2]), F32)
    accs = lax.fori_loop(0, d // rows_per_step, body, (zero,) * bsz, unroll=8)
    o_ref[0] = jnp.concatenate([jnp.sum(acc, axis=0, keepdims=True) for acc in accs], axis=0) + b_ref[0]


def _ada_mod(c, w, b):
    n, d, d3 = w.shape
    bsz = c.shape[0]
    nblk = d3 // d
    out = pl.pallas_call(
        _ada_mod_kernel,
        grid=(n, nblk),
        in_specs=[
            pl.BlockSpec((d, bsz), lambda i, j: (0, 0)),
            pl.BlockSpec((1, d, d), lambda i, j: (i, 0, j)),
            pl.BlockSpec((1, 1, d), lambda i, j: (i, 0, j)),
        ],
        out_specs=pl.BlockSpec((1, bsz, d), lambda i, j: (i, 0, j)),
        out_shape=jax.ShapeDtypeStruct((n, bsz, d3), F32),
        compiler_params=pltpu.CompilerParams(
            dimension_semantics=("arbitrary", "arbitrary"),
            vmem_limit_bytes=_vmem_limit(32 * 1024 * 1024)),
        name="ada_mod",
    )(c.T, w, b.reshape(n, 1, d3))
    return out.reshape(n, bsz, 3, d)


def _conv_mixer_kernel(x_ref, mod_ref, w1_ref, b1_ref, wdw_ref, bdw_ref, gcn_ref, bcn_ref,
                       w2_ref, b2_ref, pg_ref, pb_ref, o_ref, ext_ref, y_ref):
    ts, d = x_ref.shape[1], x_ref.shape[2]
    n_lb = ext_ref.shape[0]
    s_idx = pl.program_id(1)

    @pl.when(s_idx == 0)
    def _():
        ext_ref[:, 0:CONV_HALO, :] = jnp.zeros((n_lb, CONV_HALO, V7X_LANES), F32)

    mod = mod_ref[0]
    shift, scale, gate = mod[0:1], mod[1:2], mod[2:3]
    x = x_ref[0]
    h = (x * (1.0 + scale) + shift).astype(BF16)
    a = jnp.dot(h, w1_ref[...], preferred_element_type=F32) + b1_ref[...]
    u = a[:, :d] * jax.nn.sigmoid(a[:, d:])
    for lb in range(n_lb):
        ext_ref[lb, CONV_HALO:CONV_HALO + ts, :] = u[:, lb * V7X_LANES:(lb + 1) * V7X_LANES]

    off0 = CONV_HALO - (CONV_WIDTH - 1)

    n_grp = CONV_ROWS // V7X_SUBLANES

    def row_block(rb, carry):
        t0 = pl.multiple_of(rb * CONV_ROWS, CONV_ROWS)

        def lane_block(lb, carry2):
            w_all = wdw_ref[lb]
            acc = [jnp.broadcast_to(bdw_ref[lb], (V7X_SUBLANES, V7X_LANES))] * n_grp
            for r in range(V7X_SUBLANES):
                taps = [j for j in range(CONV_WIDTH) if (off0 + j) % V7X_SUBLANES == r]
                first = off0 + taps[0]
                n_load = n_grp + (taps[-1] - taps[0]) // V7X_SUBLANES
                groups = [ext_ref[lb, pl.ds(t0 + (first + V7X_SUBLANES * g), V7X_SUBLANES), :]
                          for g in range(n_load)]
                for j in taps:
                    w_row = jnp.broadcast_to(w_all[j:j + 1], (V7X_SUBLANES, V7X_LANES))
                    g0 = (off0 + j - first) // V7X_SUBLANES
                    acc = [acc[i] + w_row * groups[g0 + i] for i in range(n_grp)]
            y_ref[lb, pl.ds(t0, CONV_ROWS), :] = jnp.concatenate(acc, axis=0)
            return carry2

        lax.fori_loop(0, n_lb, lane_block, 0)
        return carry

    lax.fori_loop(0, ts // CONV_ROWS, row_block, 0)
    ext_ref[:, 0:CONV_HALO, :] = ext_ref[:, ts:ts + CONV_HALO, :]

    y = y_ref[...]
    mu = jnp.sum(jnp.sum(y, axis=0, keepdims=True), axis=2, keepdims=True) * (1.0 / d)
    yc = y - mu
    var = jnp.sum(jnp.sum(yc * yc, axis=0, keepdims=True), axis=2, keepdims=True) * (1.0 / d)
    y = yc * lax.rsqrt(var + LN_EPS) * gcn_ref[...] + bcn_ref[...]
    y = (y * jax.nn.sigmoid(y)).astype(BF16)
    v = jnp.concatenate([y[lb] for lb in range(n_lb)], axis=-1)
    y = jnp.dot(v, w2_ref[...], preferred_element_type=F32) + b2_ref[...]
    o_ref[0] = _layer_norm(ALPHA * x_ref[0] + gate * y, pg_ref[...], pb_ref[...])


def _conv_mixer(x, mod, w1, b1, wdw, bdw, gcn, bcn, w2, b2, pg, pb):
    bsz, s, d = x.shape
    ts = CONV_TILE
    n_lb = d // V7X_LANES
    row = lambda v: v.reshape(1, -1)
    slab = lambda v: v.reshape(-1, n_lb, V7X_LANES).transpose(1, 0, 2)
    return pl.pallas_call(
        _conv_mixer_kernel,
        grid=(bsz, s // ts),
        in_specs=[
            pl.BlockSpec((1, ts, d), lambda b, i: (b, i, 0)),
            pl.BlockSpec((1, 3, d), lambda b, i: (b, 0, 0)),
            _const_spec((d, 2 * d), True),
            _const_spec((1, 2 * d)),
            _const_spec((n_lb, CONV_WIDTH, V7X_LANES)),
            _const_spec((n_lb, 1, V7X_LANES)),
            _const_spec((n_lb, 1, V7X_LANES)),
            _const_spec((n_lb, 1, V7X_LANES)),
            _const_spec((d, d), True),
            _const_spec((1, d)),
            _const_spec((1, d)),
            _const_spec((1, d)),
        ],
        out_specs=pl.BlockSpec((1, ts, d), lambda b, i: (b, i, 0)),
        out_shape=jax.ShapeDtypeStruct((bsz, s, d), F32),
        scratch_shapes=[
            pltpu.VMEM((n_lb, ts + CONV_HALO, V7X_LANES), F32),
            pltpu.VMEM((n_lb, ts, V7X_LANES), F32),
        ],
        compiler_params=pltpu.CompilerParams(
            dimension_semantics=("arbitrary", "arbitrary"),
            vmem_limit_bytes=_vmem_limit(56 * 1024 * 1024)),
        name="conv_mixer",
    )(x, mod, w1.astype(BF16), row(b1), slab(wdw), slab(bdw), slab(gcn), slab(bcn),
      w2.astype(BF16), row(b2), row(pg), row(pb))


def _mlp_sublayer(x, mod, w1_ref, w2_ref, pg_ref, pb_ref):
    d_ff = w1_ref.shape[1]
    shift, scale, gate = mod[0:1], mod[1:2], mod[2:3]
    h = (x * (1.0 + scale) + shift).astype(BF16)
    y = None
    for c0 in range(0, d_ff, FF_CHUNK):
        a = jnp.dot(h, w1_ref[:, c0:c0 + FF_CHUNK], preferred_element_type=F32)
        a = jnp.maximum(a, 0.0)
        part = jnp.dot((a * a).astype(BF16), w2_ref[c0:c0 + FF_CHUNK, :], preferred_element_type=F32)
        y = part if y is None else y + part
    return _layer_norm(ALPHA * x + gate * y, pg_ref[...], pb_ref[...])


def _mlp_kernel(x_ref, mod_ref, w1_ref, w2_ref, pg_ref, pb_ref, o_ref):
    o_ref[0] = _mlp_sublayer(x_ref[0], mod_ref[0], w1_ref, w2_ref, pg_ref, pb_ref)


def _mlp(x, mod, w1, w2, pg, pb):
    bsz, s, d = x.shape
    d_ff = w1.shape[1]
    ts = SEQ_TILE
    row = lambda v: v.reshape(1, -1)
    return pl.pallas_call(
        _mlp_kernel,
        grid=(bsz, s // ts),
        in_specs=[
            pl.BlockSpec((1, ts, d), lambda b, i: (b, i, 0)),
            pl.BlockSpec((1, 3, d), lambda b, i: (b, 0, 0)),
            _const_spec((d, d_ff), True),
            _const_spec((d_ff, d), True),
            _const_spec((1, d)),
            _const_spec((1, d)),
        ],
        out_specs=pl.BlockSpec((1, ts, d), lambda b, i: (b, i, 0)),
        out_shape=jax.ShapeDtypeStruct((bsz, s, d), F32),
        compiler_params=pltpu.CompilerParams(
            dimension_semantics=("arbitrary", "arbitrary"),
            vmem_limit_bytes=_vmem_limit(52 * 1024 * 1024)),
        name="sq_relu_mlp",
    )(x, mod, w1.astype(BF16), w2.astype(BF16), row(pg), row(pb))


def _qkv_kernel(x_ref, mod_ref, wq_ref, wk_ref, wvt_ref, qc_ref, k_ref, vt_ref):
    mod = mod_ref[0]
    shift, scale = mod[0:1], mod[1:2]
    h = (x_ref[0] * (1.0 + scale) + shift).astype(BF16)
    q = (jnp.dot(h, wq_ref[...], preferred_element_type=F32) * (HEAD_DIM ** -0.5 * LOG2_E)).astype(BF16)
    n_cb, cb, hw = qc_ref.shape[3:]
    lane = lax.broadcasted_iota(jnp.int32, (q.shape[0], hw), 1)
    for hd in range(qc_ref.shape[2]):
        qh = q[:, hd * hw:(hd + 1) * hw]
        zero = jnp.zeros_like(qh)
        qcat = jnp.concatenate([jnp.where(lane < HEAD_DIM, qh, zero),
                                jnp.where(lane >= HEAD_DIM, qh, zero)], axis=0)
        qc_ref[0, 0, hd] = qcat.reshape(n_cb, cb, hw)
    k_ref[0] = jnp.dot(h, wk_ref[...], preferred_element_type=F32).astype(BF16)
    vt = lax.dot_general(wvt_ref[...], h, (((1,), (1,)), ((), ())), preferred_element_type=F32)
    vt_ref[0, 0] = vt.astype(BF16)


def _qkv(x, mod, w_qkv):
    bsz, s, d = x.shape
    tk = ATTN_TILE
    cb, hw = ATTN_COL_BLOCK, 2 * HEAD_DIM
    n_cb = 2 * tk // cb
    dq = N_HEADS * hw
    dv = N_HEADS * V_DIM
    wq = w_qkv[:, :dq].astype(BF16)
    wk = w_qkv[:, dq:2 * dq].astype(BF16)
    wvt = w_qkv[:, 2 * dq:].T.astype(BF16)
    return pl.pallas_call(
        _qkv_kernel,
        grid=(bsz, s // tk),
        in_specs=[
            pl.BlockSpec((1, tk, d), lambda b, i: (b, i, 0)),
            pl.BlockSpec((1, 3, d), lambda b, i: (b, 0, 0)),
            _const_spec((d, dq), True),
            _const_spec((d, dq), True),
            _const_spec((dv, d), True),
        ],
        out_specs=[
            pl.BlockSpec((1, 1, N_HEADS, n_cb, cb, hw), lambda b, i: (b, i, 0, 0, 0, 0)),
            pl.BlockSpec((1, tk, dq), lambda b, i: (b, i, 0)),
            pl.BlockSpec((1, 1, dv, tk), lambda b, i: (b, i, 0, 0)),
        ],
        out_shape=[
            jax.ShapeDtypeStruct((bsz, s // tk, N_HEADS, n_cb, cb, hw), BF16),
            jax.ShapeDtypeStruct((bsz, s, dq), BF16),
            jax.ShapeDtypeStruct((bsz, s // tk, dv, tk), BF16),
        ],
        compiler_params=pltpu.CompilerParams(
            dimension_semantics=("arbitrary", "arbitrary"),
            vmem_limit_bytes=_vmem_limit(40 * 1024 * 1024)),
        name="qkv_proj",
    )(x, mod, wq, wk, wvt)


def _bias_tiles_kernel(rb_ref, o_ref):
    h = pl.program_id(0)
    t = o_ref.shape[2]
    far = rb_ref[h, REL_BUCKETS - 1]
    kk = lax.broadcasted_iota(jnp.int32, (t, t), 0)
    qq = lax.broadcasted_iota(jnp.int32, (t, t), 1)
    for dlt in range(o_ref.shape[1]):
        rel = dlt * t + qq - kk
        val = jnp.full((t, t), (rb_ref[h, 0] - far) * LOG2_E, F32)
        for bkt in range(1, REL_BUCKETS):
            val = jnp.where(rel >= T5_THRESHOLDS[bkt], (rb_ref[h, bkt] - far) * LOG2_E, val)
        o_ref[0, dlt] = jnp.where(rel >= 0, val, MASK_VALUE)


def _bias_tiles(rel_bias):
    t = ATTN_TILE
    n_near = -(-(T5_LAST_BUCKET_START + t - 1) // t)
    return pl.pallas_call(
        _bias_tiles_kernel,
        grid=(N_HEADS,),
        in_specs=[pl.BlockSpec(memory_space=pltpu.SMEM)],
        out_specs=pl.BlockSpec((1, n_near, t, t), lambda h: (h, 0, 0, 0)),
        out_shape=jax.ShapeDtypeStruct((N_HEADS, n_near, t, t), F32),
        compiler_params=pltpu.CompilerParams(dimension_semantics=("arbitrary",)),
        name="t5_bias_tiles",
    )(rel_bias.T)


_TAB_QI, _TAB_KJ = range(2)
_PIPE_DEPTH = 1
_UNROLL_CHOICES = (20, 18, 16, 14, 12)
ATTN_COL_BLOCK = 256
ATTN_STAGE_SKEW = 1


def _attn_schedule(pairs, inert):
    cols = [inert] * _PIPE_DEPTH + list(pairs) + [inert] * _PIPE_DEPTH
    n_ticks = len(cols) - _PIPE_DEPTH
    unroll = min(_UNROLL_CHOICES, key=lambda u: (-n_ticks % u, -u))
    cols += [inert] * (-n_ticks % unroll)
    return np.asarray(cols, np.int32).T, unroll


_NEAR_PERIOD = 2


def _near_distance(col):
    return (col + 1) % _NEAR_PERIOD


def _near_extent(dist, c, t, cb):
    if dist is None:
        return t, t
    q_lo = (c * cb) % t
    if dist == 0:
        return min(t, q_lo + cb), 0
    lo = dist * t + q_lo - T5_LAST_BUCKET_START + 1
    return t, min(t, max(0, lo // V7X_SUBLANES * V7X_SUBLANES))


def _attn_schedules(nq, n_near):
    assert n_near == _NEAR_PERIOD
    far = [(qi, kj) for qi in range(nq) for kj in range(qi - n_near + 1)]
    near = [(qi, qi - d) for qi in range(nq) for d in reversed(range(min(n_near, qi + 1)))]
    assert all(qi - kj == _near_distance(_PIPE_DEPTH + i) for i, (qi, kj) in enumerate(near))
    near_tab, near_unroll = _attn_schedule(near, (nq, 0))
    visited = range(near_tab.shape[1] - _PIPE_DEPTH)
    assert all(near_tab[_TAB_QI, j] < nq for j in visited if _near_distance(j) == 0)
    return _attn_schedule(far, (nq, 0)), (near_tab, near_unroll)


def _attn_kernel(far_ref, near_ref, qc_ref, k_ref, vt_ref, bias_ref, lam_ref, g_ref, o_ref,
                 s0_ref, s1_ref, mt0_ref, mt1_ref, m_ref, l_ref, acc_ref,
                 *, far_unroll, near_unroll):
    s_refs, mt_refs = (s0_ref, s1_ref), (mt0_ref, mt1_ref)
    n_cb, t, cb = s0_ref.shape
    nq = m_ref.shape[0] - 1

    s_refs[1][...] = jnp.zeros(s_refs[1].shape, F32)
    mt_refs[1][...] = jnp.zeros(mt_refs[1].shape, F32)
    m_ref[...] = jnp.full(m_ref.shape, MASK_VALUE, F32)
    l_ref[...] = jnp.zeros(l_ref.shape, F32)
    acc_ref[...] = jnp.zeros(acc_ref.shape, F32)

    def q_tile_index(qi):
        return jnp.where(qi == nq, 0, qi)

    lam_v = lam_ref[...]
    lam = (jnp.exp(jnp.sum(lam_v[0:1] * lam_v[1:2], axis=-1, keepdims=True))
           - jnp.exp(jnp.sum(lam_v[2:3] * lam_v[3:4], axis=-1, keepdims=True)) + LAMBDA_INIT)

    def finish(qi):
        def normalised(c):
            return acc_ref[qi, c] * (1.0 / l_ref[qi, c])
        half = n_cb // 2
        o = jnp.concatenate([normalised(c) - lam * normalised(half + c) for c in range(half)],
                            axis=1).T
        o = (o * lax.rsqrt(jnp.mean(o * o, axis=-1, keepdims=True) + LN_EPS)
             * g_ref[...] * (1.0 - LAMBDA_INIT))
        o_ref[0, pl.ds(pl.multiple_of(qi * t, t), t), :] = o.astype(BF16)

    def tick(tab_ref, n, u, near):
        col_a, col_b = n + 1, n
        par, cur = u % 2, 1 - u % 2
        dist_a, dist_b = (_near_distance(u + 1), _near_distance(u)) if near else (None, None)

        qi_a = q_tile_index(tab_ref[_TAB_QI, col_a])
        qi_b = tab_ref[_TAB_QI, col_b]
        vt = vt_ref[0, tab_ref[_TAB_KJ, col_b]]
        k = k_ref[0, pl.ds(pl.multiple_of(tab_ref[_TAB_KJ, col_a] * t, t), t), :]

        def stage_a(c):
            keys_a, bias_lo = _near_extent(dist_a, c, t, cb)
            st = lax.dot_general(k[:keys_a], qc_ref[0, qi_a, 0, c], (((1,), (1,)), ((), ())),
                                 preferred_element_type=F32)
            if bias_lo < keys_a:
                bias = bias_ref[0, dist_a, bias_lo:keys_a, pl.ds((c * cb) % t, cb)]
                st = jnp.concatenate([st[:bias_lo], st[bias_lo:] + bias], axis=0) if bias_lo else st + bias
            s_refs[par][c, 0:keys_a] = st
            mt_refs[par][c] = jnp.max(st, axis=0, keepdims=True)

        def stage_b(c):
            keys_b, _ = _near_extent(dist_b, c, t, cb)
            m_old = m_ref[qi_b, c]
            m_new = jnp.maximum(m_old, mt_refs[cur][c])
            m_ref[qi_b, c] = m_new
            p = jnp.exp2(s_refs[cur][c, 0:keys_b] - m_new)
            pv = jnp.dot(vt[:, :keys_b], p.astype(BF16), preferred_element_type=F32)
            alpha = jnp.exp2(m_old - m_new)
            acc_ref[qi_b, c] = alpha * acc_ref[qi_b, c] + pv
            l_ref[qi_b, c] = alpha * l_ref[qi_b, c] + jnp.sum(p, axis=0, keepdims=True)

        for c in range(n_cb + ATTN_STAGE_SKEW):
            if c < n_cb:
                stage_a(c)
            if c >= ATTN_STAGE_SKEW:
                stage_b(c - ATTN_STAGE_SKEW)
        if dist_b == 0:
            finish(qi_b)

    def run(tab_ref, unroll, near):
        n_ticks = tab_ref.shape[1] - _PIPE_DEPTH
        assert n_ticks % unroll == 0 and unroll % _NEAR_PERIOD == 0

        def body(i, carry):
            for u in range(unroll):
                tick(tab_ref, unroll * i + u, u, near)
            return carry

        lax.fori_loop(0, n_ticks // unroll, body, 0)

    run(far_ref, far_unroll, False)
    run(near_ref, near_unroll, True)


def _attention(qc, k, vt, bias, lam_vecs, g_sub):
    bsz, nq, _, n_cb, cb, hw = qc.shape
    s = k.shape[1]
    t = s // nq
    n_near = bias.shape[1]
    (far_tab, far_unroll), (near_tab, near_unroll) = _attn_schedules(nq, n_near)
    return pl.pallas_call(
        functools.partial(_attn_kernel, far_unroll=far_unroll, near_unroll=near_unroll),
        grid=(bsz, N_HEADS),
        in_specs=[
            pl.BlockSpec(memory_space=pltpu.SMEM),
            pl.BlockSpec(memory_space=pltpu.SMEM),
            pl.BlockSpec((1, nq, 1, n_cb, cb, hw), lambda b, h: (b, 0, h, 0, 0, 0)),
            pl.BlockSpec((1, s, hw), lambda b, h: (b, 0, h)),
            pl.BlockSpec((1, nq, V_DIM, t), lambda b, h: (b, 0, h, 0)),
            pl.BlockSpec((1, n_near, t, t), lambda b, h: (h, 0, 0, 0)),
            _const_spec((4, HEAD_DIM)),
            _const_spec((1, V_DIM)),
        ],
        out_specs=pl.BlockSpec((1, s, V_DIM), lambda b, h: (b, 0, h)),
        out_shape=jax.ShapeDtypeStruct((bsz, s, N_HEADS * V_DIM), BF16),
        scratch_shapes=[
            pltpu.VMEM((n_cb, t, cb), F32),
            pltpu.VMEM((n_cb, t, cb), F32),
            pltpu.VMEM((n_cb, 1, cb), F32),
            pltpu.VMEM((n_cb, 1, cb), F32),
            pltpu.VMEM((nq + 1, n_cb, 1, cb), F32),
            pltpu.VMEM((nq + 1, n_cb, 1, cb), F32),
            pltpu.VMEM((nq + 1, n_cb, V_DIM, cb), F32),
        ],
        compiler_params=pltpu.CompilerParams(
            dimension_semantics=("arbitrary", "arbitrary"),
            vmem_limit_bytes=_vmem_limit(52 * 1024 * 1024)),
        name="diff_attention",
    )(jnp.asarray(far_tab), jnp.asarray(near_tab), qc, k, vt, bias, lam_vecs,
      g_sub.reshape(1, V_DIM))


def _out_proj_mlp_kernel(a_ref, x_ref, amod_ref, wo_ref, mg_ref, mb_ref,
                         mod_ref, w1_ref, w2_ref, pg_ref, pb_ref, o_ref):
    gate = amod_ref[0][2:3]
    y = jnp.dot(a_ref[0], wo_ref[...], preferred_element_type=F32)
    x = _layer_norm(ALPHA * x_ref[0] + gate * y, mg_ref[...], mb_ref[...])
    o_ref[0] = _mlp_sublayer(x, mod_ref[0], w1_ref, w2_ref, pg_ref, pb_ref)


def _out_proj_mlp(a, x, amod, wo, mg, mb, mod, w1, w2, pg, pb):
    bsz, s, d = x.shape
    da = a.shape[2]
    d_ff = w1.shape[1]
    ts = SEQ_TILE
    row = lambda v: v.reshape(1, -1)
    return pl.pallas_call(
        _out_proj_mlp_kernel,
        grid=(bsz, s // ts),
        in_specs=[
            pl.BlockSpec((1, ts, da), lambda b, i: (b, i, 0)),
            pl.BlockSpec((1, ts, d), lambda b, i: (b, i, 0)),
            pl.BlockSpec((1, 3, d), lambda b, i: (b, 0, 0)),
            _const_spec((da, d), True),
            _const_spec((1, d)),
            _const_spec((1, d)),
            pl.BlockSpec((1, 3, d), lambda b, i: (b, 0, 0)),
            _const_spec((d, d_ff), True),
            _const_spec((d_ff, d), True),
            _const_spec((1, d)),
            _const_spec((1, d)),
        ],
        out_specs=pl.BlockSpec((1, ts, d), lambda b, i: (b, i, 0)),
        out_shape=jax.ShapeDtypeStruct((bsz, s, d), F32),
        compiler_params=pltpu.CompilerParams(
            dimension_semantics=("arbitrary", "arbitrary"),
            vmem_limit_bytes=_vmem_limit(54 * 1024 * 1024)),
        name="out_proj_mlp",
    )(a, x, amod, wo.astype(BF16), row(mg), row(mb),
      mod, w1.astype(BF16), w2.astype(BF16), row(pg), row(pb))


def kernel(x, c, conv_mod_w, conv_mod_b, conv_pw1_w, conv_pw1_b, conv_dw_w, conv_dw_b, conv_norm_g, conv_norm_b, conv_pw2_w, conv_pw2_b, attn_mod_w, attn_mod_b, attn_qkv_w, attn_lam_q1, attn_lam_k1, attn_lam_q2, attn_lam_k2, attn_subln_g, attn_out_w, rel_bias, mlp_mod_w, mlp_mod_b, mlp_w1, mlp_w2, post_mix_g, post_mix_b, post_mlp_g, post_mlp_b):
    assert x.shape[1] % SEQ_TILE == 0 and x.shape[1] % ATTN_TILE == 0
    assert x.shape[1] % CONV_TILE == 0 and CONV_TILE % CONV_ROWS == 0 and CONV_HALO >= CONV_WIDTH - 1
    conv_mod = _ada_mod(c, conv_mod_w, conv_mod_b)
    attn_mod = _ada_mod(c, attn_mod_w, attn_mod_b)
    mlp_mod = _ada_mod(c, mlp_mod_w, mlp_mod_b)

    x = _conv_mixer(x, conv_mod[0], conv_pw1_w[0], conv_pw1_b[0], conv_dw_w[0], conv_dw_b[0],
                    conv_norm_g[0], conv_norm_b[0], conv_pw2_w[0], conv_pw2_b[0],
                    post_mix_g[0], post_mix_b[0])
    x = _mlp(x, mlp_mod[0], mlp_w1[0], mlp_w2[0], post_mlp_g[0], post_mlp_b[0])

    qc, k, vt = _qkv(x, attn_mod[0], attn_qkv_w[0])
    bias = _bias_tiles(rel_bias)
    lam_vecs = jnp.stack([attn_lam_q1[0], attn_lam_k1[0], attn_lam_q2[0], attn_lam_k2[0]])
    a = _attention(qc, k, vt, bias, lam_vecs, attn_subln_g[0])
    return _out_proj_mlp(a, x, attn_mod[0], attn_out_w[0], post_mix_g[1], post_mix_b[1],
                         mlp_mod[1], mlp_w1[1], mlp_w2[1], post_mlp_g[1], post_mlp_b[1])
```

```python
import functools
import math

import jax
import jax.numpy as jnp
import numpy as np
from jax import lax
from jax.experimental import pallas as pl
from jax.experimental.pallas import tpu as pltpu

DEPTH = 2
CONV_WIDTH = 31
N_HEADS = 8
HEAD_DIM = 64
V_DIM = 2 * HEAD_DIM
REL_BUCKETS = 32
REL_MAX_DIST = 128
ALPHA = (2 * DEPTH) ** 0.25
LN_EPS = 1e-5
ATTN_LAYER = 1
LAMBDA_INIT = 0.8 - 0.6 * math.exp(-0.3 * ATTN_LAYER)
LOG2_E = math.log2(math.e)

V7X_SUBLANES = 8
V7X_LANES = 128
V7X_VMEM_BYTES = 64 * 1024 * 1024
V7X_VMEM_RESERVE_BYTES = 8 * 1024 * 1024

MASK_VALUE = -1e30
CONV_HALO = 32
CONV_ROWS = 512
SEQ_TILE = 1024
CONV_TILE = 1024
ATTN_TILE = 512
FF_CHUNK = 1024

F32 = jnp.float32
BF16 = jnp.bfloat16


def _t5_thresholds():
    max_exact = REL_BUCKETS // 2
    buckets = []
    for n in range(2 * REL_MAX_DIST):
        if n < max_exact:
            buckets.append(n)
        else:
            v = math.log(n / max_exact) / math.log(REL_MAX_DIST / max_exact) * (REL_BUCKETS - max_exact)
            buckets.append(min(max_exact + int(v), REL_BUCKETS - 1))
    assert all(b1 >= b0 for b0, b1 in zip(buckets, buckets[1:]))
    assert buckets[-1] == REL_BUCKETS - 1
    return [buckets.index(b) for b in range(REL_BUCKETS)]


T5_THRESHOLDS = _t5_thresholds()
T5_LAST_BUCKET_START = T5_THRESHOLDS[REL_BUCKETS - 1]


def _vmem_limit(nbytes):
    return int(min(nbytes, V7X_VMEM_BYTES - V7X_VMEM_RESERVE_BYTES))


def _layer_norm(z, g, b):
    mu = jnp.mean(z, axis=-1, keepdims=True)
    zc = z - mu
    var = jnp.mean(zc * zc, axis=-1, keepdims=True)
    return zc * lax.rsqrt(var + LN_EPS) * g + b


def _const_spec(shape, single_buffer=False):
    nd = len(shape)
    kwargs = {"pipeline_mode": pl.Buffered(1)} if single_buffer else {}
    return pl.BlockSpec(shape, lambda *_: (0,) * nd, **kwargs)


def _ada_mod_kernel(c_ref, w_ref, b_ref, o_ref):
    c = c_ref[...]
    sc = c * jax.nn.sigmoid(c)
    o_ref[0] = jnp.dot(sc, w_ref[0], preferred_element_type=F32,
                       precision=lax.Precision.HIGHEST) + b_ref[0]


def _ada_mod(c, w, b):
    n, d, d3 = w.shape
    bsz = c.shape[0]
    nblk = d3 // d
    out = pl.pallas_call(
        _ada_mod_kernel,
        grid=(n, nblk),
        in_specs=[
            pl.BlockSpec((bsz, d), lambda i, j: (0, 0)),
            pl.BlockSpec((1, d, d), lambda i, j: (i, 0, j)),
            pl.BlockSpec((1, 1, d), lambda i, j: (i, 0, j)),
        ],
        out_specs=pl.BlockSpec((1, bsz, d), lambda i, j: (i, 0, j)),
        out_shape=jax.ShapeDtypeStruct((n, bsz, d3), F32),
        compiler_params=pltpu.CompilerParams(
            dimension_semantics=("arbitrary", "arbitrary"),
            vmem_limit_bytes=_vmem_limit(32 * 1024 * 1024)),
        name="ada_mod",
    )(c, w, b.reshape(n, 1, d3))
    return out.reshape(n, bsz, 3, d)


def _conv_mixer_kernel(x_ref, mod_ref, w1_ref, b1_ref, wdw_ref, bdw_ref, gcn_ref, bcn_ref,
                       w2_ref, b2_ref, pg_ref, pb_ref, o_ref, ext_ref, y_ref):
    ts, d = x_ref.shape[1], x_ref.shape[2]
    n_lb = ext_ref.shape[0]
    s_idx = pl.program_id(1)

    @pl.when(s_idx == 0)
    def _():
        ext_ref[:, 0:CONV_HALO, :] = jnp.zeros((n_lb, CONV_HALO, V7X_LANES), F32)

    mod = mod_ref[0]
    shift, scale, gate = mod[0:1], mod[1:2], mod[2:3]
    x = x_ref[0]
    h = (x * (1.0 + scale) + shift).astype(BF16)
    a = jnp.dot(h, w1_ref[...], preferred_element_type=F32) + b1_ref[...]
    u = a[:, :d] * jax.nn.sigmoid(a[:, d:])
    for lb in range(n_lb):
        ext_ref[lb, CONV_HALO:CONV_HALO + ts, :] = u[:, lb * V7X_LANES:(lb + 1) * V7X_LANES]

    off0 = CONV_HALO - (CONV_WIDTH - 1)

    n_grp = CONV_ROWS // V7X_SUBLANES

    def row_block(rb, carry):
        t0 = pl.multiple_of(rb * CONV_ROWS, CONV_ROWS)

        def lane_block(lb, carry2):
            w_all = wdw_ref[lb]
            acc = [jnp.broadcast_to(bdw_ref[lb], (V7X_SUBLANES, V7X_LANES))] * n_grp
            for r in range(V7X_SUBLANES):
                taps = [j for j in range(CONV_WIDTH) if (off0 + j) % V7X_SUBLANES == r]
                first = off0 + taps[0]
                n_load = n_grp + (taps[-1] - taps[0]) // V7X_SUBLANES
                groups = [ext_ref[lb, pl.ds(t0 + (first + V7X_SUBLANES * g), V7X_SUBLANES), :]
                          for g in range(n_load)]
                for j in taps:
                    w_row = jnp.broadcast_to(w_all[j:j + 1], (V7X_SUBLANES, V7X_LANES))
                    g0 = (off0 + j - first) // V7X_SUBLANES
                    acc = [acc[i] + w_row * groups[g0 + i] for i in range(n_grp)]
            y_ref[lb, pl.ds(t0, CONV_ROWS), :] = jnp.concatenate(acc, axis=0)
            return carry2

        lax.fori_loop(0, n_lb, lane_block, 0)
        return carry

    lax.fori_loop(0, ts // CONV_ROWS, row_block, 0)
    ext_ref[:, 0:CONV_HALO, :] = ext_ref[:, ts:ts + CONV_HALO, :]

    y = y_ref[...]
    mu = jnp.sum(jnp.sum(y, axis=0, keepdims=True), axis=2, keepdims=True) * (1.0 / d)
    yc = y - mu
    var = jnp.sum(jnp.sum(yc * yc, axis=0, keepdims=True), axis=2, keepdims=True) * (1.0 / d)
    y = yc * lax.rsqrt(var + LN_EPS) * gcn_ref[...] + bcn_ref[...]
    y = (y * jax.nn.sigmoid(y)).astype(BF16)
    v = jnp.concatenate([y[lb] for lb in range(n_lb)], axis=-1)
    y = jnp.dot(v, w2_ref[...], preferred_element_type=F32) + b2_ref[...]
    o_ref[0] = _layer_norm(ALPHA * x_ref[0] + gate * y, pg_ref[...], pb_ref[...])


def _conv_mixer(x, mod, w1, b1, wdw, bdw, gcn, bcn, w2, b2, pg, pb):
    bsz, s, d = x.shape
    ts = CONV_TILE
    n_lb = d // V7X_LANES
    row = lambda v: v.reshape(1, -1)
    slab = lambda v: v.reshape(-1, n_lb, V7X_LANES).transpose(1, 0, 2)
    return pl.pallas_call(
        _conv_mixer_kernel,
        grid=(bsz, s // ts),
        in_specs=[
            pl.BlockSpec((1, ts, d), lambda b, i: (b, i, 0)),
            pl.BlockSpec((1, 3, d), lambda b, i: (b, 0, 0)),
            _const_spec((d, 2 * d), True),
            _const_spec((1, 2 * d)),
            _const_spec((n_lb, CONV_WIDTH, V7X_LANES)),
            _const_spec((n_lb, 1, V7X_LANES)),
            _const_spec((n_lb, 1, V7X_LANES)),
            _const_spec((n_lb, 1, V7X_LANES)),
            _const_spec((d, d), True),
            _const_spec((1, d)),
            _const_spec((1, d)),
            _const_spec((1, d)),
        ],
        out_specs=pl.BlockSpec((1, ts, d), lambda b, i: (b, i, 0)),
        out_shape=jax.ShapeDtypeStruct((bsz, s, d), F32),
        scratch_shapes=[
            pltpu.VMEM((n_lb, ts + CONV_HALO, V7X_LANES), F32),
            pltpu.VMEM((n_lb, ts, V7X_LANES), F32),
        ],
        compiler_params=pltpu.CompilerParams(
            dimension_semantics=("arbitrary", "arbitrary"),
            vmem_limit_bytes=_vmem_limit(56 * 1024 * 1024)),
        name="conv_mixer",
    )(x, mod, w1.astype(BF16), row(b1), slab(wdw), slab(bdw), slab(gcn), slab(bcn),
      w2.astype(BF16), row(b2), row(pg), row(pb))


def _mlp_sublayer(x, mod, w1_ref, w2_ref, pg_ref, pb_ref):
    d_ff = w1_ref.shape[1]
    shift, scale, gate = mod[0:1], mod[1:2], mod[2:3]
    h = (x * (1.0 + scale) + shift).astype(BF16)
    y = None
    for c0 in range(0, d_ff, FF_CHUNK):
        a = jnp.dot(h, w1_ref[:, c0:c0 + FF_CHUNK], preferred_element_type=F32)
        a = jnp.maximum(a, 0.0)
        part = jnp.dot((a * a).astype(BF16), w2_ref[c0:c0 + FF_CHUNK, :], preferred_element_type=F32)
        y = part if y is None else y + part
    return _layer_norm(ALPHA * x + gate * y, pg_ref[...], pb_ref[...])


def _mlp_kernel(x_ref, mod_ref, w1_ref, w2_ref, pg_ref, pb_ref, o_ref):
    o_ref[0] = _mlp_sublayer(x_ref[0], mod_ref[0], w1_ref, w2_ref, pg_ref, pb_ref)


def _mlp(x, mod, w1, w2, pg, pb):
    bsz, s, d = x.shape
    d_ff = w1.shape[1]
    ts = SEQ_TILE
    row = lambda v: v.reshape(1, -1)
    return pl.pallas_call(
        _mlp_kernel,
        grid=(bsz, s // ts),
        in_specs=[
            pl.BlockSpec((1, ts, d), lambda b, i: (b, i, 0)),
            pl.BlockSpec((1, 3, d), lambda b, i: (b, 0, 0)),
            _const_spec((d, d_ff), True),
            _const_spec((d_ff, d), True),
            _const_spec((1, d)),
            _const_spec((1, d)),
        ],
        out_specs=pl.BlockSpec((1, ts, d), lambda b, i: (b, i, 0)),
        out_shape=jax.ShapeDtypeStruct((bsz, s, d), F32),
        compiler_params=pltpu.CompilerParams(
            dimension_semantics=("arbitrary", "arbitrary"),
            vmem_limit_bytes=_vmem_limit(52 * 1024 * 1024)),
        name="sq_relu_mlp",
    )(x, mod, w1.astype(BF16), w2.astype(BF16), row(pg), row(pb))


def _qkv_kernel(x_ref, mod_ref, wq_ref, wk_ref, wvt_ref, qc_ref, k_ref, vt_ref):
    mod = mod_ref[0]
    shift, scale = mod[0:1], mod[1:2]
    h = (x_ref[0] * (1.0 + scale) + shift).astype(BF16)
    q = (jnp.dot(h, wq_ref[...], preferred_element_type=F32) * (HEAD_DIM ** -0.5 * LOG2_E)).astype(BF16)
    n_cb, cb, hw = qc_ref.shape[3:]
    lane = lax.broadcasted_iota(jnp.int32, (q.shape[0], hw), 1)
    for hd in range(qc_ref.shape[2]):
        qh = q[:, hd * hw:(hd + 1) * hw]
        zero = jnp.zeros_like(qh)
        qcat = jnp.concatenate([jnp.where(lane < HEAD_DIM, qh, zero),
                                jnp.where(lane >= HEAD_DIM, qh, zero)], axis=0)
        qc_ref[0, 0, hd] = qcat.reshape(n_cb, cb, hw)
    k_ref[0] = jnp.dot(h, wk_ref[...], preferred_element_type=F32).astype(BF16)
    vt = lax.dot_general(wvt_ref[...], h, (((1,), (1,)), ((), ())), preferred_element_type=F32)
    vt_ref[0, 0] = vt.astype(BF16)


def _qkv(x, mod, w_qkv):
    bsz, s, d = x.shape
    tk = ATTN_TILE
    cb, hw = ATTN_COL_BLOCK, 2 * HEAD_DIM
    n_cb = 2 * tk // cb
    dq = N_HEADS * hw
    dv = N_HEADS * V_DIM
    wq = w_qkv[:, :dq].astype(BF16)
    wk = w_qkv[:, dq:2 * dq].astype(BF16)
    wvt = w_qkv[:, 2 * dq:].T.astype(BF16)
    return pl.pallas_call(
        _qkv_kernel,
        grid=(bsz, s // tk),
        in_specs=[
            pl.BlockSpec((1, tk, d), lambda b, i: (b, i, 0)),
            pl.BlockSpec((1, 3, d), lambda b, i: (b, 0, 0)),
            _const_spec((d, dq), True),
            _const_spec((d, dq), True),
            _const_spec((dv, d), True),
        ],
        out_specs=[
            pl.BlockSpec((1, 1, N_HEADS, n_cb, cb, hw), lambda b, i: (b, i, 0, 0, 0, 0)),
            pl.BlockSpec((1, tk, dq), lambda b, i: (b, i, 0)),
            pl.BlockSpec((1, 1, dv, tk), lambda b, i: (b, i, 0, 0)),
        ],
        out_shape=[
            jax.ShapeDtypeStruct((bsz, s // tk, N_HEADS, n_cb, cb, hw), BF16),
            jax.ShapeDtypeStruct((bsz, s, dq), BF16),
            jax.ShapeDtypeStruct((bsz, s // tk, dv, tk), BF16),
        ],
        compiler_params=pltpu.CompilerParams(
            dimension_semantics=("arbitrary", "arbitrary"),
            vmem_limit_bytes=_vmem_limit(40 * 1024 * 1024)),
        name="qkv_proj",
    )(x, mod, wq, wk, wvt)


def _bias_tiles_kernel(rb_ref, o_ref):
    h = pl.program_id(0)
    t = o_ref.shape[2]
    far = rb_ref[h, REL_BUCKETS - 1]
    kk = lax.broadcasted_iota(jnp.int32, (t, t), 0)
    qq = lax.broadcasted_iota(jnp.int32, (t, t), 1)
    for dlt in range(o_ref.shape[1]):
        rel = dlt * t + qq - kk
        val = jnp.full((t, t), (rb_ref[h, 0] - far) * LOG2_E, F32)
        for bkt in range(1, REL_BUCKETS):
            val = jnp.where(rel >= T5_THRESHOLDS[bkt], (rb_ref[h, bkt] - far) * LOG2_E, val)
        o_ref[0, dlt] = jnp.where(rel >= 0, val, MASK_VALUE)


def _bias_tiles(rel_bias):
    t = ATTN_TILE
    n_near = -(-(T5_LAST_BUCKET_START + t - 1) // t)
    return pl.pallas_call(
        _bias_tiles_kernel,
        grid=(N_HEADS,),
        in_specs=[pl.BlockSpec(memory_space=pltpu.SMEM)],
        out_specs=pl.BlockSpec((1, n_near, t, t), lambda h: (h, 0, 0, 0)),
        out_shape=jax.ShapeDtypeStruct((N_HEADS, n_near, t, t), F32),
        compiler_params=pltpu.CompilerParams(dimension_semantics=("arbitrary",)),
        name="t5_bias_tiles",
    )(rel_bias.T)


_TAB_QI, _TAB_KJ = range(2)
_PIPE_DEPTH = 1
_UNROLL_CHOICES = (20, 18, 16, 14, 12)
ATTN_COL_BLOCK = 256
ATTN_STAGE_SKEW = 1


def _attn_schedule(pairs, inert):
    cols = [inert] * _PIPE_DEPTH + list(pairs) + [inert] * _PIPE_DEPTH
    n_ticks = len(cols) - _PIPE_DEPTH
    unroll = min(_UNROLL_CHOICES, key=lambda u: (-n_ticks % u, -u))
    cols += [inert] * (-n_ticks % unroll)
    return np.asarray(cols, np.int32).T, unroll


_NEAR_PERIOD = 2


def _near_distance(col):
    return (col + 1) % _NEAR_PERIOD


def _near_extent(dist, c, t, cb):
    if dist is None:
        return t, t
    q_lo = (c * cb) % t
    if dist == 0:
        return min(t, q_lo + cb), 0
    lo = dist * t + q_lo - T5_LAST_BUCKET_START + 1
    return t, min(t, max(0, lo // V7X_SUBLANES * V7X_SUBLANES))


def _attn_schedules(nq, n_near):
    assert n_near == _NEAR_PERIOD
    far = [(qi, kj) for qi in range(nq) for kj in range(qi - n_near + 1)]
    near = [(qi, qi - d) for qi in range(nq) for d in reversed(range(min(n_near, qi + 1)))]
    assert all(qi - kj == _near_distance(_PIPE_DEPTH + i) for i, (qi, kj) in enumerate(near))
    near_tab, near_unroll = _attn_schedule(near, (nq, 0))
    visited = range(near_tab.shape[1] - _PIPE_DEPTH)
    assert all(near_tab[_TAB_QI, j] < nq for j in visited if _near_distance(j) == 0)
    return _attn_schedule(far, (nq, 0)), (near_tab, near_unroll)


def _attn_kernel(far_ref, near_ref, qc_ref, k_ref, vt_ref, bias_ref, lam_ref, g_ref, o_ref,
                 s0_ref, s1_ref, mt0_ref, mt1_ref, m_ref, l_ref, acc_ref,
                 *, far_unroll, near_unroll):
    s_refs, mt_refs = (s0_ref, s1_ref), (mt0_ref, mt1_ref)
    n_cb, t, cb = s0_ref.shape
    nq = m_ref.shape[0] - 1

    def reset_state(idx):
        m_ref[idx] = jnp.full(m_ref.shape[1:], MASK_VALUE, F32)
        l_ref[idx] = jnp.zeros(l_ref.shape[1:], F32)
        acc_ref[idx] = jnp.zeros(acc_ref.shape[1:], F32)

    @pl.when((pl.program_id(0) == 0) & (pl.program_id(1) == 0))
    def _():
        s_refs[1][...] = jnp.zeros(s_refs[1].shape, F32)
        mt_refs[1][...] = jnp.zeros(mt_refs[1].shape, F32)
        for idx in range(nq + 1):
            reset_state(idx)

    def q_tile_index(qi):
        return jnp.where(qi == nq, 0, qi)

    lam_v = lam_ref[...]
    lam = (jnp.exp(jnp.sum(lam_v[0:1] * lam_v[1:2], axis=-1, keepdims=True))
           - jnp.exp(jnp.sum(lam_v[2:3] * lam_v[3:4], axis=-1, keepdims=True)) + LAMBDA_INIT)

    def finish(qi):
        def normalised(c):
            return acc_ref[qi, c] * (1.0 / l_ref[qi, c])
        half = n_cb // 2
        o = jnp.concatenate([normalised(c) - lam * normalised(half + c) for c in range(half)],
                            axis=1).T
        o = (o * lax.rsqrt(jnp.mean(o * o, axis=-1, keepdims=True) + LN_EPS)
             * g_ref[...] * (1.0 - LAMBDA_INIT))
        o_ref[0, pl.ds(pl.multiple_of(qi * t, t), t), :] = o.astype(BF16)
        reset_state(qi)

    def tick(tab_ref, n, u, near):
        col_a, col_b = n + 1, n
        par, cur = u % 2, 1 - u % 2
        dist_a, dist_b = (_near_distance(u + 1), _near_distance(u)) if near else (None, None)

        qi_a = q_tile_index(tab_ref[_TAB_QI, col_a])
        qi_b = tab_ref[_TAB_QI, col_b]
        vt = vt_ref[0, tab_ref[_TAB_KJ, col_b]]
        k = k_ref[0, pl.ds(pl.multiple_of(tab_ref[_TAB_KJ, col_a] * t, t), t), :]

        def stage_a(c):
            keys_a, bias_lo = _near_extent(dist_a, c, t, cb)
            st = lax.dot_general(k[:keys_a], qc_ref[0, qi_a, 0, c], (((1,), (1,)), ((), ())),
                                 preferred_element_type=F32)
            if bias_lo < keys_a:
                bias = bias_ref[0, dist_a, bias_lo:keys_a, pl.ds((c * cb) % t, cb)]
                st = jnp.concatenate([st[:bias_lo], st[bias_lo:] + bias], axis=0) if bias_lo else st + bias
            s_refs[par][c, 0:keys_a] = st
            mt_refs[par][c] = jnp.max(st, axis=0, keepdims=True)

        def stage_b(c):
            keys_b, _ = _near_extent(dist_b, c, t, cb)
            m_old = m_ref[qi_b, c]
            m_new = jnp.maximum(m_old, mt_refs[cur][c])
            m_ref[qi_b, c] = m_new
            p = jnp.exp2(s_refs[cur][c, 0:keys_b] - m_new)
            pv = jnp.dot(vt[:, :keys_b], p.astype(BF16), preferred_element_type=F32)
            alpha = jnp.exp2(m_old - m_new)
            acc_ref[qi_b, c] = alpha * acc_ref[qi_b, c] + pv
            l_ref[qi_b, c] = alpha * l_ref[qi_b, c] + jnp.sum(p, axis=0, keepdims=True)

        for c in range(n_cb + ATTN_STAGE_SKEW):
            if c < n_cb:
                stage_a(c)
            if c >= ATTN_STAGE_SKEW:
                stage_b(c - ATTN_STAGE_SKEW)
        if dist_b == 0:
            finish(qi_b)

    def run(tab_ref, unroll, near):
        n_ticks = tab_ref.shape[1] - _PIPE_DEPTH
        assert n_ticks % unroll == 0 and unroll % _NEAR_PERIOD == 0

        def body(i, carry):
            for u in range(unroll):
                tick(tab_ref, unroll * i + u, u, near)
            return carry

        lax.fori_loop(0, n_ticks // unroll, body, 0)

    run(far_ref, far_unroll, False)
    run(near_ref, near_unroll, True)


def _attention(qc, k, vt, bias, lam_vecs, g_sub):
    bsz, nq, _, n_cb, cb, hw = qc.shape
    s = k.shape[1]
    t = s // nq
    n_near = bias.shape[1]
    (far_tab, far_unroll), (near_tab, near_unroll) = _attn_schedules(nq, n_near)
    return pl.pallas_call(
        functools.partial(_attn_kernel, far_unroll=far_unroll, near_unroll=near_unroll),
        grid=(bsz, N_HEADS),
        in_specs=[
            pl.BlockSpec(memory_space=pltpu.SMEM),
            pl.BlockSpec(memory_space=pltpu.SMEM),
            pl.BlockSpec((1, nq, 1, n_cb, cb, hw), lambda b, h: (b, 0, h, 0, 0, 0)),
            pl.BlockSpec((1, s, hw), lambda b, h: (b, 0, h)),
            pl.BlockSpec((1, nq, V_DIM, t), lambda b, h: (b, 0, h, 0)),
            pl.BlockSpec((1, n_near, t, t), lambda b, h: (h, 0, 0, 0)),
            _const_spec((4, HEAD_DIM)),
            _const_spec((1, V_DIM)),
        ],
        out_specs=pl.BlockSpec((1, s, V_DIM), lambda b, h: (b, 0, h)),
        out_shape=jax.ShapeDtypeStruct((bsz, s, N_HEADS * V_DIM), BF16),
        scratch_shapes=[
            pltpu.VMEM((n_cb, t, cb), F32),
            pltpu.VMEM((n_cb, t, cb), F32),
            pltpu.VMEM((n_cb, 1, cb), F32),
            pltpu.VMEM((n_cb, 1, cb), F32),
            pltpu.VMEM((nq + 1, n_cb, 1, cb), F32),
            pltpu.VMEM((nq + 1, n_cb, 1, cb), F32),
            pltpu.VMEM((nq + 1, n_cb, V_DIM, cb), F32),
        ],
        compiler_params=pltpu.CompilerParams(
            dimension_semantics=("arbitrary", "arbitrary"),
            vmem_limit_bytes=_vmem_limit(52 * 1024 * 1024)),
        name="diff_attention",
    )(jnp.asarray(far_tab), jnp.asarray(near_tab), qc, k, vt, bias, lam_vecs,
      g_sub.reshape(1, V_DIM))


def _out_proj_mlp_kernel(a_ref, x_ref, amod_ref, wo_ref, mg_ref, mb_ref,
                         mod_ref, w1_ref, w2_ref, pg_ref, pb_ref, o_ref):
    gate = amod_ref[0][2:3]
    y = jnp.dot(a_ref[0], wo_ref[...], preferred_element_type=F32)
    x = _layer_norm(ALPHA * x_ref[0] + gate * y, mg_ref[...], mb_ref[...])
    o_ref[0] = _mlp_sublayer(x, mod_ref[0], w1_ref, w2_ref, pg_ref, pb_ref)


def _out_proj_mlp(a, x, amod, wo, mg, mb, mod, w1, w2, pg, pb):
    bsz, s, d = x.shape
    da = a.shape[2]
    d_ff = w1.shape[1]
    ts = SEQ_TILE
    row = lambda v: v.reshape(1, -1)
    return pl.pallas_call(
        _out_proj_mlp_kernel,
        grid=(bsz, s // ts),
        in_specs=[
            pl.BlockSpec((1, ts, da), lambda b, i: (b, i, 0)),
            pl.BlockSpec((1, ts, d), lambda b, i: (b, i, 0)),
            pl.BlockSpec((1, 3, d), lambda b, i: (b, 0, 0)),
            _const_spec((da, d), True),
            _const_spec((1, d)),
            _const_spec((1, d)),
            pl.BlockSpec((1, 3, d), lambda b, i: (b, 0, 0)),
            _const_spec((d, d_ff), True),
            _const_spec((d_ff, d), True),
            _const_spec((1, d)),
            _const_spec((1, d)),
        ],
        out_specs=pl.BlockSpec((1, ts, d), lambda b, i: (b, i, 0)),
        out_shape=jax.ShapeDtypeStruct((bsz, s, d), F32),
        compiler_params=pltpu.CompilerParams(
            dimension_semantics=("arbitrary", "arbitrary"),
            vmem_limit_bytes=_vmem_limit(54 * 1024 * 1024)),
        name="out_proj_mlp",
    )(a, x, amod, wo.astype(BF16), row(mg), row(mb),
      mod, w1.astype(BF16), w2.astype(BF16), row(pg), row(pb))


def kernel(x, c, conv_mod_w, conv_mod_b, conv_pw1_w, conv_pw1_b, conv_dw_w, conv_dw_b, conv_norm_g, conv_norm_b, conv_pw2_w, conv_pw2_b, attn_mod_w, attn_mod_b, attn_qkv_w, attn_lam_q1, attn_lam_k1, attn_lam_q2, attn_lam_k2, attn_subln_g, attn_out_w, rel_bias, mlp_mod_w, mlp_mod_b, mlp_w1, mlp_w2, post_mix_g, post_mix_b, post_mlp_g, post_mlp_b):
    assert x.shape[1] % SEQ_TILE == 0 and x.shape[1] % ATTN_TILE == 0
    assert x.shape[1] % CONV_TILE == 0 and CONV_TILE % CONV_ROWS == 0 and CONV_HALO >= CONV_WIDTH - 1
    conv_mod = _ada_mod(c, conv_mod_w, conv_mod_b)
    attn_mod = _ada_mod(c, attn_mod_w, attn_mod_b)
    mlp_mod = _ada_mod(c, mlp_mod_w, mlp_mod_b)

    x = _conv_mixer(x, conv_mod[0], conv_pw1_w[0], conv_pw1_b[0], conv_dw_w[0], conv_dw_b[0],
                    conv_norm_g[0], conv_norm_b[0], conv_pw2_w[0], conv_pw2_b[0],
                    post_mix_g[0], post_mix_b[0])
    x = _mlp(x, mlp_mod[0], mlp_w1[0], mlp_w2[0], post_mlp_g[0], post_mlp_b[0])

    qc, k, vt = _qkv(x, attn_mod[0], attn_qkv_w[0])
    bias = _bias_tiles(rel_bias)
    lam_vecs = jnp.stack([attn_lam_q1[0], attn_lam_k1[0], attn_lam_q2[0], attn_lam_k2[0]])
    a = _attention(qc, k, vt, bias, lam_vecs, attn_subln_g[0])
    return _out_proj_mlp(a, x, attn_mod[0], attn_out_w[0], post_mix_g[1], post_mix_b[1],
                         mlp_mod[1], mlp_w1[1], mlp_w2[1], post_mlp_g[1], post_mlp_b[1])
```
